```python
import jax, jax.numpy as jnp
from jax import lax
import numpy as np

D_MODEL = 2048
BATCH = 8
SEQ = 4096
DEPTH = 1

CHUNK = 64

GLA_HEADS = 4
GLA_DK = 128
GLA_DV = 256
GLA_QK = GLA_HEADS * GLA_DK
GLA_V = GLA_HEADS * GLA_DV
GLA_LORA = 16
GLA_TAU = 16.0

RWKV_HEADS = 16
RWKV_HD = 64
RWKV_W = RWKV_HEADS * RWKV_HD
DECAY_LORA = 96
AAA_LORA = 96
GATE_LORA = 256
GN_EPS = 64e-5

N_BRANCH = 2
D_FF = 5632
NORM_EPS = 1e-6

GLA_SPLITS = (GLA_QK, GLA_QK, GLA_V, GLA_V, GLA_LORA)
GLA_IN = 2 * GLA_QK + 2 * GLA_V + GLA_LORA
RWKV_SPLITS = (RWKV_W, RWKV_W, RWKV_W, DECAY_LORA, AAA_LORA, GATE_LORA)
RWKV_IN = 3 * RWKV_W + DECAY_LORA + AAA_LORA + GATE_LORA
D_IN = GLA_IN + RWKV_IN + N_BRANCH * D_MODEL
BRANCH_IN = GLA_V + RWKV_W

kernel_name = "hybrid_gla_rwkv7_macaron_block"


def _split(p, sizes):
    out, off = [], 0
    for s in sizes:
        out.append(p[..., off:off + s])
        off += s
    return out


def rmsnorm(x, g):
    xf = x.astype(jnp.float32)
    y = xf * lax.rsqrt(jnp.mean(xf * xf, axis=-1, keepdims=True) + NORM_EPS)
    return (y * g.astype(jnp.float32)).astype(x.dtype)


def swiglu(h, wg, wu, wd):
    return (jax.nn.silu(h @ wg) * (h @ wu)) @ wd


def token_shift(p, mu):
    prev = jnp.pad(p, ((0, 0), (1, 0), (0, 0)))[:, :-1]
    return p + mu * (prev - p)


def gla_branch(q, k, v, r, a_down, w_a2, b_a, gn_w):
    B, S, _ = q.shape
    nc = S // CHUNK
    f32 = jnp.float32
    log_alpha = jax.nn.log_sigmoid(a_down.astype(f32) @ w_a2.astype(f32) + b_a.astype(f32)) / GLA_TAU
    shp = (B, nc, CHUNK, GLA_HEADS, GLA_DK)
    qf = q.astype(f32).reshape(shp) * (GLA_DK ** -0.5)
    kf = k.astype(f32).reshape(shp)
    vf = v.astype(f32).reshape(B, nc, CHUNK, GLA_HEADS, GLA_DV)
    cum = jnp.cumsum(log_alpha.reshape(shp), axis=2)
    total = cum[:, :, -1]
    kdec = kf * jnp.exp(total[:, :, None] - cum)
    u = jnp.einsum('bnchk,bnchv->nbhkv', kdec, vf)

    def step(state, inp):
        lt, uc = inp
        state = jnp.exp(lt)[..., None] * state + uc
        return state, state

    s0 = jnp.zeros((B, GLA_HEADS, GLA_DK, GLA_DV), f32)
    _, states = lax.scan(step, s0, (jnp.moveaxis(total, 1, 0), u))
    o = jnp.einsum('bnchk,nbhkv->bnchv', qf, states)
    o = o * lax.rsqrt(jnp.mean(o * o, axis=-1, keepdims=True) + NORM_EPS) * gn_w.astype(f32)
    o = o.reshape(B, S, GLA_V) * jax.nn.silu(r.astype(f32))
    return o.astype(q.dtype)


def rwkv7_branch(r, k, v, wd, ad, gd, w0, w_w2, a0, w_a2, w_g2, k_k, k_a, r_k, lnx_w, lnx_b):
    B, S, _ = r.shape
    f32 = jnp.float32
    r, k, v = r.astype(f32), k.astype(f32), v.astype(f32)
    w_raw = w0.astype(f32) + jnp.tanh(wd.astype(f32)) @ w_w2.astype(f32)
    log_w = -jnp.exp(-jax.nn.softplus(-w_raw) - 0.5)
    a = jax.nn.sigmoid(a0.astype(f32) + ad.astype(f32) @ w_a2.astype(f32))
    g = jax.nn.sigmoid(gd.astype(f32)) @ w_g2.astype(f32)
    hs = (B, S, RWKV_HEADS, RWKV_HD)
    kk = (k * k_k.astype(f32)).reshape(hs)
    kk = kk / jnp.maximum(jnp.linalg.norm(kk, axis=-1, keepdims=True), 1e-12)
    k = k * (1.0 + (a - 1.0) * k_a.astype(f32))
    rh, kh, vh, ah = r.reshape(hs), k.reshape(hs), v.reshape(hs), a.reshape(hs)
    decay = jnp.exp(log_w).reshape(hs)
    b = kk * ah

    def step(state, inp):
        r_t, w_t, k_t, v_t, kk_t, b_t = inp
        sa = jnp.einsum('bhvk,bhk->bhv', state, kk_t)
        state = (state * w_t[:, :, None, :] - sa[..., None] * b_t[:, :, None, :]
                 + v_t[..., None] * k_t[:, :, None, :])
        return state, jnp.einsum('bhvk,bhk->bhv', state, r_t)

    xs = tuple(jnp.moveaxis(t, 1, 0) for t in (rh, decay, kh, vh, kk, b))
    s0 = jnp.zeros((B, RWKV_HEADS, RWKV_HD, RWKV_HD), f32)
    _, y = lax.scan(step, s0, xs)
    y = jnp.moveaxis(y, 0, 1)
    mu = jnp.mean(y, axis=-1, keepdims=True)
    var = jnp.mean(jnp.square(y - mu), axis=-1, keepdims=True)
    yn = ((y - mu) * lax.rsqrt(var + GN_EPS)).reshape(B, S, RWKV_W)
    yn = yn * lnx_w.astype(f32) + lnx_b.astype(f32)
    bonus = (jnp.sum(rh * kh * r_k.astype(f32), axis=-1, keepdims=True) * vh).reshape(B, S, RWKV_W)
    return ((yn + bonus) * g).astype(r.dtype)


def hybrid_mixer(h, w_in, gla_w_a2, gla_b_a, gla_gn_w, rwkv_mu, rwkv_w0, rwkv_w_w2,
                 rwkv_a0, rwkv_w_a2, rwkv_w_g2, rwkv_k_k, rwkv_k_a, rwkv_r_k,
                 rwkv_lnx_w, rwkv_lnx_b, gate_b, w_branch, w_out):
    p = h @ w_in
    gla_p = p[..., :GLA_IN]
    rw_p = token_shift(p[..., GLA_IN:GLA_IN + RWKV_IN], rwkv_mu)
    gate_p = p[..., GLA_IN + RWKV_IN:]
    gq, gk, gv, gr, gad = _split(gla_p, GLA_SPLITS)
    rr, rk, rv, rwd, rad, rgd = _split(rw_p, RWKV_SPLITS)
    o_gla = gla_branch(gq, gk, gv, gr, gad, gla_w_a2, gla_b_a, gla_gn_w)
    o_rw = rwkv7_branch(rr, rk, rv, rwd, rad, rgd, rwkv_w0, rwkv_w_w2, rwkv_a0, rwkv_w_a2,
                        rwkv_w_g2, rwkv_k_k, rwkv_k_a, rwkv_r_k, rwkv_lnx_w, rwkv_lnx_b)
    gates = jax.nn.sigmoid((gate_p + gate_b).astype(jnp.float32))
    y_gla = (o_gla @ w_branch[:GLA_V]).astype(jnp.float32)
    y_rw = (o_rw @ w_branch[GLA_V:]).astype(jnp.float32)
    merged = gates[..., :D_MODEL] * y_gla + gates[..., D_MODEL:] * y_rw
    return merged.astype(h.dtype) @ w_out


def _fwd_setup_inputs(seed: int = 0) -> dict:
    key = jax.random.key(seed)
    ks = iter(jax.random.split(key, 40))
    L, D = DEPTH, D_MODEL

    def nrm(shape, scale):
        return scale * jax.random.normal(next(ks), shape, jnp.float32)

    def gain(shape):
        return 1.0 + nrm(shape, 0.02)

    return {
        "x": nrm((BATCH, SEQ, D), 1.0),
        "ffn1_norm": gain((L, D)),
        "ffn1_wg": nrm((L, D, D_FF), D ** -0.5),
        "ffn1_wu": nrm((L, D, D_FF), D ** -0.5),
        "ffn1_wd": nrm((L, D_FF, D), D_FF ** -0.5),
        "mix_norm": gain((L, D)),
        "w_in": nrm((L, D, D_IN), D ** -0.5),
        "gla_w_a2": nrm((L, GLA_LORA, GLA_QK), GLA_LORA ** -0.5),
        "gla_b_a": nrm((L, GLA_QK), 0.1),
        "gla_gn_w": gain((L, GLA_DV)),
        "rwkv_mu": jax.random.uniform(next(ks), (L, RWKV_IN), jnp.float32, 0.0, 1.0),
        "rwkv_w0": -2.0 + nrm((L, RWKV_W), 0.5),
        "rwkv_w_w2": nrm((L, DECAY_LORA, RWKV_W), 0.3 * DECAY_LORA ** -0.5),
        "rwkv_a0": nrm((L, RWKV_W), 0.1),
        "rwkv_w_a2": nrm((L, AAA_LORA, RWKV_W), 0.3 * AAA_LORA ** -0.5),
        "rwkv_w_g2": nrm((L, GATE_LORA, RWKV_W), GATE_LORA ** -0.5),
        "rwkv_k_k": 0.85 + nrm((L, RWKV_W), 0.05),
        "rwkv_k_a": gain((L, RWKV_W)),
        "rwkv_r_k": nrm((L, RWKV_HEADS, RWKV_HD), 0.1),
        "rwkv_lnx_w": gain((L, RWKV_W)),
        "rwkv_lnx_b": nrm((L, RWKV_W), 0.01),
        "gate_b": nrm((L, N_BRANCH * D), 0.1),
        "w_branch": nrm((L, BRANCH_IN, D), GLA_V ** -0.5),
        "w_out": nrm((L, D, D), D ** -0.5),
        "ffn2_norm": gain((L, D)),
        "ffn2_wg": nrm((L, D, D_FF), D ** -0.5),
        "ffn2_wu": nrm((L, D, D_FF), D ** -0.5),
        "ffn2_wd": nrm((L, D_FF, D), D_FF ** -0.5),
        "final_norm": gain((D,)),
    }


def _fwd_reference(x, ffn1_norm, ffn1_wg, ffn1_wu, ffn1_wd, mix_norm, w_in, gla_w_a2, gla_b_a,
              gla_gn_w, rwkv_mu, rwkv_w0, rwkv_w_w2, rwkv_a0, rwkv_w_a2, rwkv_w_g2, rwkv_k_k,
              rwkv_k_a, rwkv_r_k, rwkv_lnx_w, rwkv_lnx_b, gate_b, w_branch, w_out,
              ffn2_norm, ffn2_wg, ffn2_wu, ffn2_wd, final_norm):
    for l in range(DEPTH):
        h = rmsnorm(x, ffn1_norm[l])
        x = x + 0.5 * swiglu(h, ffn1_wg[l], ffn1_wu[l], ffn1_wd[l])
        h = rmsnorm(x, mix_norm[l])
        x = x + hybrid_mixer(h, w_in[l], gla_w_a2[l], gla_b_a[l], gla_gn_w[l], rwkv_mu[l],
                             rwkv_w0[l], rwkv_w_w2[l], rwkv_a0[l], rwkv_w_a2[l], rwkv_w_g2[l],
                             rwkv_k_k[l], rwkv_k_a[l], rwkv_r_k[l], rwkv_lnx_w[l], rwkv_lnx_b[l],
                             gate_b[l], w_branch[l], w_out[l])
        h = rmsnorm(x, ffn2_norm[l])
        x = x + 0.5 * swiglu(h, ffn2_wg[l], ffn2_wu[l], ffn2_wd[l])
    return rmsnorm(x, final_norm)


import jax as _jax
import jax.numpy as _jnp

TWIN_FORMAT = 'train_step'
FWD_PARAMS = ['x', 'ffn1_norm', 'ffn1_wg', 'ffn1_wu', 'ffn1_wd', 'mix_norm', 'w_in', 'gla_w_a2', 'gla_b_a', 'gla_gn_w', 'rwkv_mu', 'rwkv_w0', 'rwkv_w_w2', 'rwkv_a0', 'rwkv_w_a2', 'rwkv_w_g2', 'rwkv_k_k', 'rwkv_k_a', 'rwkv_r_k', 'rwkv_lnx_w', 'rwkv_lnx_b', 'gate_b', 'w_branch', 'w_out', 'ffn2_norm', 'ffn2_wg', 'ffn2_wu', 'ffn2_wd', 'final_norm']
TWIN_WEIGHTS = ['ffn1_norm', 'ffn1_wg', 'ffn1_wu', 'ffn1_wd', 'mix_norm', 'w_in', 'gla_w_a2', 'gla_b_a', 'gla_gn_w', 'rwkv_mu', 'rwkv_w0', 'rwkv_w_w2', 'rwkv_a0', 'rwkv_w_a2', 'rwkv_w_g2', 'rwkv_k_k', 'rwkv_k_a', 'rwkv_r_k', 'rwkv_lnx_w', 'rwkv_lnx_b', 'gate_b', 'w_branch', 'w_out', 'ffn2_norm', 'ffn2_wg', 'ffn2_wu', 'ffn2_wd', 'final_norm']
TWIN_DIFF_INPUT = 'x'
TWIN_INPUTS = ['x', 'ffn1_norm', 'ffn1_wg', 'ffn1_wu', 'ffn1_wd', 'mix_norm', 'w_in', 'gla_w_a2', 'gla_b_a', 'gla_gn_w', 'rwkv_mu', 'rwkv_w0', 'rwkv_w_w2', 'rwkv_a0', 'rwkv_w_a2', 'rwkv_w_g2', 'rwkv_k_k', 'rwkv_k_a', 'rwkv_r_k', 'rwkv_lnx_w', 'rwkv_lnx_b', 'gate_b', 'w_branch', 'w_out', 'ffn2_norm', 'ffn2_wg', 'ffn2_wu', 'ffn2_wd', 'final_norm', 'loss_target', 'm_ffn1_norm', 'm_ffn1_wg', 'm_ffn1_wu', 'm_ffn1_wd', 'm_mix_norm', 'm_w_in', 'm_gla_w_a2', 'm_gla_b_a', 'm_gla_gn_w', 'm_rwkv_mu', 'm_rwkv_w0', 'm_rwkv_w_w2', 'm_rwkv_a0', 'm_rwkv_w_a2', 'm_rwkv_w_g2', 'm_rwkv_k_k', 'm_rwkv_k_a', 'm_rwkv_r_k', 'm_rwkv_lnx_w', 'm_rwkv_lnx_b', 'm_gate_b', 'm_w_branch', 'm_w_out', 'm_ffn2_norm', 'm_ffn2_wg', 'm_ffn2_wu', 'm_ffn2_wd', 'm_final_norm', 'v_ffn1_norm', 'v_ffn1_wg', 'v_ffn1_wu', 'v_ffn1_wd', 'v_mix_norm', 'v_w_in', 'v_gla_w_a2', 'v_gla_b_a', 'v_gla_gn_w', 'v_rwkv_mu', 'v_rwkv_w0', 'v_rwkv_w_w2', 'v_rwkv_a0', 'v_rwkv_w_a2', 'v_rwkv_w_g2', 'v_rwkv_k_k', 'v_rwkv_k_a', 'v_rwkv_r_k', 'v_rwkv_lnx_w', 'v_rwkv_lnx_b', 'v_gate_b', 'v_w_branch', 'v_w_out', 'v_ffn2_norm', 'v_ffn2_wg', 'v_ffn2_wu', 'v_ffn2_wd', 'v_final_norm']
TWIN_OUTPUTS = ['loss', 'grad_x', 'grad_ffn1_norm', 'grad_ffn1_wg', 'grad_ffn1_wu', 'grad_ffn1_wd', 'grad_mix_norm', 'grad_w_in', 'grad_gla_w_a2', 'grad_gla_b_a', 'grad_gla_gn_w', 'grad_rwkv_mu', 'grad_rwkv_w0', 'grad_rwkv_w_w2', 'grad_rwkv_a0', 'grad_rwkv_w_a2', 'grad_rwkv_w_g2', 'grad_rwkv_k_k', 'grad_rwkv_k_a', 'grad_rwkv_r_k', 'grad_rwkv_lnx_w', 'grad_rwkv_lnx_b', 'grad_gate_b', 'grad_w_branch', 'grad_w_out', 'grad_ffn2_norm', 'grad_ffn2_wg', 'grad_ffn2_wu', 'grad_ffn2_wd', 'grad_final_norm', 'delta_ffn1_norm', 'delta_ffn1_wg', 'delta_ffn1_wu', 'delta_ffn1_wd', 'delta_mix_norm', 'delta_w_in', 'delta_gla_w_a2', 'delta_gla_b_a', 'delta_gla_gn_w', 'delta_rwkv_mu', 'delta_rwkv_w0', 'delta_rwkv_w_w2', 'delta_rwkv_a0', 'delta_rwkv_w_a2', 'delta_rwkv_w_g2', 'delta_rwkv_k_k', 'delta_rwkv_k_a', 'delta_rwkv_r_k', 'delta_rwkv_lnx_w', 'delta_rwkv_lnx_b', 'delta_gate_b', 'delta_w_branch', 'delta_w_out', 'delta_ffn2_norm', 'delta_ffn2_wg', 'delta_ffn2_wu', 'delta_ffn2_wd', 'delta_final_norm', 'new_m_ffn1_norm', 'new_m_ffn1_wg', 'new_m_ffn1_wu', 'new_m_ffn1_wd', 'new_m_mix_norm', 'new_m_w_in', 'new_m_gla_w_a2', 'new_m_gla_b_a', 'new_m_gla_gn_w', 'new_m_rwkv_mu', 'new_m_rwkv_w0', 'new_m_rwkv_w_w2', 'new_m_rwkv_a0', 'new_m_rwkv_w_a2', 'new_m_rwkv_w_g2', 'new_m_rwkv_k_k', 'new_m_rwkv_k_a', 'new_m_rwkv_r_k', 'new_m_rwkv_lnx_w', 'new_m_rwkv_lnx_b', 'new_m_gate_b', 'new_m_w_branch', 'new_m_w_out', 'new_m_ffn2_norm', 'new_m_ffn2_wg', 'new_m_ffn2_wu', 'new_m_ffn2_wd', 'new_m_final_norm', 'new_v_ffn1_norm', 'new_v_ffn1_wg', 'new_v_ffn1_wu', 'new_v_ffn1_wd', 'new_v_mix_norm', 'new_v_w_in', 'new_v_gla_w_a2', 'new_v_gla_b_a', 'new_v_gla_gn_w', 'new_v_rwkv_mu', 'new_v_rwkv_w0', 'new_v_rwkv_w_w2', 'new_v_rwkv_a0', 'new_v_rwkv_w_a2', 'new_v_rwkv_w_g2', 'new_v_rwkv_k_k', 'new_v_rwkv_k_a', 'new_v_rwkv_r_k', 'new_v_rwkv_lnx_w', 'new_v_rwkv_lnx_b', 'new_v_gate_b', 'new_v_w_branch', 'new_v_w_out', 'new_v_ffn2_norm', 'new_v_ffn2_wg', 'new_v_ffn2_wu', 'new_v_ffn2_wd', 'new_v_final_norm']
TWIN_LEAF_KINDS = {'loss': 'loss', 'grad_x': 'grad_x', 'grad_ffn1_norm': 'grad_w', 'grad_ffn1_wg': 'grad_w', 'grad_ffn1_wu': 'grad_w', 'grad_ffn1_wd': 'grad_w', 'grad_mix_norm': 'grad_w', 'grad_w_in': 'grad_w', 'grad_gla_w_a2': 'grad_w', 'grad_gla_b_a': 'grad_w', 'grad_gla_gn_w': 'grad_w', 'grad_rwkv_mu': 'grad_w', 'grad_rwkv_w0': 'grad_w', 'grad_rwkv_w_w2': 'grad_w', 'grad_rwkv_a0': 'grad_w', 'grad_rwkv_w_a2': 'grad_w', 'grad_rwkv_w_g2': 'grad_w', 'grad_rwkv_k_k': 'grad_w', 'grad_rwkv_k_a': 'grad_w', 'grad_rwkv_r_k': 'grad_w', 'grad_rwkv_lnx_w': 'grad_w', 'grad_rwkv_lnx_b': 'grad_w', 'grad_gate_b': 'grad_w', 'grad_w_branch': 'grad_w', 'grad_w_out': 'grad_w', 'grad_ffn2_norm': 'grad_w', 'grad_ffn2_wg': 'grad_w', 'grad_ffn2_wu': 'grad_w', 'grad_ffn2_wd': 'grad_w', 'grad_final_norm': 'grad_w', 'delta_ffn1_norm': 'delta_w', 'delta_ffn1_wg': 'delta_w', 'delta_ffn1_wu': 'delta_w', 'delta_ffn1_wd': 'delta_w', 'delta_mix_norm': 'delta_w', 'delta_w_in': 'delta_w', 'delta_gla_w_a2': 'delta_w', 'delta_gla_b_a': 'delta_w', 'delta_gla_gn_w': 'delta_w', 'delta_rwkv_mu': 'delta_w', 'delta_rwkv_w0': 'delta_w', 'delta_rwkv_w_w2': 'delta_w', 'delta_rwkv_a0': 'delta_w', 'delta_rwkv_w_a2': 'delta_w', 'delta_rwkv_w_g2': 'delta_w', 'delta_rwkv_k_k': 'delta_w', 'delta_rwkv_k_a': 'delta_w', 'delta_rwkv_r_k': 'delta_w', 'delta_rwkv_lnx_w': 'delta_w', 'delta_rwkv_lnx_b': 'delta_w', 'delta_gate_b': 'delta_w', 'delta_w_branch': 'delta_w', 'delta_w_out': 'delta_w', 'delta_ffn2_norm': 'delta_w', 'delta_ffn2_wg': 'delta_w', 'delta_ffn2_wu': 'delta_w', 'delta_ffn2_wd': 'delta_w', 'delta_final_norm': 'delta_w', 'new_m_ffn1_norm': 'new_m', 'new_m_ffn1_wg': 'new_m', 'new_m_ffn1_wu': 'new_m', 'new_m_ffn1_wd': 'new_m', 'new_m_mix_norm': 'new_m', 'new_m_w_in': 'new_m', 'new_m_gla_w_a2': 'new_m', 'new_m_gla_b_a': 'new_m', 'new_m_gla_gn_w': 'new_m', 'new_m_rwkv_mu': 'new_m', 'new_m_rwkv_w0': 'new_m', 'new_m_rwkv_w_w2': 'new_m', 'new_m_rwkv_a0': 'new_m', 'new_m_rwkv_w_a2': 'new_m', 'new_m_rwkv_w_g2': 'new_m', 'new_m_rwkv_k_k': 'new_m', 'new_m_rwkv_k_a': 'new_m', 'new_m_rwkv_r_k': 'new_m', 'new_m_rwkv_lnx_w': 'new_m', 'new_m_rwkv_lnx_b': 'new_m', 'new_m_gate_b': 'new_m', 'new_m_w_branch': 'new_m', 'new_m_w_out': 'new_m', 'new_m_ffn2_norm': 'new_m', 'new_m_ffn2_wg': 'new_m', 'new_m_ffn2_wu': 'new_m', 'new_m_ffn2_wd': 'new_m', 'new_m_final_norm': 'new_m', 'new_v_ffn1_norm': 'new_v', 'new_v_ffn1_wg': 'new_v', 'new_v_ffn1_wu': 'new_v', 'new_v_ffn1_wd': 'new_v', 'new_v_mix_norm': 'new_v', 'new_v_w_in': 'new_v', 'new_v_gla_w_a2': 'new_v', 'new_v_gla_b_a': 'new_v', 'new_v_gla_gn_w': 'new_v', 'new_v_rwkv_mu': 'new_v', 'new_v_rwkv_w0': 'new_v', 'new_v_rwkv_w_w2': 'new_v', 'new_v_rwkv_a0': 'new_v', 'new_v_rwkv_w_a2': 'new_v', 'new_v_rwkv_w_g2': 'new_v', 'new_v_rwkv_k_k': 'new_v', 'new_v_rwkv_k_a': 'new_v', 'new_v_rwkv_r_k': 'new_v', 'new_v_rwkv_lnx_w': 'new_v', 'new_v_rwkv_lnx_b': 'new_v', 'new_v_gate_b': 'new_v', 'new_v_w_branch': 'new_v', 'new_v_w_out': 'new_v', 'new_v_ffn2_norm': 'new_v', 'new_v_ffn2_wg': 'new_v', 'new_v_ffn2_wu': 'new_v', 'new_v_ffn2_wd': 'new_v', 'new_v_final_norm': 'new_v'}


def _forward(args):
    return _fwd_reference(*[args[k] for k in FWD_PARAMS])


def _output_shape():
    def fwd():
        inp = _fwd_setup_inputs(0)
        return _fwd_reference(*[inp[k] for k in FWD_PARAMS])
    out = _jax.eval_shape(fwd)
    return out.shape, out.dtype

N_MICROBATCH = 1
ADAM_LR = 0.001
ADAM_B1 = 0.9
ADAM_B2 = 0.999
ADAM_EPS = 1e-08
ADAM_WD = 0.01
ADAM_STEP = 10
PER_EXAMPLE_BATCH_AXIS = {'x': 0, 'loss_target': 0}
SHARED_INPUTS = []
_WEIGHT_DTYPES = {'ffn1_norm': _jnp.float32, 'ffn1_wg': _jnp.float32, 'ffn1_wu': _jnp.float32, 'ffn1_wd': _jnp.float32, 'mix_norm': _jnp.float32, 'w_in': _jnp.float32, 'gla_w_a2': _jnp.float32, 'gla_b_a': _jnp.float32, 'gla_gn_w': _jnp.float32, 'rwkv_mu': _jnp.float32, 'rwkv_w0': _jnp.float32, 'rwkv_w_w2': _jnp.float32, 'rwkv_a0': _jnp.float32, 'rwkv_w_a2': _jnp.float32, 'rwkv_w_g2': _jnp.float32, 'rwkv_k_k': _jnp.float32, 'rwkv_k_a': _jnp.float32, 'rwkv_r_k': _jnp.float32, 'rwkv_lnx_w': _jnp.float32, 'rwkv_lnx_b': _jnp.float32, 'gate_b': _jnp.float32, 'w_branch': _jnp.float32, 'w_out': _jnp.float32, 'ffn2_norm': _jnp.float32, 'ffn2_wg': _jnp.float32, 'ffn2_wu': _jnp.float32, 'ffn2_wd': _jnp.float32, 'final_norm': _jnp.float32}
MOMENT_SCALE = {'ffn1_norm': 4.668469e-02, 'ffn1_wg': 1.964677e-02, 'ffn1_wu': 1.902524e-02, 'ffn1_wd': 3.154874e-02, 'mix_norm': 7.211200e-02, 'w_in': 3.133767e-02, 'gla_w_a2': 5.997190e-03, 'gla_b_a': 2.451363e-02, 'gla_gn_w': 7.512690e-02, 'rwkv_mu': 5.546973e-02, 'rwkv_w0': 1.950615e-02, 'rwkv_w_w2': 3.601707e-03, 'rwkv_a0': 1.644874e-02, 'rwkv_w_a2': 1.434213e-02, 'rwkv_w_g2': 3.482749e-02, 'rwkv_k_k': 1.244988e-02, 'rwkv_k_a': 3.631174e-02, 'rwkv_r_k': 7.512357e-02, 'rwkv_lnx_w': 3.497673e-02, 'rwkv_lnx_b': 3.260745e-02, 'gate_b': 9.892728e-03, 'w_branch': 2.527451e-02, 'w_out': 3.577747e-02, 'ffn2_norm': 3.252997e-02, 'ffn2_wg': 1.404831e-02, 'ffn2_wu': 1.361400e-02, 'ffn2_wd': 2.253560e-02, 'final_norm': 1.598047e+01}


def _to_microbatches(a, axis):
    t = _jnp.moveaxis(a, axis, 0)
    t = t.reshape((N_MICROBATCH, t.shape[0] // N_MICROBATCH) + t.shape[1:])
    return _jnp.moveaxis(t, 1, axis + 1)


def setup_inputs(seed: int = 0) -> dict:
    inp = _fwd_setup_inputs(seed)
    key = _jax.random.fold_in(_jax.random.key(seed), 7919)
    shape, _ = _output_shape()
    out = dict(inp)
    out["loss_target"] = _jax.random.normal(_jax.random.fold_in(key, 0), shape, _jnp.float32)
    for i, name in enumerate(TWIN_WEIGHTS):
        w = inp[name].astype(_jnp.float32)
        if MOMENT_SCALE is None:
            s = _jnp.sqrt(_jnp.mean(_jnp.square(w)) + 1e-30)
        else:
            s = MOMENT_SCALE[name]
        km, kv = _jax.random.split(_jax.random.fold_in(key, i + 1))
        out[name] = w
        out["m_" + name] = s * _jax.random.normal(km, w.shape, _jnp.float32)
        out["v_" + name] = (s * s) * _jax.random.uniform(kv, w.shape, _jnp.float32, 0.5, 1.5)
    if N_MICROBATCH > 1:
        for name, axis in PER_EXAMPLE_BATCH_AXIS.items():
            out[name] = _to_microbatches(out[name], axis)
    return {'x': out['x'], 'ffn1_norm': out['ffn1_norm'], 'ffn1_wg': out['ffn1_wg'], 'ffn1_wu': out['ffn1_wu'], 'ffn1_wd': out['ffn1_wd'], 'mix_norm': out['mix_norm'], 'w_in': out['w_in'], 'gla_w_a2': out['gla_w_a2'], 'gla_b_a': out['gla_b_a'], 'gla_gn_w': out['gla_gn_w'], 'rwkv_mu': out['rwkv_mu'], 'rwkv_w0': out['rwkv_w0'], 'rwkv_w_w2': out['rwkv_w_w2'], 'rwkv_a0': out['rwkv_a0'], 'rwkv_w_a2': out['rwkv_w_a2'], 'rwkv_w_g2': out['rwkv_w_g2'], 'rwkv_k_k': out['rwkv_k_k'], 'rwkv_k_a': out['rwkv_k_a'], 'rwkv_r_k': out['rwkv_r_k'], 'rwkv_lnx_w': out['rwkv_lnx_w'], 'rwkv_lnx_b': out['rwkv_lnx_b'], 'gate_b': out['gate_b'], 'w_branch': out['w_branch'], 'w_out': out['w_out'], 'ffn2_norm': out['ffn2_norm'], 'ffn2_wg': out['ffn2_wg'], 'ffn2_wu': out['ffn2_wu'], 'ffn2_wd': out['ffn2_wd'], 'final_norm': out['final_norm'], 'loss_target': out['loss_target'], 'm_ffn1_norm': out['m_ffn1_norm'], 'm_ffn1_wg': out['m_ffn1_wg'], 'm_ffn1_wu': out['m_ffn1_wu'], 'm_ffn1_wd': out['m_ffn1_wd'], 'm_mix_norm': out['m_mix_norm'], 'm_w_in': out['m_w_in'], 'm_gla_w_a2': out['m_gla_w_a2'], 'm_gla_b_a': out['m_gla_b_a'], 'm_gla_gn_w': out['m_gla_gn_w'], 'm_rwkv_mu': out['m_rwkv_mu'], 'm_rwkv_w0': out['m_rwkv_w0'], 'm_rwkv_w_w2': out['m_rwkv_w_w2'], 'm_rwkv_a0': out['m_rwkv_a0'], 'm_rwkv_w_a2': out['m_rwkv_w_a2'], 'm_rwkv_w_g2': out['m_rwkv_w_g2'], 'm_rwkv_k_k': out['m_rwkv_k_k'], 'm_rwkv_k_a': out['m_rwkv_k_a'], 'm_rwkv_r_k': out['m_rwkv_r_k'], 'm_rwkv_lnx_w': out['m_rwkv_lnx_w'], 'm_rwkv_lnx_b': out['m_rwkv_lnx_b'], 'm_gate_b': out['m_gate_b'], 'm_w_branch': out['m_w_branch'], 'm_w_out': out['m_w_out'], 'm_ffn2_norm': out['m_ffn2_norm'], 'm_ffn2_wg': out['m_ffn2_wg'], 'm_ffn2_wu': out['m_ffn2_wu'], 'm_ffn2_wd': out['m_ffn2_wd'], 'm_final_norm': out['m_final_norm'], 'v_ffn1_norm': out['v_ffn1_norm'], 'v_ffn1_wg': out['v_ffn1_wg'], 'v_ffn1_wu': out['v_ffn1_wu'], 'v_ffn1_wd': out['v_ffn1_wd'], 'v_mix_norm': out['v_mix_norm'], 'v_w_in': out['v_w_in'], 'v_gla_w_a2': out['v_gla_w_a2'], 'v_gla_b_a': out['v_gla_b_a'], 'v_gla_gn_w': out['v_gla_gn_w'], 'v_rwkv_mu': out['v_rwkv_mu'], 'v_rwkv_w0': out['v_rwkv_w0'], 'v_rwkv_w_w2': out['v_rwkv_w_w2'], 'v_rwkv_a0': out['v_rwkv_a0'], 'v_rwkv_w_a2': out['v_rwkv_w_a2'], 'v_rwkv_w_g2': out['v_rwkv_w_g2'], 'v_rwkv_k_k': out['v_rwkv_k_k'], 'v_rwkv_k_a': out['v_rwkv_k_a'], 'v_rwkv_r_k': out['v_rwkv_r_k'], 'v_rwkv_lnx_w': out['v_rwkv_lnx_w'], 'v_rwkv_lnx_b': out['v_rwkv_lnx_b'], 'v_gate_b': out['v_gate_b'], 'v_w_branch': out['v_w_branch'], 'v_w_out': out['v_w_out'], 'v_ffn2_norm': out['v_ffn2_norm'], 'v_ffn2_wg': out['v_ffn2_wg'], 'v_ffn2_wu': out['v_ffn2_wu'], 'v_ffn2_wd': out['v_ffn2_wd'], 'v_final_norm': out['v_final_norm']}


def _loss(weights, diff, rest, loss_target):
    with _jax.named_scope("forward"):
        args = {**rest, TWIN_DIFF_INPUT: diff, **{k: w.astype(_WEIGHT_DTYPES[k]) for k, w in weights.items()}}
        y = _forward(args)
    with _jax.named_scope("loss_head"):
        err = _jnp.square(y.astype(_jnp.float32) - loss_target)
        return 0.5 * _jnp.sum(_jnp.mean(err, axis=-1)) if err.ndim else 0.5 * err


def _adamw(w, g, m, v):
    m = ADAM_B1 * m + (1.0 - ADAM_B1) * g
    v = ADAM_B2 * v + (1.0 - ADAM_B2) * _jnp.square(g)
    m_hat = m / (1.0 - ADAM_B1 ** ADAM_STEP)
    v_hat = v / (1.0 - ADAM_B2 ** ADAM_STEP)
    delta = -ADAM_LR * (m_hat / (_jnp.sqrt(v_hat) + ADAM_EPS) + ADAM_WD * w)
    return delta, m, v


def reference(x, ffn1_norm, ffn1_wg, ffn1_wu, ffn1_wd, mix_norm, w_in, gla_w_a2, gla_b_a, gla_gn_w, rwkv_mu, rwkv_w0, rwkv_w_w2, rwkv_a0, rwkv_w_a2, rwkv_w_g2, rwkv_k_k, rwkv_k_a, rwkv_r_k, rwkv_lnx_w, rwkv_lnx_b, gate_b, w_branch, w_out, ffn2_norm, ffn2_wg, ffn2_wu, ffn2_wd, final_norm, loss_target, m_ffn1_norm, m_ffn1_wg, m_ffn1_wu, m_ffn1_wd, m_mix_norm, m_w_in, m_gla_w_a2, m_gla_b_a, m_gla_gn_w, m_rwkv_mu, m_rwkv_w0, m_rwkv_w_w2, m_rwkv_a0, m_rwkv_w_a2, m_rwkv_w_g2, m_rwkv_k_k, m_rwkv_k_a, m_rwkv_r_k, m_rwkv_lnx_w, m_rwkv_lnx_b, m_gate_b, m_w_branch, m_w_out, m_ffn2_norm, m_ffn2_wg, m_ffn2_wu, m_ffn2_wd, m_final_norm, v_ffn1_norm, v_ffn1_wg, v_ffn1_wu, v_ffn1_wd, v_mix_norm, v_w_in, v_gla_w_a2, v_gla_b_a, v_gla_gn_w, v_rwkv_mu, v_rwkv_w0, v_rwkv_w_w2, v_rwkv_a0, v_rwkv_w_a2, v_rwkv_w_g2, v_rwkv_k_k, v_rwkv_k_a, v_rwkv_r_k, v_rwkv_lnx_w, v_rwkv_lnx_b, v_gate_b, v_w_branch, v_w_out, v_ffn2_norm, v_ffn2_wg, v_ffn2_wu, v_ffn2_wd, v_final_norm):
    given = dict(x=x, ffn1_norm=ffn1_norm, ffn1_wg=ffn1_wg, ffn1_wu=ffn1_wu, ffn1_wd=ffn1_wd, mix_norm=mix_norm, w_in=w_in, gla_w_a2=gla_w_a2, gla_b_a=gla_b_a, gla_gn_w=gla_gn_w, rwkv_mu=rwkv_mu, rwkv_w0=rwkv_w0, rwkv_w_w2=rwkv_w_w2, rwkv_a0=rwkv_a0, rwkv_w_a2=rwkv_w_a2, rwkv_w_g2=rwkv_w_g2, rwkv_k_k=rwkv_k_k, rwkv_k_a=rwkv_k_a, rwkv_r_k=rwkv_r_k, rwkv_lnx_w=rwkv_lnx_w, rwkv_lnx_b=rwkv_lnx_b, gate_b=gate_b, w_branch=w_branch, w_out=w_out, ffn2_norm=ffn2_norm, ffn2_wg=ffn2_wg, ffn2_wu=ffn2_wu, ffn2_wd=ffn2_wd, final_norm=final_norm, loss_target=loss_target, m_ffn1_norm=m_ffn1_norm, m_ffn1_wg=m_ffn1_wg, m_ffn1_wu=m_ffn1_wu, m_ffn1_wd=m_ffn1_wd, m_mix_norm=m_mix_norm, m_w_in=m_w_in, m_gla_w_a2=m_gla_w_a2, m_gla_b_a=m_gla_b_a, m_gla_gn_w=m_gla_gn_w, m_rwkv_mu=m_rwkv_mu, m_rwkv_w0=m_rwkv_w0, m_rwkv_w_w2=m_rwkv_w_w2, m_rwkv_a0=m_rwkv_a0, m_rwkv_w_a2=m_rwkv_w_a2, m_rwkv_w_g2=m_rwkv_w_g2, m_rwkv_k_k=m_rwkv_k_k, m_rwkv_k_a=m_rwkv_k_a, m_rwkv_r_k=m_rwkv_r_k, m_rwkv_lnx_w=m_rwkv_lnx_w, m_rwkv_lnx_b=m_rwkv_lnx_b, m_gate_b=m_gate_b, m_w_branch=m_w_branch, m_w_out=m_w_out, m_ffn2_norm=m_ffn2_norm, m_ffn2_wg=m_ffn2_wg, m_ffn2_wu=m_ffn2_wu, m_ffn2_wd=m_ffn2_wd, m_final_norm=m_final_norm, v_ffn1_norm=v_ffn1_norm, v_ffn1_wg=v_ffn1_wg, v_ffn1_wu=v_ffn1_wu, v_ffn1_wd=v_ffn1_wd, v_mix_norm=v_mix_norm, v_w_in=v_w_in, v_gla_w_a2=v_gla_w_a2, v_gla_b_a=v_gla_b_a, v_gla_gn_w=v_gla_gn_w, v_rwkv_mu=v_rwkv_mu, v_rwkv_w0=v_rwkv_w0, v_rwkv_w_w2=v_rwkv_w_w2, v_rwkv_a0=v_rwkv_a0, v_rwkv_w_a2=v_rwkv_w_a2, v_rwkv_w_g2=v_rwkv_w_g2, v_rwkv_k_k=v_rwkv_k_k, v_rwkv_k_a=v_rwkv_k_a, v_rwkv_r_k=v_rwkv_r_k, v_rwkv_lnx_w=v_rwkv_lnx_w, v_rwkv_lnx_b=v_rwkv_lnx_b, v_gate_b=v_gate_b, v_w_branch=v_w_branch, v_w_out=v_w_out, v_ffn2_norm=v_ffn2_norm, v_ffn2_wg=v_ffn2_wg, v_ffn2_wu=v_ffn2_wu, v_ffn2_wd=v_ffn2_wd, v_final_norm=v_final_norm)
    weights = {n: given[n] for n in TWIN_WEIGHTS}
    shared = {n: given[n] for n in SHARED_INPUTS}
    per_example = {n: given[n] for n in ['x']}
    grad_fn = _jax.value_and_grad(_loss, argnums=(0, 1))

    def one_microbatch(ex, loss_target):
        ex = dict(ex)
        diff = ex.pop(TWIN_DIFF_INPUT)
        return grad_fn(weights, diff, {**shared, **ex}, loss_target)

    if N_MICROBATCH == 1:
        loss, (grad_w, grad_x) = one_microbatch(per_example, given["loss_target"])
    else:
        def body(carry, xs):
            loss_sum, grad_sum = carry
            l_k, (gw_k, gx_k) = one_microbatch(xs[0], xs[1])
            with _jax.named_scope("update"):
                return (loss_sum + l_k, _jax.tree.map(_jnp.add, grad_sum, gw_k)), gx_k

        init = (_jnp.zeros((), _jnp.float32), _jax.tree.map(_jnp.zeros_like, weights))
        (loss, grad_w), grad_x = _jax.lax.scan(body, init, (per_example, given["loss_target"]))
    with _jax.named_scope("update"):
        delta_w, new_m, new_v = {}, {}, {}
        for n in TWIN_WEIGHTS:
            delta_w[n], new_m[n], new_v[n] = _adamw(weights[n], grad_w[n], given["m_" + n], given["v_" + n])
    return (loss, grad_x, *[grad_w[n] for n in TWIN_WEIGHTS], *[delta_w[n] for n in TWIN_WEIGHTS],
            *[new_m[n] for n in TWIN_WEIGHTS], *[new_v[n] for n in TWIN_WEIGHTS])
```

```python
import functools
import math

import jax
import jax.numpy as jnp
from jax import lax
from jax.experimental import pallas as pl
from jax.experimental.pallas import tpu as pltpu

F32 = jnp.float32
BF16 = jnp.bfloat16
MESH_IDS = pl.DeviceIdType.MESH

NORM_EPS = 1e-6
GN_EPS = 64e-5
GLA_TAU = 16.0
CHUNK = 64
GLA_DK = 128
GLA_DV = 256
RW_HD = 64
LORA_PAD = 128
GATE_LORA = 256
ADAM_LR, ADAM_B1, ADAM_B2, ADAM_EPS, ADAM_WD, ADAM_STEP = 0.001, 0.9, 0.999, 1e-08, 0.01, 10

VMEM_LIMIT_BYTES = 56 * 1024 * 1024
HBM_SPEC = pl.BlockSpec(memory_space=pltpu.HBM)


def _cparams(sem=None):
    return pltpu.CompilerParams(dimension_semantics=sem, vmem_limit_bytes=VMEM_LIMIT_BYTES)


def _pick(n, target, mult=128):
    best = None
    for t in range(mult, min(n, target) + 1, mult):
        if n % t == 0:
            best = t
    return best if best is not None else n


_NN = (((1,), (0,)), ((), ()))
_NT = (((1,), (1,)), ((), ()))
_TN = (((0,), (0,)), ((), ()))


def _dg(a, b, dims):
    return lax.dot_general(a.astype(BF16), b.astype(BF16), dims, preferred_element_type=F32)


@jax.custom_vjp
def mm_nn(a, b):
    return _dg(a, b, _NN)


def _mm_nn_fwd(a, b):
    return _dg(a, b, _NN), (a, b)


def _mm_nn_bwd(res, g):
    a, b = res
    return _dg(g, b, _NT).astype(a.dtype), _dg(a, g, _TN).astype(b.dtype)


mm_nn.defvjp(_mm_nn_fwd, _mm_nn_bwd)


@jax.custom_vjp
def mm_nt(a, b):
    return _dg(a, b, _NT)


def _mm_nt_fwd(a, b):
    return _dg(a, b, _NT), (a, b)


def _mm_nt_bwd(res, g):
    a, b = res
    return _dg(g, b, _NN).astype(a.dtype), _dg(g, a, _TN).astype(b.dtype)


mm_nt.defvjp(_mm_nt_fwd, _mm_nt_bwd)


@jax.custom_vjp
def mm_tn(a, b):
    return _dg(a, b, _TN)


def _mm_tn_fwd(a, b):
    return _dg(a, b, _TN), (a, b)


def _mm_tn_bwd(res, g):
    a, b = res
    return _dg(b, g, _NT).astype(a.dtype), _dg(a, g, _NN).astype(b.dtype)


mm_tn.defvjp(_mm_tn_fwd, _mm_tn_bwd)


def _split3(x):
    h = x.astype(BF16)
    r = x - h.astype(F32)
    m = r.astype(BF16)
    l = (r - m.astype(F32)).astype(BF16)
    return h, m, l


def _block_ones(n, seg):
    i = lax.broadcasted_iota(jnp.int32, (n, n), 0) // seg
    j = lax.broadcasted_iota(jnp.int32, (n, n), 1) // seg
    return (i == j).astype(BF16)


def _segsum_raw(x, seg, terms):
    ones = _block_ones(128, seg)
    outs = []
    for j in range(x.shape[1] // 128):
        t = x[:, j * 128:(j + 1) * 128]
        parts = _split3(t)[:terms]
        acc = jnp.dot(parts[0], ones, preferred_element_type=F32)
        for p_ in parts[1:]:
            acc = acc + jnp.dot(p_, ones, preferred_element_type=F32)
        outs.append(acc)
    return outs[0] if len(outs) == 1 else jnp.concatenate(outs, axis=1)


@jax.custom_vjp
def segsum64(x):
    return _segsum_raw(x, RW_HD, 3)


segsum64.defvjp(lambda x: (_segsum_raw(x, RW_HD, 3), None), lambda _, g: (_segsum_raw(g, RW_HD, 3),))


def _tri(n, upper):
    i = lax.broadcasted_iota(jnp.int32, (n, n), 0)
    j = lax.broadcasted_iota(jnp.int32, (n, n), 1)
    return ((i <= j) if upper else (i >= j)).astype(BF16)


def _tri_mm(x, upper):
    t = _tri(x.shape[0], upper)
    h, m, l = _split3(x)
    return (jnp.dot(t, h, preferred_element_type=F32) + jnp.dot(t, m, preferred_element_type=F32)
            + jnp.dot(t, l, preferred_element_type=F32))


@jax.custom_vjp
def cumsum_rows(x):
    return _tri_mm(x, False)


cumsum_rows.defvjp(lambda x: (_tri_mm(x, False), None), lambda _, g: (_tri_mm(g, True),))


def _make_split(sizes, axis):
    offs = [sum(sizes[:i]) for i in range(len(sizes))]

    def cut(x):
        if axis == 1:
            return tuple(x[:, o:o + s] for o, s in zip(offs, sizes))
        return tuple(x[o:o + s, :] for o, s in zip(offs, sizes))

    @jax.custom_vjp
    def split(x):
        return cut(x)

    split.defvjp(lambda x: (cut(x), None), lambda _, gs: (jnp.concatenate(gs, axis=axis),))
    return split


@jax.custom_vjp
def log_sigmoid(z):
    return jnp.minimum(z, 0.0) - jnp.log(1.0 + jnp.exp(-jnp.abs(z)))


log_sigmoid.defvjp(lambda z: (log_sigmoid(z), z), lambda z, g: (g * (1.0 - jax.nn.sigmoid(z)),))


def silu(x):
    return x * jax.nn.sigmoid(x)


def matmul(a, b, mode, out_dtype, name, tm=1024, tn=512, tk=2048, job=None):
    if mode == "nn":
        (M, K), (K2, N) = a.shape, b.shape
    elif mode == "nt":
        (M, K), (N, K2) = a.shape, b.shape
    else:
        (K, M), (K2, N) = a.shape, b.shape
    assert K == K2, (name, a.shape, b.shape)
    tm, tn, tk = _pick(M, tm), _pick(N, tn), _pick(K, tk)
    grid = (M // tm, N // tn, K // tk)
    dims = {"nn": _NN, "nt": _NT, "tn": _TN}[mode]
    a_spec = pl.BlockSpec((tk, tm), lambda i, j, k: (k, i)) if mode == "tn" else pl.BlockSpec((tm, tk), lambda i, j, k: (i, k))
    b_spec = pl.BlockSpec((tn, tk), lambda i, j, k: (j, k)) if mode == "nt" else pl.BlockSpec((tk, tn), lambda i, j, k: (k, j))
    n_in = 0 if job is None else len(job.operands)
    n_out = 0 if job is None else len(job.out_shapes)

    def body(a_ref, b_ref, *rest):
        job_ins, o_ref, job_outs = rest[:n_in], rest[n_in], rest[n_in + 1:n_in + 1 + n_out]
        acc_ref, sems = rest[n_in + 1 + n_out], rest[n_in + 2 + n_out:]
        i, j, k = pl.program_id(0), pl.program_id(1), pl.program_id(2)
        if job is not None:
            @pl.when((i == 0) & (j == 0) & (k == 0))
            def _():
                job.start(job_ins, job_outs, sems)

        part = _dg(a_ref[...], b_ref[...], dims)

        @pl.when(k == 0)
        def _():
            acc_ref[...] = part

        @pl.when(k > 0)
        def _():
            acc_ref[...] += part

        @pl.when(k == grid[2] - 1)
        def _():
            o_ref[...] = acc_ref[...].astype(o_ref.dtype)

        if job is not None:
            @pl.when((i == grid[0] - 1) & (j == grid[1] - 1) & (k == grid[2] - 1))
            def _():
                job.finish(job_ins, job_outs, sems)

    main_out = jax.ShapeDtypeStruct((M, N), out_dtype)
    main_spec = pl.BlockSpec((tm, tn), lambda i, j, k: (i, j))
    if job is None:
        return pl.pallas_call(
            body, name=name, grid=grid, in_specs=[a_spec, b_spec], out_specs=main_spec, out_shape=main_out,
            scratch_shapes=[pltpu.VMEM((tm, tn), F32)],
            compiler_params=_cparams(("parallel", "parallel", "arbitrary")),
        )(a, b)
    res = pl.pallas_call(
        body, name=name, grid=grid, in_specs=[a_spec, b_spec] + [HBM_SPEC] * n_in,
        out_specs=[main_spec] + [HBM_SPEC] * n_out, out_shape=[main_out] + list(job.out_shapes),
        scratch_shapes=[pltpu.VMEM((tm, tn), F32)] + list(job.scratch),
        compiler_params=_cparams(("arbitrary", "arbitrary", "arbitrary")),
    )(a, b, *job.operands)
    return res[0], res[1:]


def _cols(arr, width=None, off=0):
    width = arr.shape[1] if width is None else width
    assert off % width == 0, (off, width)
    return (arr, width, off // width)


def rowwise(name, fn, rows, consts, row_outs, acc_outs, tr, extra_specs=()):
    S = rows[0][0].shape[0]
    tr = min(tr, S)
    assert S % tr == 0
    n_in = len(rows) + len(extra_specs) + len(consts)
    n_ro = len(row_outs)
    in_specs = [pl.BlockSpec((tr, w), functools.partial(lambda i, cb: (i, cb), cb=cb)) for (_, w, cb) in rows]
    in_specs += [spec for (_, spec) in extra_specs]
    in_specs += [pl.BlockSpec(c.shape, lambda i: (0, 0)) for c in consts]
    out_shape = [jax.ShapeDtypeStruct((S, w), dt) for (w, dt) in row_outs]
    out_shape += [jax.ShapeDtypeStruct(shp, F32) for shp in acc_outs]
    out_specs = [pl.BlockSpec((tr, w), lambda i: (i, 0)) for (w, _) in row_outs]
    out_specs += [pl.BlockSpec(shp, lambda i: (0, 0)) for shp in acc_outs]

    def body(*refs):
        ins = [r[...] for r in refs[:n_in]]
        outs = fn(*ins)
        outs = outs if isinstance(outs, (tuple, list)) else (outs,)
        assert len(outs) == n_ro + len(acc_outs), (name, len(outs))
        for o_ref, val in zip(refs[n_in:n_in + n_ro], outs[:n_ro]):
            o_ref[...] = val.astype(o_ref.dtype)
        i = pl.program_id(0)
        for a_ref, val in zip(refs[n_in + n_ro:], outs[n_ro:]):
            @pl.when(i == 0)
            def _(a_ref=a_ref, val=val):
                a_ref[...] = val.astype(F32)

            @pl.when(i > 0)
            def _(a_ref=a_ref, val=val):
                a_ref[...] += val.astype(F32)

    res = pl.pallas_call(
        body, name=name, grid=(S // tr,), in_specs=in_specs, out_specs=out_specs, out_shape=out_shape,
        compiler_params=_cparams(("arbitrary",) if acc_outs else ("parallel",)),
    )(*[r[0] for r in rows], *[e[0] for e in extra_specs], *consts)
    return res


def f32(*xs):
    return [x.astype(F32) for x in xs]


def f_rms(x, g):
    return x * lax.rsqrt(jnp.mean(x * x, axis=-1, keepdims=True) + NORM_EPS) * g


def f_swiglu(gx, ux):
    return silu(gx) * ux


def f_merge(gp, yg, yr, gate_b, d):
    gates = jax.nn.sigmoid(gp + gate_b)
    g1, g2 = _make_split((d, d), 1)(gates)
    return g1 * yg + g2 * yr


def rw_window(rw):
    o = dict(rr=0, rk=rw, rv=2 * rw, gd=3 * rw, wd=3 * rw + GATE_LORA)
    o["ad"] = o["wd"] + LORA_PAD
    o["a"] = o["ad"] + LORA_PAD
    o["used"] = o["a"] + LORA_PAD
    return o


def f_rw_pre(xs, w0, w_w2, a0, w_a2, w_g2, k_k, k_a, rw):
    win = xs.shape[1]
    o = rw_window(rw)
    sizes = (rw, rw, rw, GATE_LORA, LORA_PAD, LORA_PAD, win - o["a"])
    rr, rk, rv, gd, wd, ad, _ = _make_split(sizes, 1)(xs)
    w_raw = w0 + mm_nn(jnp.tanh(wd), w_w2)
    dec = jnp.exp(-jnp.exp(log_sigmoid(w_raw) - 0.5))
    a = jax.nn.sigmoid(a0 + mm_nn(ad, w_a2))
    g = mm_nn(jax.nn.sigmoid(gd), w_g2)
    kx = rk * k_k
    kk = kx / jnp.maximum(jnp.sqrt(segsum64(kx * kx)), 1e-12)
    k2 = rk * (1.0 + (a - 1.0) * k_a)
    return rr, dec, k2, rv, kk, kk * a, g


def f_rw_post(y, r, k2, v, g, lnx_w, lnx_b, r_k):
    mu = segsum64(y) * (1.0 / RW_HD)
    yc = y - mu
    var = segsum64(yc * yc) * (1.0 / RW_HD)
    yn = yc * lax.rsqrt(var + GN_EPS) * lnx_w + lnx_b
    bonus = segsum64(r * k2 * r_k) * v
    return (yn + bonus) * g


def f_gla_chunk(q, k, v, r, a, st_prev, w_a2, b_a, gn_w, heads):
    z = mm_nn(a, w_a2) + b_a
    la = log_sigmoid(z) * (1.0 / GLA_TAU)
    cum = cumsum_rows(la)
    total = jnp.sum(la, axis=0, keepdims=True)
    kdec = k * jnp.exp(total - cum)
    et = jnp.exp(total)
    qs = q * (GLA_DK ** -0.5)
    sk = _make_split((GLA_DK,) * heads, 1)
    sv = _make_split((GLA_DV,) * heads, 1)
    ss = _make_split((GLA_DV,) * heads, 0)
    kd_h, q_h, et_h, v_h, st_h = sk(kdec), sk(qs), sk(et), sv(v), ss(st_prev)
    outs, news = [], []
    for h in range(heads):
        st_new = st_h[h] * et_h[h] + mm_tn(v_h[h], kd_h[h])
        o = mm_nt(q_h[h], st_new)
        o = o * lax.rsqrt(jnp.mean(o * o, axis=-1, keepdims=True) + NORM_EPS) * gn_w
        outs.append(o)
        news.append(st_new)
    o_all = outs[0] if heads == 1 else jnp.concatenate(outs, axis=1)
    st_all = news[0] if heads == 1 else jnp.concatenate(news, axis=0)
    return o_all * silu(r), st_all


def gla_forward(p, lay, w_a2, b_a, gn_w, heads):
    S = p.shape[0]
    nc = S // CHUNK
    gq, gv = heads * GLA_DK, heads * GLA_DV

    def spec(width, off, rev=False):
        assert off % width == 0
        return pl.BlockSpec((CHUNK, width), functools.partial(lambda n, cb: (n, cb), cb=off // width))

    def body(q_ref, k_ref, v_ref, r_ref, a_ref, w_ref, b_ref, g_ref, o_ref, st_out_ref, st_sc):
        @pl.when(pl.program_id(0) == 0)
        def _():
            st_sc[...] = jnp.zeros_like(st_sc)

        st_prev = st_sc[...]
        st_out_ref[0] = st_prev
        o, st_new = f_gla_chunk(*f32(q_ref[...], k_ref[...], v_ref[...], r_ref[...], a_ref[...]), st_prev,
                                w_ref[...], b_ref[...], g_ref[...], heads)
        o_ref[...] = o.astype(o_ref.dtype)
        st_sc[...] = st_new

    return pl.pallas_call(
        body, name="gla_fwd", grid=(nc,),
        in_specs=[spec(gq, lay["q"]), spec(gq, lay["k"]), spec(gv, lay["v"]), spec(gv, lay["r"]), spec(LORA_PAD, lay["a"]),
                  pl.BlockSpec(w_a2.shape, lambda n: (0, 0)), pl.BlockSpec(b_a.shape, lambda n: (0, 0)),
                  pl.BlockSpec(gn_w.shape, lambda n: (0, 0))],
        out_specs=[pl.BlockSpec((CHUNK, gv), lambda n: (n, 0)), pl.BlockSpec((1, gv, GLA_DK), lambda n: (n, 0, 0))],
        out_shape=[jax.ShapeDtypeStruct((S, gv), BF16), jax.ShapeDtypeStruct((nc, gv, GLA_DK), F32)],
        scratch_shapes=[pltpu.VMEM((gv, GLA_DK), F32)],
        compiler_params=_cparams(("arbitrary",)),
    )(p, p, p, p, p, w_a2, b_a, gn_w)


def gla_backward(p, lay, states, d_out, w_a2, b_a, gn_w, heads):
    S = p.shape[0]
    nc = S // CHUNK
    gq, gv = heads * GLA_DK, heads * GLA_DV

    def spec(width, off):
        assert off % width == 0
        return pl.BlockSpec((CHUNK, width), functools.partial(lambda n, cb: (nc - 1 - n, cb), cb=off // width))

    def rev(width):
        return pl.BlockSpec((CHUNK, width), lambda n: (nc - 1 - n, 0))

    def whole(arr):
        return pl.BlockSpec(arr.shape, lambda n: (0, 0))

    def body(q_ref, k_ref, v_ref, r_ref, a_ref, st_ref, do_ref, w_ref, b_ref, g_ref,
             dq_ref, dk_ref, dv_ref, dr_ref, da_ref, dw_ref, db_ref, dg_ref, dst_sc):
        n = pl.program_id(0)

        @pl.when(n == 0)
        def _():
            dst_sc[...] = jnp.zeros_like(dst_sc)

        fn = functools.partial(f_gla_chunk, heads=heads)
        prim = (*f32(q_ref[...], k_ref[...], v_ref[...], r_ref[...], a_ref[...]), st_ref[0],
                w_ref[...].astype(F32), b_ref[...], g_ref[...])
        _, vjp = jax.vjp(fn, *prim)
        dq, dk, dv, dr, da, dst, dw, db, dg = vjp((do_ref[...].astype(F32), dst_sc[...]))
        for ref, val in ((dq_ref, dq), (dk_ref, dk), (dv_ref, dv), (dr_ref, dr), (da_ref, da)):
            ref[...] = val.astype(ref.dtype)
        dst_sc[...] = dst

        @pl.when(n == 0)
        def _():
            dw_ref[...] = dw
            db_ref[...] = db
            dg_ref[...] = dg

        @pl.when(n > 0)
        def _():
            dw_ref[...] += dw
            db_ref[...] += db
            dg_ref[...] += dg

    return pl.pallas_call(
        body, name="gla_bwd", grid=(nc,),
        in_specs=[spec(gq, lay["q"]), spec(gq, lay["k"]), spec(gv, lay["v"]), spec(gv, lay["r"]), spec(LORA_PAD, lay["a"]),
                  pl.BlockSpec((1, gv, GLA_DK), lambda n: (nc - 1 - n, 0, 0)), rev(gv),
                  whole(w_a2), whole(b_a), whole(gn_w)],
        out_specs=[rev(gq), rev(gq), rev(gv), rev(gv), rev(LORA_PAD), whole(w_a2), whole(b_a), whole(gn_w)],
        out_shape=[jax.ShapeDtypeStruct((S, gq), BF16), jax.ShapeDtypeStruct((S, gq), BF16),
                   jax.ShapeDtypeStruct((S, gv), BF16), jax.ShapeDtypeStruct((S, gv), BF16),
                   jax.ShapeDtypeStruct((S, LORA_PAD), F32),
                   jax.ShapeDtypeStruct(w_a2.shape, F32), jax.ShapeDtypeStruct(b_a.shape, F32),
                   jax.ShapeDtypeStruct(gn_w.shape, F32)],
        scratch_shapes=[pltpu.VMEM((gv, GLA_DK), F32)],
        compiler_params=_cparams(("arbitrary",)),
    )(p, p, p, p, p, states, d_out, w_a2, b_a, gn_w)


SCAN_BLOCK = 32


def _scan_helpers(nt):
    ones = _block_ones(128, RW_HD)
    rows = lax.broadcasted_iota(jnp.int32, (nt * RW_HD, 128), 0) % RW_HD
    lanes = lax.broadcasted_iota(jnp.int32, (nt * RW_HD, 128), 1) % RW_HD
    eye = rows == lanes

    def bc(ref, t):
        parts = [jnp.broadcast_to(ref[t, j:j + 1, :], (RW_HD, 128)) for j in range(nt)]
        return parts[0] if nt == 1 else jnp.concatenate(parts, axis=0)

    def seg1(x):
        return jnp.dot(x.astype(BF16), ones, preferred_element_type=F32)

    def seg2(x):
        h = x.astype(BF16)
        l = (x - h.astype(F32)).astype(BF16)
        return jnp.dot(jnp.concatenate([h, l], axis=1), jnp.concatenate([ones, ones], axis=0), preferred_element_type=F32)

    def column(ref, t):
        return seg1(jnp.where(eye, bc(ref, t), 0.0))

    def put_diag(ref, t, x):
        put_colsum(ref, t, jnp.where(eye, x, 0.0))

    def put_colsum(ref, t, x, sign=1.0):
        for j in range(nt):
            ref[t, j:j + 1, :] = sign * jnp.sum(x[j * RW_HD:(j + 1) * RW_HD, :], axis=0, keepdims=True)

    return bc, seg1, seg2, column, put_diag, put_colsum


def rwkv_scan_forward(r, w, k2, v, kk, b):
    S, nt, _ = r.shape
    tb = min(SCAN_BLOCK, S)

    def body(r_ref, w_ref, k2_ref, v_ref, kk_ref, b_ref, y_ref, st_ref, s_sc):
        @pl.when(pl.program_id(0) == 0)
        def _():
            s_sc[...] = jnp.zeros_like(s_sc)

        bc, seg1, seg2, column, put_diag, put_colsum = _scan_helpers(nt)

        def step(t, carry):
            s = s_sc[...]
            st_ref[t] = s
            sa_e = seg2(s * bc(kk_ref, t))
            s = s * bc(w_ref, t) - sa_e * bc(b_ref, t) + column(v_ref, t) * bc(k2_ref, t)
            s_sc[...] = s
            put_diag(y_ref, t, seg1(s * bc(r_ref, t)))
            return carry

        lax.fori_loop(0, tb, step, 0, unroll=8)

    row = pl.BlockSpec((tb, nt, 128), lambda i: (i, 0, 0))
    return pl.pallas_call(
        body, name="rwkv_scan_fwd", grid=(S // tb,),
        in_specs=[row] * 6,
        out_specs=[row, pl.BlockSpec((tb, nt * RW_HD, 128), lambda i: (i, 0, 0))],
        out_shape=[jax.ShapeDtypeStruct((S, nt, 128), F32), jax.ShapeDtypeStruct((S, nt * RW_HD, 128), F32)],
        scratch_shapes=[pltpu.VMEM((nt * RW_HD, 128), F32)],
        compiler_params=_cparams(("arbitrary",)),
    )(r, w, k2, v, kk, b)


def rwkv_scan_backward(r, w, k2, v, kk, b, states, dy):
    S, nt, _ = r.shape
    tb = min(SCAN_BLOCK, S)
    nb = S // tb

    def body(r_ref, w_ref, k2_ref, v_ref, kk_ref, b_ref, st_ref, dy_ref,
             dr_ref, dw_ref, dk2_ref, dv_ref, dkk_ref, db_ref, ds_sc):
        @pl.when(pl.program_id(0) == 0)
        def _():
            ds_sc[...] = jnp.zeros_like(ds_sc)

        bc, seg1, seg2, column, put_diag, put_colsum = _scan_helpers(nt)

        def step(i, carry):
            t = tb - 1 - i
            s_prev = st_ref[t]
            r_e, w_e, k2_e, kk_e, b_e = (bc(ref, t) for ref in (r_ref, w_ref, k2_ref, kk_ref, b_ref))
            v_e = column(v_ref, t)
            sa_e = seg2(s_prev * kk_e)
            s_new = s_prev * w_e - sa_e * b_e + v_e * k2_e
            dy_e = column(dy_ref, t)
            put_colsum(dr_ref, t, s_new * dy_e)
            ds = ds_sc[...] + dy_e * r_e
            put_colsum(dw_ref, t, ds * s_prev)
            nsa_e = seg2(ds * b_e)
            put_colsum(db_ref, t, ds * sa_e, -1.0)
            put_diag(dv_ref, t, seg1(ds * k2_e))
            put_colsum(dk2_ref, t, ds * v_e)
            put_colsum(dkk_ref, t, s_prev * nsa_e, -1.0)
            ds_sc[...] = ds * w_e - nsa_e * kk_e
            return carry

        lax.fori_loop(0, tb, step, 0, unroll=4)

    row = pl.BlockSpec((tb, nt, 128), lambda i: (nb - 1 - i, 0, 0))
    return pl.pallas_call(
        body, name="rwkv_scan_bwd", grid=(nb,),
        in_specs=[row] * 6 + [pl.BlockSpec((tb, nt * RW_HD, 128), lambda i: (nb - 1 - i, 0, 0)), row],
        out_specs=[row] * 6,
        out_shape=[jax.ShapeDtypeStruct((S, nt, 128), F32)] * 6,
        scratch_shapes=[pltpu.VMEM((nt * RW_HD, 128), F32)],
        compiler_params=_cparams(("arbitrary",)),
    )(r, w, k2, v, kk, b, states, dy)


def _edge_spec(width, col_block, tr, n_rows, after):
    last = n_rows // 8 - 1
    if after:
        return pl.BlockSpec((8, width), lambda i: (jnp.minimum((i + 1) * (tr // 8), last), col_block))
    return pl.BlockSpec((8, width), lambda i: (jnp.maximum(i * (tr // 8) - 1, 0), col_block))


def _shifted_prev(p, prev8):
    first = jnp.where(pl.program_id(0) == 0, 0.0, prev8[7:8, :])
    rows = lax.broadcasted_iota(jnp.int32, p.shape, 0)
    return jnp.where(rows == 0, first, pltpu.roll(p, 1, axis=0))


def token_shift_forward(p, mu_win, win, tr):
    def fn(pw, prev8, mu):
        return pw + mu * (_shifted_prev(pw, prev8) - pw)

    return rowwise("token_shift_fwd", fn, [_cols(p, win, 0)], [mu_win], [(win, F32)], [], tr,
                   extra_specs=[(p, _edge_spec(win, 0, tr, p.shape[0], False))])[0]


def token_shift_backward(p, dxs, da_gla, mu_win, win, a_off, tr):
    S = p.shape[0]
    n = S // min(tr, S)

    def fn(pw, dx, da, prev8, next8, mu):
        trr = pw.shape[0]
        last = jnp.where(pl.program_id(0) == n - 1, 0.0, next8[0:1, :])
        rows = lax.broadcasted_iota(jnp.int32, dx.shape, 0)
        dnext = jnp.where(rows == trr - 1, last, pltpu.roll(dx, trr - 1, axis=0))
        dp = (1.0 - mu) * dx + mu * dnext
        dp = jnp.concatenate([dp[:, :a_off], dp[:, a_off:a_off + LORA_PAD] + da, dp[:, a_off + LORA_PAD:]], axis=1)
        dmu = jnp.sum(dx * (_shifted_prev(pw, prev8) - pw), axis=0, keepdims=True)
        return dp, dmu

    return rowwise("token_shift_bwd", fn, [_cols(p, win, 0), _cols(dxs), _cols(da_gla)], [mu_win],
                   [(win, BF16)], [(1, win)], tr,
                   extra_specs=[(p, _edge_spec(win, 0, tr, S, False)), (dxs, _edge_spec(win, 0, tr, S, True))])


def _adamw_math(w, g, m, v):
    m = ADAM_B1 * m + (1.0 - ADAM_B1) * g
    v = ADAM_B2 * v + (1.0 - ADAM_B2) * (g * g)
    m_hat = m / (1.0 - ADAM_B1 ** ADAM_STEP)
    v_hat = v / (1.0 - ADAM_B2 ** ADAM_STEP)
    delta = -ADAM_LR * (m_hat / (jnp.sqrt(v_hat) + ADAM_EPS) + ADAM_WD * w)
    return delta, m, v


def adamw_small(ws, gs, ms, vs):
    n = len(ws)

    def body(*refs):
        for i in range(n):
            d, m, v = _adamw_math(refs[i][...], refs[n + i][...], refs[2 * n + i][...], refs[3 * n + i][...])
            refs[4 * n + i][...] = d
            refs[5 * n + i][...] = m
            refs[6 * n + i][...] = v

    shapes = [jax.ShapeDtypeStruct(w.shape, F32) for w in ws]
    outs = pl.pallas_call(body, name="adamw_small", out_shape=shapes * 3, compiler_params=_cparams())(*ws, *gs, *ms, *vs)
    return outs[:n], outs[n:2 * n], outs[2 * n:]


def _place():
    x, y, c = lax.axis_index("x"), lax.axis_index("y"), lax.axis_index("c")
    chips = [(1 - x, y), (x, 1 - y), (1 - x, 1 - y)]
    return x, y, c, chips


def _full_shape(kind, r, c):
    return {"col": (r, 4 * c), "row": (4 * r, c), "slab": (4, r, c)}[kind]


def _slab(ref, kind, k, r, c, half=None):
    n, off = (r, 0) if half is None else (r // 2, half * (r // 2))
    if kind == "col":
        return ref.at[pl.ds(off, n), pl.ds(k * c, c)]
    if kind == "row":
        return ref.at[pl.ds(k * r + off, n), :]
    return ref.at[k, pl.ds(off, n), :]


def _remote(src, dst, sems, idx, to):
    return pltpu.make_async_remote_copy(src_ref=src, dst_ref=dst, send_sem=sems[0].at[idx], recv_sem=sems[1].at[idx],
                                        device_id=to, device_id_type=MESH_IDS)


class CommJob:
    def __init__(self, operands, out_shapes, scratch, start, finish):
        self.operands, self.out_shapes, self.scratch, self.start, self.finish = operands, out_shapes, scratch, start, finish


def gather_job(big, small=()):
    big, small = list(big), list(small)
    nb, ns = len(big), len(small)
    meta = [(kind, *a.shape) for a, kind in big + small]

    def sends(srcs, outs, sems):
        own_s, own_r, ici_s, ici_r, _, _, sm_s, sm_r = sems
        x, y, c, chips = _place()
        me, sib = 2 * x + y, (x, y, 1 - c)
        cps = []
        for a in range(nb):
            kind, r, cc = meta[a]
            for j, chip in enumerate(chips):
                cps.append(_remote(srcs[a].at[pl.ds(c * (r // 2), r // 2)], _slab(outs[a], kind, me, r, cc, c),
                                   (ici_s, ici_r), (a, j), (*chip, c)))
        for a in range(nb):
            kind, r, cc = meta[a]
            cps.append(_remote(srcs[a], _slab(outs[a], kind, me, r, cc), (own_s, own_r), (a,), sib))
        for s in range(ns):
            kind, r, cc = meta[nb + s]
            for t, to in enumerate([sib] + [(*chip, c) for chip in chips]):
                cps.append(_remote(srcs[nb + s], _slab(outs[nb + s], kind, me, r, cc), (sm_s, sm_r), (s, t), to))
        return cps

    def start(srcs, outs, sems):
        for cp in sends(srcs, outs, sems):
            cp.start()

    def finish(srcs, outs, sems):
        own_s, own_r, ici_s, ici_r, fwd_s, fwd_r, sm_s, sm_r = sems
        x, y, c, chips = _place()
        me, sib = 2 * x + y, (x, y, 1 - c)
        cids = [2 * chip[0] + chip[1] for chip in chips]
        hands = []
        for a in range(nb):
            kind, r, cc = meta[a]
            for j in range(3):
                blk = _slab(outs[a], kind, cids[j], r, cc, c)
                _remote(blk, blk, (ici_s, ici_r), (a, j), sib).wait_recv()
                hands.append(_remote(blk, blk, (fwd_s, fwd_r), (a, j), sib))
                hands[-1].start()
        for a in range(nb):
            kind, r, cc = meta[a]
            for j in range(3):
                blk = _slab(outs[a], kind, cids[j], r, cc, 1 - c)
                _remote(blk, blk, (fwd_s, fwd_r), (a, j), sib).wait_recv()
            blk = _slab(outs[a], kind, me, r, cc)
            _remote(blk, blk, (own_s, own_r), (a,), sib).wait_recv()
        for s in range(ns):
            kind, r, cc = meta[nb + s]
            for t, frm in enumerate([me] + cids):
                blk = _slab(outs[nb + s], kind, frm, r, cc)
                _remote(blk, blk, (sm_s, sm_r), (s, t), sib).wait_recv()
        for cp in sends(srcs, outs, sems) + hands:
            cp.wait_send()

    dma = pltpu.SemaphoreType.DMA
    nb1, ns1 = max(nb, 1), max(ns, 1)
    return CommJob([a for a, _ in big + small],
                   [jax.ShapeDtypeStruct(_full_shape(kind, r, cc), BF16) for (kind, r, cc) in meta],
                   [dma((nb1,)), dma((nb1,)), dma((nb1, 3)), dma((nb1, 3)), dma((nb1, 3)), dma((nb1, 3)),
                    dma((ns1, 4)), dma((ns1, 4))], start, finish)


def run_job(name, job):
    n_in, n_out = len(job.operands), len(job.out_shapes)

    def body(*refs):
        ins, outs, sems = refs[:n_in], refs[n_in:n_in + n_out], refs[n_in + n_out:]
        job.start(ins, outs, sems)
        job.finish(ins, outs, sems)

    return pl.pallas_call(body, name=name, in_specs=[HBM_SPEC] * n_in, out_specs=[HBM_SPEC] * n_out,
                          out_shape=job.out_shapes, scratch_shapes=job.scratch)(*job.operands)


def sibling_swap(name, arrays):
    n = len(arrays)

    def body(*refs):
        srcs, outs, sems = refs[:n], refs[n:2 * n], refs[2 * n:]
        x, y, c, _ = _place()
        cps = [_remote(srcs[a], outs[a], sems, (a,), (x, y, 1 - c)) for a in range(n)]
        for cp in cps:
            cp.start()
        for cp in cps:
            cp.wait_recv()
        for cp in cps:
            cp.wait_send()

    return pl.pallas_call(
        body, name=name, in_specs=[HBM_SPEC] * n, out_specs=[HBM_SPEC] * n,
        out_shape=[jax.ShapeDtypeStruct(a.shape, a.dtype) for a in arrays],
        scratch_shapes=[pltpu.SemaphoreType.DMA((n,)), pltpu.SemaphoreType.DMA((n,))],
    )(*arrays)


def exchange_job(sums):
    n = len(sums)

    def sends(srcs, outs, sems):
        ici_s, ici_r, sib_s, sib_r = sems
        x, y, c, chips = _place()
        me, sib = 2 * x + y, (x, y, 1 - c)
        cps = []
        for a, (_, kind, r, cc) in enumerate(sums):
            for j, chip in enumerate(chips):
                cid = 2 * chip[0] + chip[1]
                cps.append(_remote(_slab(srcs[a], kind, cid, r, cc, c), outs[a].at[me], (ici_s, ici_r), (a, j), (*chip, c)))
            cps.append(_remote(_slab(srcs[a], kind, me, r, cc, 1 - c), outs[a].at[me], (sib_s, sib_r), (a,), sib))
        return cps

    def start(srcs, outs, sems):
        for cp in sends(srcs, outs, sems):
            cp.start()

    def finish(srcs, outs, sems):
        ici_s, ici_r, sib_s, sib_r = sems
        x, y, c, chips = _place()
        me, sib = 2 * x + y, (x, y, 1 - c)
        for a in range(n):
            for j, chip in enumerate(chips):
                blk = outs[a].at[2 * chip[0] + chip[1]]
                _remote(blk, blk, (ici_s, ici_r), (a, j), sib).wait_recv()
            _remote(outs[a].at[me], outs[a].at[me], (sib_s, sib_r), (a,), sib).wait_recv()
        for cp in sends(srcs, outs, sems):
            cp.wait_send()

    dma = pltpu.SemaphoreType.DMA
    return CommJob([s[0] for s in sums], [jax.ShapeDtypeStruct((4, r // 2, cc), BF16) for (_, _, r, cc) in sums],
                   [dma((n, 3)), dma((n, 3)), dma((n,)), dma((n,))], start, finish)


def allreduce_small(vec):
    R, C = vec.shape

    def body(src, out, gathered, send_sems, recv_sems):
        x, y, c, _ = _place()
        me = 4 * x + 2 * y + c
        gathered[me] = src[...]
        peers = [(fx, fy, fc) for fx in (0, 1) for fy in (0, 1) for fc in (0, 1) if (fx, fy, fc) != (0, 0, 0)]
        sends = []
        for j, (fx, fy, fc) in enumerate(peers):
            to = (x ^ fx, y ^ fy, c ^ fc)
            cp = pltpu.make_async_remote_copy(
                src_ref=src, dst_ref=gathered.at[me], send_sem=send_sems.at[j], recv_sem=recv_sems.at[j],
                device_id=to, device_id_type=MESH_IDS)
            cp.start()
            sends.append(cp)
        for j, (fx, fy, fc) in enumerate(peers):
            frm = 4 * (x ^ fx) + 2 * (y ^ fy) + (c ^ fc)
            pltpu.make_async_remote_copy(
                src_ref=src, dst_ref=gathered.at[frm], send_sem=send_sems.at[j], recv_sem=recv_sems.at[j],
                device_id=(x, y, c), device_id_type=MESH_IDS).wait_recv()
        for cp in sends:
            cp.wait_send()
        acc = gathered[0]
        for k in range(1, 8):
            acc = acc + gathered[k]
        out[...] = acc

    vm = pl.BlockSpec(memory_space=pltpu.VMEM)
    return pl.pallas_call(
        body, name="allreduce_small", in_specs=[vm], out_specs=vm,
        out_shape=jax.ShapeDtypeStruct((R, C), F32),
        scratch_shapes=[pltpu.VMEM((8, R, C), F32), pltpu.SemaphoreType.DMA((7,)), pltpu.SemaphoreType.DMA((7,))],
        compiler_params=_cparams(),
    )(vec)


def pair_sum(name, mine, theirs):
    rows, cols = mine.shape
    tr = _pick(rows, max(16, (1 << 20) // cols), 16)
    return rowwise(name, lambda a, b: a.astype(F32) + b.astype(F32), [_cols(mine), _cols(theirs)], [], [(cols, BF16)], [], tr)[0]


def chip_sum(name, rb):
    _, rh, C = rb.shape
    tr = _pick(rh, max(16, (1 << 19) // C), 16)

    def body(r_ref, o_ref):
        acc = r_ref[0].astype(F32)
        for k in range(1, 4):
            acc = acc + r_ref[k].astype(F32)
        o_ref[...] = acc

    return pl.pallas_call(body, name=name, grid=(rh // tr,),
                          in_specs=[pl.BlockSpec((4, tr, C), lambda i: (0, i, 0))],
                          out_specs=pl.BlockSpec((tr, C), lambda i: (i, 0)),
                          out_shape=jax.ShapeDtypeStruct((rh, C), F32),
                          compiler_params=_cparams(("parallel",)))(rb)


def adamw_halves(name, w, mine, theirs, m, v):
    rows, cols = w.shape
    tr = _pick(rows // 2, max(8, (1 << 19) // cols), 8)
    nbh = rows // 2 // tr
    full = pl.BlockSpec((tr, cols), lambda i: (i, 0))
    half = pl.BlockSpec((tr, cols), lambda i: (i % nbh, 0))

    def body(w_ref, a_ref, b_ref, m_ref, v_ref, g_out, d_out, m_out, v_out):
        is_mine = (pl.program_id(0) // nbh) == lax.axis_index("c")
        g = jnp.where(is_mine, a_ref[...], b_ref[...])
        d, mn, vn = _adamw_math(w_ref[...], g, m_ref[...], v_ref[...])
        g_out[...] = g
        d_out[...] = d
        m_out[...] = mn
        v_out[...] = vn

    return pl.pallas_call(body, name="adamw_" + name, grid=(rows // tr,), in_specs=[full, half, half, full, full],
                          out_specs=[full] * 4, out_shape=[jax.ShapeDtypeStruct((rows, cols), F32)] * 4,
                          compiler_params=_cparams(("parallel",)))(w, mine, theirs, m, v)


BIG = {"ffn1_wg": "col", "ffn1_wu": "col", "ffn1_wd": "row", "w_in": "slab", "w_branch": "row", "w_out": "row",
       "ffn2_wg": "col", "ffn2_wu": "col", "ffn2_wd": "row"}
LORA = ["gla_w_a2", "rwkv_w_w2", "rwkv_w_a2", "rwkv_w_g2"]
REPLICATED = ["ffn1_norm", "mix_norm", "gla_b_a", "gla_gn_w", "rwkv_mu", "rwkv_w0", "rwkv_a0", "rwkv_k_k", "rwkv_k_a",
              "rwkv_r_k", "rwkv_lnx_w", "rwkv_lnx_b", "gate_b", "ffn2_norm", "final_norm"]
WEIGHTS = ["ffn1_norm", "ffn1_wg", "ffn1_wu", "ffn1_wd", "mix_norm", "w_in", "gla_w_a2", "gla_b_a", "gla_gn_w", "rwkv_mu",
           "rwkv_w0", "rwkv_w_w2", "rwkv_a0", "rwkv_w_a2", "rwkv_w_g2", "rwkv_k_k", "rwkv_k_a", "rwkv_r_k", "rwkv_lnx_w",
           "rwkv_lnx_b", "gate_b", "w_branch", "w_out", "ffn2_norm", "ffn2_wg", "ffn2_wu", "ffn2_wd", "final_norm"]


def kernel(x, ffn1_norm, ffn1_wg, ffn1_wu, ffn1_wd, mix_norm, w_in, gla_w_a2, gla_b_a, gla_gn_w, rwkv_mu, rwkv_w0, rwkv_w_w2, rwkv_a0, rwkv_w_a2, rwkv_w_g2, rwkv_k_k, rwkv_k_a, rwkv_r_k, rwkv_lnx_w, rwkv_lnx_b, gate_b, w_branch, w_out, ffn2_norm, ffn2_wg, ffn2_wu, ffn2_wd, final_norm, loss_target, m_ffn1_norm, m_ffn1_wg, m_ffn1_wu, m_ffn1_wd, m_mix_norm, m_w_in, m_gla_w_a2, m_gla_b_a, m_gla_gn_w, m_rwkv_mu, m_rwkv_w0, m_rwkv_w_w2, m_rwkv_a0, m_rwkv_w_a2, m_rwkv_w_g2, m_rwkv_k_k, m_rwkv_k_a, m_rwkv_r_k, m_rwkv_lnx_w, m_rwkv_lnx_b, m_gate_b, m_w_branch, m_w_out, m_ffn2_norm, m_ffn2_wg, m_ffn2_wu, m_ffn2_wd, m_final_norm, v_ffn1_norm, v_ffn1_wg, v_ffn1_wu, v_ffn1_wd, v_mix_norm, v_w_in, v_gla_w_a2, v_gla_b_a, v_gla_gn_w, v_rwkv_mu, v_rwkv_w0, v_rwkv_w_w2, v_rwkv_a0, v_rwkv_w_a2, v_rwkv_w_g2, v_rwkv_k_k, v_rwkv_k_a, v_rwkv_r_k, v_rwkv_lnx_w, v_rwkv_lnx_b, v_gate_b, v_w_branch, v_w_out, v_ffn2_norm, v_ffn2_wg, v_ffn2_wu, v_ffn2_wd, v_final_norm):
    args = dict(locals())
    wts = {n: args[n] for n in WEIGHTS}
    moms = {n: args["m_" + n] for n in WEIGHTS}
    vars_ = {n: args["v_" + n] for n in WEIGHTS}

    xs = x[0]
    tgt = loss_target[0]
    S, D = xs.shape
    FF = ffn1_wd.shape[1] * 4
    gheads = gla_b_a.shape[-1] // GLA_DK
    GQ, GV = gheads * GLA_DK, gheads * GLA_DV
    rheads = rwkv_r_k.shape[1]
    RW = rheads * RW_HD
    NT = RW // 128
    lo_g = gla_w_a2.shape[1]
    lo_w = rwkv_w_w2.shape[1]
    lo_a = rwkv_w_a2.shape[1]
    assert rwkv_w_g2.shape[1] == GATE_LORA and RW % 128 == 0

    ow = rw_window(RW)
    WIN = -(-ow["used"] // (2 * D)) * (2 * D)
    lay = dict(gate=WIN, v=WIN + 2 * D, r=WIN + 2 * D + GV, q=WIN + 2 * D + 2 * GV, k=WIN + 2 * D + 2 * GV + GQ, a=ow["a"])
    DP = lay["k"] + GQ
    DIN = w_in.shape[-1] * 4
    o_sizes = [GQ, GQ, GV, GV, lo_g, RW, RW, RW, lo_w, lo_a, GATE_LORA, 2 * D]
    o_offs = [sum(o_sizes[:i]) for i in range(len(o_sizes))]
    assert o_offs[-1] + o_sizes[-1] == DIN
    p_offs = [lay["q"], lay["k"], lay["v"], lay["r"], ow["a"], ow["rr"], ow["rk"], ow["rv"], ow["wd"], ow["ad"], ow["gd"], lay["gate"]]

    def to_padded(w):
        order = sorted(range(len(o_sizes)), key=lambda i: p_offs[i])
        parts, pos = [], 0
        for i in order:
            if p_offs[i] > pos:
                parts.append(jnp.zeros((w.shape[0], p_offs[i] - pos), w.dtype))
            parts.append(w[:, o_offs[i]:o_offs[i] + o_sizes[i]])
            pos = p_offs[i] + o_sizes[i]
        if pos < DP:
            parts.append(jnp.zeros((w.shape[0], DP - pos), w.dtype))
        return jnp.concatenate(parts, axis=1)

    def from_padded(w):
        return jnp.concatenate([w[:, p_offs[i]:p_offs[i] + o_sizes[i]] for i in range(len(o_sizes))], axis=1)

    def pad_rows(w, rows):
        return jnp.pad(w, ((0, rows - w.shape[0]), (0, 0)))

    mu = rwkv_mu[0]
    mu_parts = {"rr": mu[0:RW], "rk": mu[RW:2 * RW], "rv": mu[2 * RW:3 * RW], "wd": mu[3 * RW:3 * RW + lo_w],
                "ad": mu[3 * RW + lo_w:3 * RW + lo_w + lo_a], "gd": mu[3 * RW + lo_w + lo_a:]}
    mu_win = jnp.zeros((WIN,), F32)
    for key, val in mu_parts.items():
        mu_win = lax.dynamic_update_slice(mu_win, val, (ow[key],))
    mu_win = mu_win.reshape(1, WIN)

    shard_shapes = {n: wts[n].shape[1:] for n in list(BIG) + LORA}
    W = {}

    def shard(n):
        return (wts[n][0].astype(BF16), BIG[n])

    def mm_gather(a, b, out_dtype, name, gather, lora=()):
        out, got = matmul(a, b, "nn", out_dtype, name,
                          job=gather_job([shard(n) for n in gather], [(wts[n][0].astype(BF16), "col") for n in lora]))
        W.update(zip(list(gather) + list(lora), got))
        return out

    W["ffn1_wg"] = run_job("gather_ffn1_wg", gather_job([shard("ffn1_wg")]))[0]
    r_k = rwkv_r_k.reshape(1, RW)
    fin_g = final_norm.reshape(1, D)

    TR = min(128, S)
    def swiglu_act(tag, gx, ux):
        return rowwise(tag + "_act", lambda a, b: f_swiglu(*f32(a, b)), [_cols(gx), _cols(ux)], [], [(FF, BF16)], [], TR)[0]

    h1 = rowwise("rms1", lambda a, g: f_rms(a, g), [_cols(xs)], [ffn1_norm], [(D, BF16)], [], TR)[0]
    g1 = mm_gather(h1, W["ffn1_wg"], F32, "ffn1_g", ["ffn1_wu"])
    u1 = mm_gather(h1, W["ffn1_wu"], F32, "ffn1_u", ["ffn1_wd"])
    act1 = swiglu_act("ffn1", g1, u1)
    f1 = mm_gather(act1, W["ffn1_wd"], F32, "ffn1_d", ["w_in"], LORA)
    w_in_p = to_padded(W["w_in"].transpose(1, 0, 2).reshape(D, DIN))
    gla_a2_p = pad_rows(W["gla_w_a2"], LORA_PAD)
    w_w2_p = pad_rows(W["rwkv_w_w2"], LORA_PAD)
    w_a2_p = pad_rows(W["rwkv_w_a2"], LORA_PAD)
    w_g2 = W["rwkv_w_g2"]

    def res_rms(coef):
        def fn(a, f, g):
            x1 = a + coef * f
            return x1, f_rms(x1, g)
        return fn

    x1, h2 = rowwise("res_rms_mix", res_rms(0.5), [_cols(xs), _cols(f1)], [mix_norm], [(D, F32), (D, BF16)], [], TR)
    p = mm_gather(h2, w_in_p, F32, "w_in", ["w_branch", "w_out", "ffn2_wg"])
    wb_g, wb_r = W["w_branch"][:GV], W["w_branch"][GV:]

    o_gla, gla_states = gla_forward(p, lay, gla_a2_p, gla_b_a, gla_gn_w, gheads)
    xsh = token_shift_forward(p, mu_win, WIN, TR)
    pre_consts = [rwkv_w0, w_w2_p, rwkv_a0, w_a2_p, w_g2, rwkv_k_k, rwkv_k_a]
    pre_fn = functools.partial(f_rw_pre, rw=RW)

    def pre_f32(xw, w0, ww, a0, wa, wg_, kk_, ka_):
        return pre_fn(xw, w0, ww.astype(F32), a0, wa.astype(F32), wg_.astype(F32), kk_, ka_)

    r_, dec_, k2_, v_, kk_, b_, g_ = rowwise("rw_pre", pre_f32, [_cols(xsh)], pre_consts, [(RW, F32)] * 7, [], 128)

    def tiles(a):
        return a.reshape(S, NT, 128)

    y_t, rw_states = rwkv_scan_forward(*(tiles(a) for a in (r_, dec_, k2_, v_, kk_, b_)))
    y_ = y_t.reshape(S, RW)
    post_consts = [rwkv_lnx_w, rwkv_lnx_b, r_k]
    o_rw = rowwise("rw_post", f_rw_post, [_cols(a) for a in (y_, r_, k2_, v_, g_)], post_consts, [(RW, BF16)], [], TR)[0]

    yg = matmul(o_gla, wb_g, "nn", F32, "branch_gla")
    yr = matmul(o_rw, wb_r, "nn", F32, "branch_rw")
    merge_fn = functools.partial(f_merge, d=D)
    merged = rowwise("merge", merge_fn, [_cols(p, 2 * D, lay["gate"]), _cols(yg), _cols(yr)], [gate_b], [(D, BF16)], [], TR)[0]
    mix = matmul(merged, W["w_out"], "nn", F32, "w_out")
    x2, h3 = rowwise("res_rms_ffn2", res_rms(1.0), [_cols(x1), _cols(mix)], [ffn2_norm], [(D, F32), (D, BF16)], [], TR)
    g3 = mm_gather(h3, W["ffn2_wg"], F32, "ffn2_g", ["ffn2_wu"])
    u3 = mm_gather(h3, W["ffn2_wu"], F32, "ffn2_u", ["ffn2_wd"])
    act3 = swiglu_act("ffn2", g3, u3)
    f3 = matmul(act3, W["ffn2_wd"], "nn", F32, "ffn2_d")

    def final_fn(a, f, t, g):
        def loss_of(a, f, g):
            yv = f_rms(a + 0.5 * f, g)
            return 0.5 * jnp.sum(jnp.mean(jnp.square(yv - t), axis=-1))
        val, vjp = jax.vjp(loss_of, a, f, g)
        da, df, dg = vjp(jnp.ones((), F32))
        return da, df, jnp.full((1, 128), val, F32), dg

    dx2, df3, loss_acc, d_final = rowwise("final_loss", final_fn, [_cols(x2), _cols(f3), _cols(tgt)], [fin_g],
                                          [(D, F32), (D, BF16)], [(1, 128), (1, D)], TR)
    grads = {"final_norm": d_final.reshape(D)}

    received = {}

    def pair_up(n, dw):
        r, cc = shard_shapes[n]
        flat = (4 * r, cc) if BIG[n] == "slab" else dw.shape
        theirs = sibling_swap("swap_" + n, [dw])[0]
        return n, (pair_sum("pair_" + n, dw.reshape(flat), theirs.reshape(flat)).reshape(dw.shape), BIG[n], r, cc)

    def mm_exchange(a, b, mode, out_dtype, name, pending):
        out, got = matmul(a, b, mode, out_dtype, name, job=exchange_job([entry for _, entry in pending]))
        received.update(zip([n for n, _ in pending], got))
        return out

    def ffn_backward(tag, h, gx, ux, act, df, wg, wu, wd):
        dact = matmul(df, wd, "nt", F32, tag + "_dact")
        p_wd = pair_up(tag + "_wd", matmul(act, df, "tn", BF16, tag + "_dwd"))

        def fn(a, b, d):
            _, vjp = jax.vjp(f_swiglu, *f32(a, b))
            return vjp(d)

        dgx, dux = rowwise(tag + "_dact_bwd", fn, [_cols(gx), _cols(ux), _cols(dact)], [], [(FF, BF16)] * 2, [], TR)
        p_wg = pair_up(tag + "_wg", mm_exchange(h, dgx, "tn", BF16, tag + "_dwg", [p_wd]))
        p_wu = pair_up(tag + "_wu", mm_exchange(h, dux, "tn", BF16, tag + "_dwu", [p_wg]))
        dha = mm_exchange(dgx, wg, "nt", F32, tag + "_dh_g", [p_wu])
        dhb = matmul(dux, wu, "nt", F32, tag + "_dh_u")
        return dha, dhb

    def res_rms_bwd(name, coef, a, f, g, dx1, dha, dhb):
        def fn(a, f, dx1, dha, dhb, g):
            _, vjp = jax.vjp(res_rms(coef), a, f, g)
            return vjp((dx1, dha + dhb))

        return rowwise(name, fn, [_cols(a), _cols(f), _cols(dx1), _cols(dha), _cols(dhb)], [g],
                       [(D, F32), (D, BF16)], [(1, D)], TR)

    dh3a, dh3b = ffn_backward("ffn2", h3, g3, u3, act3, df3, W["ffn2_wg"], W["ffn2_wu"], W["ffn2_wd"])
    dx1, dmix, grads["ffn2_norm"] = res_rms_bwd("res_rms_ffn2_bwd", 1.0, x1, mix, ffn2_norm, dx2, dh3a, dh3b)

    p_wo = pair_up("w_out", matmul(merged, dmix, "tn", BF16, "d_w_out"))
    dmerged = mm_exchange(dmix, W["w_out"], "nt", F32, "d_merged", [p_wo])

    def merge_bwd(gp, a, b, d, gb):
        _, vjp = jax.vjp(merge_fn, gp, a, b, gb)
        return vjp(d)

    dgate, dyg, dyr, grads["gate_b"] = rowwise(
        "merge_bwd", merge_bwd, [_cols(p, 2 * D, lay["gate"]), _cols(yg), _cols(yr), _cols(dmerged)], [gate_b],
        [(2 * D, BF16), (D, BF16), (D, BF16)], [(1, 2 * D)], TR)
    do_gla = matmul(dyg, wb_g, "nt", BF16, "d_o_gla")
    do_rw = matmul(dyr, wb_r, "nt", F32, "d_o_rw")
    p_wb = pair_up("w_branch", jnp.concatenate([matmul(o_gla, dyg, "tn", BF16, "d_wb_gla"),
                                                matmul(o_rw, dyr, "tn", BF16, "d_wb_rw")], axis=0))

    def post_bwd(yv, rv, kv, vv, gv_, d, lw, lb, rk):
        _, vjp = jax.vjp(f_rw_post, yv, rv, kv, vv, gv_, lw, lb, rk)
        return vjp(d)

    dy_, dr_p, dk2_p, dv_p, dg_p, grads["rwkv_lnx_w"], grads["rwkv_lnx_b"], d_rk = rowwise(
        "rw_post_bwd", post_bwd, [_cols(a) for a in (y_, r_, k2_, v_, g_, do_rw)], post_consts,
        [(RW, F32)] * 5, [(1, RW)] * 3, 128)
    grads["rwkv_r_k"] = d_rk.reshape(rwkv_r_k.shape[1:])

    scan_cots = rwkv_scan_backward(*(tiles(a) for a in (r_, dec_, k2_, v_, kk_, b_)), rw_states, tiles(dy_))
    dr_s, dw_s, dk2_s, dv_s, dkk_s, db_s = (a.reshape(S, RW) for a in scan_cots)

    def pre_bwd(xw, c0, c1, c2, c3, c4, c5, c6, c7, c8, c9, w0, ww, a0, wa, wg_, kk_c, ka_c):
        _, vjp = jax.vjp(pre_fn, xw, w0, ww.astype(F32), a0, wa.astype(F32), wg_.astype(F32), kk_c, ka_c)
        return vjp((c0 + c6, c1, c2 + c7, c3 + c8, c4, c5, c9))

    dxsh, grads["rwkv_w0"], d_ww2, grads["rwkv_a0"], d_wa2, d_wg2, grads["rwkv_k_k"], grads["rwkv_k_a"] = rowwise(
        "rw_pre_bwd", pre_bwd,
        [_cols(xsh)] + [_cols(a) for a in (dr_s, dw_s, dk2_s, dv_s, dkk_s, db_s, dr_p, dk2_p, dv_p, dg_p)], pre_consts,
        [(WIN, F32)], [(1, RW), w_w2_p.shape, (1, RW), w_a2_p.shape, w_g2.shape, (1, RW), (1, RW)], 128)
    grads["rwkv_w_w2"], grads["rwkv_w_a2"], grads["rwkv_w_g2"] = d_ww2[:lo_w], d_wa2[:lo_a], d_wg2

    dq, dk, dv, dr, da, d_ga2, grads["gla_b_a"], grads["gla_gn_w"] = gla_backward(
        p, lay, gla_states, do_gla, gla_a2_p, gla_b_a, gla_gn_w, gheads)
    grads["gla_w_a2"] = d_ga2[:lo_g]

    dpw, dmu_win = token_shift_backward(p, dxsh, da, mu_win, WIN, ow["a"], TR)
    dmu = dmu_win[0]
    grads["rwkv_mu"] = jnp.concatenate([dmu[ow[k_]:ow[k_] + mu_parts[k_].shape[0]] for k_ in ("rr", "rk", "rv", "wd", "ad", "gd")]).reshape(1, -1)

    dp = jnp.concatenate([dpw, dgate, dv, dr, dq, dk], axis=1)
    d_w_in = from_padded(mm_exchange(h2, dp, "tn", BF16, "d_w_in", [p_wb]))
    p_wi = pair_up("w_in", d_w_in.reshape(D, 4, shard_shapes["w_in"][1]).transpose(1, 0, 2))
    dh2 = mm_exchange(dp, w_in_p, "nt", F32, "d_h2", [p_wi])
    zeros_d = jnp.zeros_like(dh2)
    dx0, df1, grads["mix_norm"] = res_rms_bwd("res_rms_mix_bwd", 0.5, xs, f1, mix_norm, dx1, dh2, zeros_d)

    dh1a, dh1b = ffn_backward("ffn1", h1, g1, u1, act1, df1, W["ffn1_wg"], W["ffn1_wu"], W["ffn1_wd"])

    def rms1_bwd(a, dha, dhb, dxa, g):
        _, vjp = jax.vjp(f_rms, a, g)
        da_, dg_ = vjp(dha + dhb)
        return da_ + dxa, dg_

    grad_x, grads["ffn1_norm"] = rowwise("rms1_bwd", rms1_bwd, [_cols(xs), _cols(dh1a), _cols(dh1b), _cols(dx0)],
                                         [ffn1_norm], [(D, F32)], [(1, D)], TR)

    names = list(BIG)
    halves = [chip_sum("chip_sum_" + n, received[n]) for n in names]
    others = sibling_swap("sibling_join", halves)
    final_grads, delta, new_m, new_v = {}, {}, {}, {}
    for n, h, o in zip(names, halves, others):
        res = adamw_halves(n, wts[n][0], h, o, moms[n][0], vars_[n][0])
        final_grads[n], delta[n], new_m[n], new_v[n] = (a.reshape(wts[n].shape) for a in res)

    rep_flat = jnp.concatenate([grads[n].reshape(-1) for n in REPLICATED + LORA])
    rep_rows = -(-rep_flat.shape[0] // 1024) * 8
    rep_sum = allreduce_small(jnp.pad(rep_flat, (0, rep_rows * 128 - rep_flat.shape[0])).reshape(rep_rows, 128)).reshape(-1)
    my_chip = 2 * lax.axis_index("x") + lax.axis_index("y")
    off = 0
    for n in REPLICATED + LORA:
        size = grads[n].size
        full = rep_sum[off:off + size].reshape(grads[n].shape)
        off += size
        if n in LORA:
            cc = shard_shapes[n][1]
            full = lax.dynamic_slice_in_dim(full, my_chip * cc, cc, axis=1)
        final_grads[n] = full.reshape(wts[n].shape)

    loss = lax.psum(loss_acc[0, 0], ("x", "y", "c"))

    small = REPLICATED + LORA

    def two(a):
        return a.reshape(-1, a.shape[-1])

    ds, ms_, vs_ = adamw_small([two(wts[n]) for n in small], [two(final_grads[n]) for n in small],
                               [two(moms[n]) for n in small], [two(vars_[n]) for n in small])
    for i, n in enumerate(small):
        shp = wts[n].shape
        delta[n], new_m[n], new_v[n] = ds[i].reshape(shp), ms_[i].reshape(shp), vs_[i].reshape(shp)

    return (loss, grad_x.reshape(x.shape), *[final_grads[n] for n in WEIGHTS], *[delta[n] for n in WEIGHTS],
            *[new_m[n] for n in WEIGHTS], *[new_v[n] for n in WEIGHTS])
```

```python
import functools
import math

import jax
import jax.numpy as jnp
from jax import lax
from jax.experimental import pallas as pl
from jax.experimental.pallas import tpu as pltpu

F32 = jnp.float32
BF16 = jnp.bfloat16
MESH_IDS = pl.DeviceIdType.MESH

NORM_EPS = 1e-6
GN_EPS = 64e-5
GLA_TAU = 16.0
CHUNK = 64
GLA_DK = 128
GLA_DV = 256
RW_HD = 64
LORA_PAD = 128
GATE_LORA = 256
ADAM_LR, ADAM_B1, ADAM_B2, ADAM_EPS, ADAM_WD, ADAM_STEP = 0.001, 0.9, 0.999, 1e-08, 0.01, 10

VMEM_LIMIT_BYTES = 56 * 1024 * 1024
HBM_SPEC = pl.BlockSpec(memory_space=pltpu.HBM)


def _cparams(sem=None):
    return pltpu.CompilerParams(dimension_semantics=sem, vmem_limit_bytes=VMEM_LIMIT_BYTES)


def _pick(n, target, mult=128):
    best = None
    for t in range(mult, min(n, target) + 1, mult):
        if n % t == 0:
            best = t
    return best if best is not None else n


_NN = (((1,), (0,)), ((), ()))
_NT = (((1,), (1,)), ((), ()))
_TN = (((0,), (0,)), ((), ()))


def _dg(a, b, dims):
    return lax.dot_general(a.astype(BF16), b.astype(BF16), dims, preferred_element_type=F32)


@jax.custom_vjp
def mm_nn(a, b):
    return _dg(a, b, _NN)


def _mm_nn_fwd(a, b):
    return _dg(a, b, _NN), (a, b)


def _mm_nn_bwd(res, g):
    a, b = res
    return _dg(g, b, _NT).astype(a.dtype), _dg(a, g, _TN).astype(b.dtype)


mm_nn.defvjp(_mm_nn_fwd, _mm_nn_bwd)


@jax.custom_vjp
def mm_nt(a, b):
    return _dg(a, b, _NT)


def _mm_nt_fwd(a, b):
    return _dg(a, b, _NT), (a, b)


def _mm_nt_bwd(res, g):
    a, b = res
    return _dg(g, b, _NN).astype(a.dtype), _dg(g, a, _TN).astype(b.dtype)


mm_nt.defvjp(_mm_nt_fwd, _mm_nt_bwd)


@jax.custom_vjp
def mm_tn(a, b):
    return _dg(a, b, _TN)


def _mm_tn_fwd(a, b):
    return _dg(a, b, _TN), (a, b)


def _mm_tn_bwd(res, g):
    a, b = res
    return _dg(b, g, _NT).astype(a.dtype), _dg(a, g, _NN).astype(b.dtype)


mm_tn.defvjp(_mm_tn_fwd, _mm_tn_bwd)


def _split3(x):
    h = x.astype(BF16)
    r = x - h.astype(F32)
    m = r.astype(BF16)
    l = (r - m.astype(F32)).astype(BF16)
    return h, m, l


def _block_ones(n, seg):
    i = lax.broadcasted_iota(jnp.int32, (n, n), 0) // seg
    j = lax.broadcasted_iota(jnp.int32, (n, n), 1) // seg
    return (i == j).astype(BF16)


def _segsum_raw(x, seg, terms):
    ones = _block_ones(128, seg)
    outs = []
    for j in range(x.shape[1] // 128):
        t = x[:, j * 128:(j + 1) * 128]
        parts = _split3(t)[:terms]
        acc = jnp.dot(parts[0], ones, preferred_element_type=F32)
        for p_ in parts[1:]:
            acc = acc + jnp.dot(p_, ones, preferred_element_type=F32)
        outs.append(acc)
    return outs[0] if len(outs) == 1 else jnp.concatenate(outs, axis=1)


@jax.custom_vjp
def segsum64(x):
    return _segsum_raw(x, RW_HD, 3)


segsum64.defvjp(lambda x: (_segsum_raw(x, RW_HD, 3), None), lambda _, g: (_segsum_raw(g, RW_HD, 3),))


def _tri(n, upper):
    i = lax.broadcasted_iota(jnp.int32, (n, n), 0)
    j = lax.broadcasted_iota(jnp.int32, (n, n), 1)
    return ((i <= j) if upper else (i >= j)).astype(BF16)


def _tri_mm(x, upper):
    t = _tri(x.shape[0], upper)
    h, m, l = _split3(x)
    return (jnp.dot(t, h, preferred_element_type=F32) + jnp.dot(t, m, preferred_element_type=F32)
            + jnp.dot(t, l, preferred_element_type=F32))


@jax.custom_vjp
def cumsum_rows(x):
    return _tri_mm(x, False)


cumsum_rows.defvjp(lambda x: (_tri_mm(x, False), None), lambda _, g: (_tri_mm(g, True),))


def _make_split(sizes, axis):
    offs = [sum(sizes[:i]) for i in range(len(sizes))]

    def cut(x):
        if axis == 1:
            return tuple(x[:, o:o + s] for o, s in zip(offs, sizes))
        return tuple(x[o:o + s, :] for o, s in zip(offs, sizes))

    @jax.custom_vjp
    def split(x):
        return cut(x)

    split.defvjp(lambda x: (cut(x), None), lambda _, gs: (jnp.concatenate(gs, axis=axis),))
    return split


@jax.custom_vjp
def log_sigmoid(z):
    return jnp.minimum(z, 0.0) - jnp.log(1.0 + jnp.exp(-jnp.abs(z)))


log_sigmoid.defvjp(lambda z: (log_sigmoid(z), z), lambda z, g: (g * (1.0 - jax.nn.sigmoid(z)),))


def silu(x):
    return x * jax.nn.sigmoid(x)


def matmul(a, b, mode, out_dtype, name, tm=1024, tn=512, tk=2048, job=None):
    if mode == "nn":
        (M, K), (K2, N) = a.shape, b.shape
    elif mode == "nt":
        (M, K), (N, K2) = a.shape, b.shape
    else:
        (K, M), (K2, N) = a.shape, b.shape
    assert K == K2, (name, a.shape, b.shape)
    tm, tn, tk = _pick(M, tm), _pick(N, tn), _pick(K, tk)
    grid = (M // tm, N // tn, K // tk)
    dims = {"nn": _NN, "nt": _NT, "tn": _TN}[mode]
    a_spec = pl.BlockSpec((tk, tm), lambda i, j, k: (k, i)) if mode == "tn" else pl.BlockSpec((tm, tk), lambda i, j, k: (i, k))
    b_spec = pl.BlockSpec((tn, tk), lambda i, j, k: (j, k)) if mode == "nt" else pl.BlockSpec((tk, tn), lambda i, j, k: (k, j))
    n_in = 0 if job is None else len(job.operands)
    n_out = 0 if job is None else len(job.out_shapes)

    def body(a_ref, b_ref, *rest):
        job_ins, o_ref, job_outs = rest[:n_in], rest[n_in], rest[n_in + 1:n_in + 1 + n_out]
        acc_ref, sems = rest[n_in + 1 + n_out], rest[n_in + 2 + n_out:]
        i, j, k = pl.program_id(0), pl.program_id(1), pl.program_id(2)
        if job is not None:
            @pl.when((i == 0) & (j == 0) & (k == 0))
            def _():
                job.start(job_ins, job_outs, sems)

        part = _dg(a_ref[...], b_ref[...], dims)

        @pl.when(k == 0)
        def _():
            acc_ref[...] = part

        @pl.when(k > 0)
        def _():
            acc_ref[...] += part

        @pl.when(k == grid[2] - 1)
        def _():
            o_ref[...] = acc_ref[...].astype(o_ref.dtype)

        if job is not None:
            @pl.when((i == grid[0] - 1) & (j == grid[1] - 1) & (k == grid[2] - 1))
            def _():
                job.finish(job_ins, job_outs, sems)

    main_out = jax.ShapeDtypeStruct((M, N), out_dtype)
    main_spec = pl.BlockSpec((tm, tn), lambda i, j, k: (i, j))
    if job is None:
        return pl.pallas_call(
            body, name=name, grid=grid, in_specs=[a_spec, b_spec], out_specs=main_spec, out_shape=main_out,
            scratch_shapes=[pltpu.VMEM((tm, tn), F32)],
            compiler_params=_cparams(("parallel", "parallel", "arbitrary")),
        )(a, b)
    res = pl.pallas_call(
        body, name=name, grid=grid, in_specs=[a_spec, b_spec] + [HBM_SPEC] * n_in,
        out_specs=[main_spec] + [HBM_SPEC] * n_out, out_shape=[main_out] + list(job.out_shapes),
        scratch_shapes=[pltpu.VMEM((tm, tn), F32)] + list(job.scratch),
        compiler_params=_cparams(("arbitrary", "arbitrary", "arbitrary")),
    )(a, b, *job.operands)
    return res[0], res[1:]


def _cols(arr, width=None, off=0):
    width = arr.shape[1] if width is None else width
    assert off % width == 0, (off, width)
    return (arr, width, off // width)


def rowwise(name, fn, rows, consts, row_outs, acc_outs, tr, extra_specs=()):
    S = rows[0][0].shape[0]
    tr = min(tr, S)
    assert S % tr == 0
    n_in = len(rows) + len(extra_specs) + len(consts)
    n_ro = len(row_outs)
    in_specs = [pl.BlockSpec((tr, w), functools.partial(lambda i, cb: (i, cb), cb=cb)) for (_, w, cb) in rows]
    in_specs += [spec for (_, spec) in extra_specs]
    in_specs += [pl.BlockSpec(c.shape, lambda i: (0, 0)) for c in consts]
    out_shape = [jax.ShapeDtypeStruct((S, w), dt) for (w, dt) in row_outs]
    out_shape += [jax.ShapeDtypeStruct(shp, F32) for shp in acc_outs]
    out_specs = [pl.BlockSpec((tr, w), lambda i: (i, 0)) for (w, _) in row_outs]
    out_specs += [pl.BlockSpec(shp, lambda i: (0, 0)) for shp in acc_outs]

    def body(*refs):
        ins = [r[...] for r in refs[:n_in]]
        outs = fn(*ins)
        outs = outs if isinstance(outs, (tuple, list)) else (outs,)
        assert len(outs) == n_ro + len(acc_outs), (name, len(outs))
        for o_ref, val in zip(refs[n_in:n_in + n_ro], outs[:n_ro]):
            o_ref[...] = val.astype(o_ref.dtype)
        i = pl.program_id(0)
        for a_ref, val in zip(refs[n_in + n_ro:], outs[n_ro:]):
            @pl.when(i == 0)
            def _(a_ref=a_ref, val=val):
                a_ref[...] = val.astype(F32)

            @pl.when(i > 0)
            def _(a_ref=a_ref, val=val):
                a_ref[...] += val.astype(F32)

    res = pl.pallas_call(
        body, name=name, grid=(S // tr,), in_specs=in_specs, out_specs=out_specs, out_shape=out_shape,
        compiler_params=_cparams(("arbitrary",) if acc_outs else ("parallel",)),
    )(*[r[0] for r in rows], *[e[0] for e in extra_specs], *consts)
    return res


def f32(*xs):
    return [x.astype(F32) for x in xs]


def f_rms(x, g):
    return x * lax.rsqrt(jnp.mean(x * x, axis=-1, keepdims=True) + NORM_EPS) * g


def f_swiglu(gx, ux):
    return silu(gx) * ux


def f_merge(gp, yg, yr, gate_b, d):
    gates = jax.nn.sigmoid(gp + gate_b)
    g1, g2 = _make_split((d, d), 1)(gates)
    return g1 * yg + g2 * yr


def rw_window(rw):
    o = dict(rr=0, rk=rw, rv=2 * rw, gd=3 * rw, wd=3 * rw + GATE_LORA)
    o["ad"] = o["wd"] + LORA_PAD
    o["a"] = o["ad"] + LORA_PAD
    o["used"] = o["a"] + LORA_PAD
    return o


def f_rw_pre(xs, w0, w_w2, a0, w_a2, w_g2, k_k, k_a, rw):
    win = xs.shape[1]
    o = rw_window(rw)
    sizes = (rw, rw, rw, GATE_LORA, LORA_PAD, LORA_PAD, win - o["a"])
    rr, rk, rv, gd, wd, ad, _ = _make_split(sizes, 1)(xs)
    w_raw = w0 + mm_nn(jnp.tanh(wd), w_w2)
    dec = jnp.exp(-jnp.exp(log_sigmoid(w_raw) - 0.5))
    a = jax.nn.sigmoid(a0 + mm_nn(ad, w_a2))
    g = mm_nn(jax.nn.sigmoid(gd), w_g2)
    kx = rk * k_k
    kk = kx / jnp.maximum(jnp.sqrt(segsum64(kx * kx)), 1e-12)
    k2 = rk * (1.0 + (a - 1.0) * k_a)
    return rr, dec, k2, rv, kk, kk * a, g


def f_rw_post(y, r, k2, v, g, lnx_w, lnx_b, r_k):
    mu = segsum64(y) * (1.0 / RW_HD)
    yc = y - mu
    var = segsum64(yc * yc) * (1.0 / RW_HD)
    yn = yc * lax.rsqrt(var + GN_EPS) * lnx_w + lnx_b
    bonus = segsum64(r * k2 * r_k) * v
    return (yn + bonus) * g


def f_gla_chunk(q, k, v, r, a, st_prev, w_a2, b_a, gn_w, heads):
    z = mm_nn(a, w_a2) + b_a
    la = log_sigmoid(z) * (1.0 / GLA_TAU)
    cum = cumsum_rows(la)
    total = jnp.sum(la, axis=0, keepdims=True)
    kdec = k * jnp.exp(total - cum)
    et = jnp.exp(total)
    qs = q * (GLA_DK ** -0.5)
    sk = _make_split((GLA_DK,) * heads, 1)
    sv = _make_split((GLA_DV,) * heads, 1)
    ss = _make_split((GLA_DV,) * heads, 0)
    kd_h, q_h, et_h, v_h, st_h = sk(kdec), sk(qs), sk(et), sv(v), ss(st_prev)
    outs, news = [], []
    for h in range(heads):
        st_new = st_h[h] * et_h[h] + mm_tn(v_h[h], kd_h[h])
        o = mm_nt(q_h[h], st_new)
        o = o * lax.rsqrt(jnp.mean(o * o, axis=-1, keepdims=True) + NORM_EPS) * gn_w
        outs.append(o)
        news.append(st_new)
    o_all = outs[0] if heads == 1 else jnp.concatenate(outs, axis=1)
    st_all = news[0] if heads == 1 else jnp.concatenate(news, axis=0)
    return o_all * silu(r), st_all


def gla_forward(p, lay, w_a2, b_a, gn_w, heads):
    S = p.shape[0]
    nc = S // CHUNK
    gq, gv = heads * GLA_DK, heads * GLA_DV

    def spec(width, off, rev=False):
        assert off % width == 0
        return pl.BlockSpec((CHUNK, width), functools.partial(lambda n, cb: (n, cb), cb=off // width))

    def body(q_ref, k_ref, v_ref, r_ref, a_ref, w_ref, b_ref, g_ref, o_ref, st_out_ref, st_sc):
        @pl.when(pl.program_id(0) == 0)
        def _():
            st_sc[...] = jnp.zeros_like(st_sc)

        st_prev = st_sc[...]
        st_out_ref[0] = st_prev
        o, st_new = f_gla_chunk(*f32(q_ref[...], k_ref[...], v_ref[...], r_ref[...], a_ref[...]), st_prev,
                                w_ref[...], b_ref[...], g_ref[...], heads)
        o_ref[...] = o.astype(o_ref.dtype)
        st_sc[...] = st_new

    return pl.pallas_call(
        body, name="gla_fwd", grid=(nc,),
        in_specs=[spec(gq, lay["q"]), spec(gq, lay["k"]), spec(gv, lay["v"]), spec(gv, lay["r"]), spec(LORA_PAD, lay["a"]),
                  pl.BlockSpec(w_a2.shape, lambda n: (0, 0)), pl.BlockSpec(b_a.shape, lambda n: (0, 0)),
                  pl.BlockSpec(gn_w.shape, lambda n: (0, 0))],
        out_specs=[pl.BlockSpec((CHUNK, gv), lambda n: (n, 0)), pl.BlockSpec((1, gv, GLA_DK), lambda n: (n, 0, 0))],
        out_shape=[jax.ShapeDtypeStruct((S, gv), BF16), jax.ShapeDtypeStruct((nc, gv, GLA_DK), F32)],
        scratch_shapes=[pltpu.VMEM((gv, GLA_DK), F32)],
        compiler_params=_cparams(("arbitrary",)),
    )(p, p, p, p, p, w_a2, b_a, gn_w)


def gla_backward(p, lay, states, d_out, w_a2, b_a, gn_w, heads):
    S = p.shape[0]
    nc = S // CHUNK
    gq, gv = heads * GLA_DK, heads * GLA_DV

    def spec(width, off):
        assert off % width == 0
        return pl.BlockSpec((CHUNK, width), functools.partial(lambda n, cb: (nc - 1 - n, cb), cb=off // width))

    def rev(width):
        return pl.BlockSpec((CHUNK, width), lambda n: (nc - 1 - n, 0))

    def whole(arr):
        return pl.BlockSpec(arr.shape, lambda n: (0, 0))

    def body(q_ref, k_ref, v_ref, r_ref, a_ref, st_ref, do_ref, w_ref, b_ref, g_ref,
             dq_ref, dk_ref, dv_ref, dr_ref, da_ref, dw_ref, db_ref, dg_ref, dst_sc):
        n = pl.program_id(0)

        @pl.when(n == 0)
        def _():
            dst_sc[...] = jnp.zeros_like(dst_sc)

        fn = functools.partial(f_gla_chunk, heads=heads)
        prim = (*f32(q_ref[...], k_ref[...], v_ref[...], r_ref[...], a_ref[...]), st_ref[0],
                w_ref[...].astype(F32), b_ref[...], g_ref[...])
        _, vjp = jax.vjp(fn, *prim)
        dq, dk, dv, dr, da, dst, dw, db, dg = vjp((do_ref[...].astype(F32), dst_sc[...]))
        for ref, val in ((dq_ref, dq), (dk_ref, dk), (dv_ref, dv), (dr_ref, dr), (da_ref, da)):
            ref[...] = val.astype(ref.dtype)
        dst_sc[...] = dst

        @pl.when(n == 0)
        def _():
            dw_ref[...] = dw
            db_ref[...] = db
            dg_ref[...] = dg

        @pl.when(n > 0)
        def _():
            dw_ref[...] += dw
            db_ref[...] += db
            dg_ref[...] += dg

    return pl.pallas_call(
        body, name="gla_bwd", grid=(nc,),
        in_specs=[spec(gq, lay["q"]), spec(gq, lay["k"]), spec(gv, lay["v"]), spec(gv, lay["r"]), spec(LORA_PAD, lay["a"]),
                  pl.BlockSpec((1, gv, GLA_DK), lambda n: (nc - 1 - n, 0, 0)), rev(gv),
                  whole(w_a2), whole(b_a), whole(gn_w)],
        out_specs=[rev(gq), rev(gq), rev(gv), rev(gv), rev(LORA_PAD), whole(w_a2), whole(b_a), whole(gn_w)],
        out_shape=[jax.ShapeDtypeStruct((S, gq), BF16), jax.ShapeDtypeStruct((S, gq), BF16),
                   jax.ShapeDtypeStruct((S, gv), BF16), jax.ShapeDtypeStruct((S, gv), BF16),
                   jax.ShapeDtypeStruct((S, LORA_PAD), F32),
                   jax.ShapeDtypeStruct(w_a2.shape, F32), jax.ShapeDtypeStruct(b_a.shape, F32),
                   jax.ShapeDtypeStruct(gn_w.shape, F32)],
        scratch_shapes=[pltpu.VMEM((gv, GLA_DK), F32)],
        compiler_params=_cparams(("arbitrary",)),
    )(p, p, p, p, p, states, d_out, w_a2, b_a, gn_w)


SCAN_BLOCK = 32


def _scan_helpers(nt):
    ones = _block_ones(128, RW_HD)
    rows = lax.broadcasted_iota(jnp.int32, (nt * RW_HD, 128), 0) % RW_HD
    lanes = lax.broadcasted_iota(jnp.int32, (nt * RW_HD, 128), 1) % RW_HD
    eye = rows == lanes

    def bc(ref, t):
        parts = [jnp.broadcast_to(ref[t, j:j + 1, :], (RW_HD, 128)) for j in range(nt)]
        return parts[0] if nt == 1 else jnp.concatenate(parts, axis=0)

    def seg1(x):
        return jnp.dot(x.astype(BF16), ones, preferred_element_type=F32)

    def column(ref, t):
        return seg1(jnp.where(eye, bc(ref, t), 0.0))

    def put_diag(ref, t, x):
        put_colsum(ref, t, jnp.where(eye, x, 0.0))

    def put_colsum(ref, t, x, sign=1.0):
        for j in range(nt):
            ref[t, j:j + 1, :] = sign * jnp.sum(x[j * RW_HD:(j + 1) * RW_HD, :], axis=0, keepdims=True)

    return bc, seg1, column, put_diag, put_colsum


def rwkv_scan_forward(r, w, k2, v, kk, b):
    S, nt, _ = r.shape
    tb = min(SCAN_BLOCK, S)

    def body(r_ref, w_ref, k2_ref, v_ref, kk_ref, b_ref, y_ref, st_ref, s_sc):
        @pl.when(pl.program_id(0) == 0)
        def _():
            s_sc[...] = jnp.zeros_like(s_sc)

        bc, seg1, column, put_diag, put_colsum = _scan_helpers(nt)

        def step(t, carry):
            s = s_sc[...]
            st_ref[t] = s
            sa_e = seg1(s * bc(kk_ref, t))
            s = s * bc(w_ref, t) - sa_e * bc(b_ref, t) + column(v_ref, t) * bc(k2_ref, t)
            s_sc[...] = s
            put_diag(y_ref, t, seg1(s * bc(r_ref, t)))
            return carry

        lax.fori_loop(0, tb, step, 0, unroll=8)

    row = pl.BlockSpec((tb, nt, 128), lambda i: (i, 0, 0))
    return pl.pallas_call(
        body, name="rwkv_scan_fwd", grid=(S // tb,),
        in_specs=[row] * 6,
        out_specs=[row, pl.BlockSpec((tb, nt * RW_HD, 128), lambda i: (i, 0, 0))],
        out_shape=[jax.ShapeDtypeStruct((S, nt, 128), F32), jax.ShapeDtypeStruct((S, nt * RW_HD, 128), F32)],
        scratch_shapes=[pltpu.VMEM((nt * RW_HD, 128), F32)],
        compiler_params=_cparams(("arbitrary",)),
    )(r, w, k2, v, kk, b)


def rwkv_scan_backward(r, w, k2, v, kk, b, states, dy):
    S, nt, _ = r.shape
    tb = min(SCAN_BLOCK, S)
    nb = S // tb

    def body(r_ref, w_ref, k2_ref, v_ref, kk_ref, b_ref, st_ref, dy_ref,
             dr_ref, dw_ref, dk2_ref, dv_ref, dkk_ref, db_ref, ds_sc):
        @pl.when(pl.program_id(0) == 0)
        def _():
            ds_sc[...] = jnp.zeros_like(ds_sc)

        bc, seg1, column, put_diag, put_colsum = _scan_helpers(nt)

        def step(i, carry):
            t = tb - 1 - i
            s_prev = st_ref[t]
            r_e, w_e, k2_e, kk_e, b_e = (bc(ref, t) for ref in (r_ref, w_ref, k2_ref, kk_ref, b_ref))
            v_e = column(v_ref, t)
            sa_e = seg1(s_prev * kk_e)
            s_new = s_prev * w_e - sa_e * b_e + v_e * k2_e
            dy_e = column(dy_ref, t)
            put_colsum(dr_ref, t, s_new * dy_e)
            ds = ds_sc[...] + dy_e * r_e
            put_colsum(dw_ref, t, ds * s_prev)
            nsa_e = seg1(ds * b_e)
            put_colsum(db_ref, t, ds * sa_e, -1.0)
            put_diag(dv_ref, t, seg1(ds * k2_e))
            put_colsum(dk2_ref, t, ds * v_e)
            put_colsum(dkk_ref, t, s_prev * nsa_e, -1.0)
            ds_sc[...] = ds * w_e - nsa_e * kk_e
            return carry

        lax.fori_loop(0, tb, step, 0, unroll=4)

    row = pl.BlockSpec((tb, nt, 128), lambda i: (nb - 1 - i, 0, 0))
    return pl.pallas_call(
        body, name="rwkv_scan_bwd", grid=(nb,),
        in_specs=[row] * 6 + [pl.BlockSpec((tb, nt * RW_HD, 128), lambda i: (nb - 1 - i, 0, 0)), row],
        out_specs=[row] * 6,
        out_shape=[jax.ShapeDtypeStruct((S, nt, 128), F32)] * 6,
        scratch_shapes=[pltpu.VMEM((nt * RW_HD, 128), F32)],
        compiler_params=_cparams(("arbitrary",)),
    )(r, w, k2, v, kk, b, states, dy)


def _edge_spec(width, col_block, tr, n_rows, after):
    last = n_rows // 8 - 1
    if after:
        return pl.BlockSpec((8, width), lambda i: (jnp.minimum((i + 1) * (tr // 8), last), col_block))
    return pl.BlockSpec((8, width), lambda i: (jnp.maximum(i * (tr // 8) - 1, 0), col_block))


def _shifted_prev(p, prev8):
    first = jnp.where(pl.program_id(0) == 0, 0.0, prev8[7:8, :])
    rows = lax.broadcasted_iota(jnp.int32, p.shape, 0)
    return jnp.where(rows == 0, first, pltpu.roll(p, 1, axis=0))


def token_shift_forward(p, mu_win, win, tr):
    def fn(pw, prev8, mu):
        return pw + mu * (_shifted_prev(pw, prev8) - pw)

    return rowwise("token_shift_fwd", fn, [_cols(p, win, 0)], [mu_win], [(win, F32)], [], tr,
                   extra_specs=[(p, _edge_spec(win, 0, tr, p.shape[0], False))])[0]


def token_shift_backward(p, dxs, da_gla, mu_win, win, a_off, tr):
    S = p.shape[0]
    n = S // min(tr, S)

    def fn(pw, dx, da, prev8, next8, mu):
        trr = pw.shape[0]
        last = jnp.where(pl.program_id(0) == n - 1, 0.0, next8[0:1, :])
        rows = lax.broadcasted_iota(jnp.int32, dx.shape, 0)
        dnext = jnp.where(rows == trr - 1, last, pltpu.roll(dx, trr - 1, axis=0))
        dp = (1.0 - mu) * dx + mu * dnext
        dp = jnp.concatenate([dp[:, :a_off], dp[:, a_off:a_off + LORA_PAD] + da, dp[:, a_off + LORA_PAD:]], axis=1)
        dmu = jnp.sum(dx * (_shifted_prev(pw, prev8) - pw), axis=0, keepdims=True)
        return dp, dmu

    return rowwise("token_shift_bwd", fn, [_cols(p, win, 0), _cols(dxs), _cols(da_gla)], [mu_win],
                   [(win, BF16)], [(1, win)], tr,
                   extra_specs=[(p, _edge_spec(win, 0, tr, S, False)), (dxs, _edge_spec(win, 0, tr, S, True))])


def _adamw_math(w, g, m, v):
    m = ADAM_B1 * m + (1.0 - ADAM_B1) * g
    v = ADAM_B2 * v + (1.0 - ADAM_B2) * (g * g)
    m_hat = m / (1.0 - ADAM_B1 ** ADAM_STEP)
    v_hat = v / (1.0 - ADAM_B2 ** ADAM_STEP)
    delta = -ADAM_LR * (m_hat / (jnp.sqrt(v_hat) + ADAM_EPS) + ADAM_WD * w)
    return delta, m, v


def adamw_small(ws, gs, ms, vs):
    n = len(ws)

    def body(*refs):
        for i in range(n):
            d, m, v = _adamw_math(refs[i][...], refs[n + i][...], refs[2 * n + i][...], refs[3 * n + i][...])
            refs[4 * n + i][...] = d
            refs[5 * n + i][...] = m
            refs[6 * n + i][...] = v

    shapes = [jax.ShapeDtypeStruct(w.shape, F32) for w in ws]
    outs = pl.pallas_call(body, name="adamw_small", out_shape=shapes * 3, compiler_params=_cparams())(*ws, *gs, *ms, *vs)
    return outs[:n], outs[n:2 * n], outs[2 * n:]


def _place():
    x, y, c = lax.axis_index("x"), lax.axis_index("y"), lax.axis_index("c")
    chips = [(1 - x, y), (x, 1 - y), (1 - x, 1 - y)]
    return x, y, c, chips


def _full_shape(kind, r, c):
    return {"col": (r, 4 * c), "row": (4 * r, c), "slab": (4, r, c)}[kind]


def _slab(ref, kind, k, r, c, half=None):
    n, off = (r, 0) if half is None else (r // 2, half * (r // 2))
    if kind == "col":
        return ref.at[pl.ds(off, n), pl.ds(k * c, c)]
    if kind == "row":
        return ref.at[pl.ds(k * r + off, n), :]
    return ref.at[k, pl.ds(off, n), :]


def _remote(src, dst, sems, idx, to):
    return pltpu.make_async_remote_copy(src_ref=src, dst_ref=dst, send_sem=sems[0].at[idx], recv_sem=sems[1].at[idx],
                                        device_id=to, device_id_type=MESH_IDS)


class CommJob:
    def __init__(self, operands, out_shapes, scratch, start, finish):
        self.operands, self.out_shapes, self.scratch, self.start, self.finish = operands, out_shapes, scratch, start, finish


def gather_job(big, small=()):
    big, small = list(big), list(small)
    nb, ns = len(big), len(small)
    meta = [(kind, *a.shape) for a, kind in big + small]

    def sends(srcs, outs, sems):
        own_s, own_r, ici_s, ici_r, _, _, sm_s, sm_r = sems
        x, y, c, chips = _place()
        me, sib = 2 * x + y, (x, y, 1 - c)
        cps = []
        for a in range(nb):
            kind, r, cc = meta[a]
            for j, chip in enumerate(chips):
                cps.append(_remote(srcs[a].at[pl.ds(c * (r // 2), r // 2)], _slab(outs[a], kind, me, r, cc, c),
                                   (ici_s, ici_r), (a, j), (*chip, c)))
        for a in range(nb):
            kind, r, cc = meta[a]
            cps.append(_remote(srcs[a], _slab(outs[a], kind, me, r, cc), (own_s, own_r), (a,), sib))
        for s in range(ns):
            kind, r, cc = meta[nb + s]
            for t, to in enumerate([sib] + [(*chip, c) for chip in chips]):
                cps.append(_remote(srcs[nb + s], _slab(outs[nb + s], kind, me, r, cc), (sm_s, sm_r), (s, t), to))
        return cps

    def start(srcs, outs, sems):
        for cp in sends(srcs, outs, sems):
            cp.start()

    def finish(srcs, outs, sems):
        own_s, own_r, ici_s, ici_r, fwd_s, fwd_r, sm_s, sm_r = sems
        x, y, c, chips = _place()
        me, sib = 2 * x + y, (x, y, 1 - c)
        cids = [2 * chip[0] + chip[1] for chip in chips]
        hands = []
        for a in range(nb):
            kind, r, cc = meta[a]
            for j in range(3):
                blk = _slab(outs[a], kind, cids[j], r, cc, c)
                _remote(blk, blk, (ici_s, ici_r), (a, j), sib).wait_recv()
                hands.append(_remote(blk, blk, (fwd_s, fwd_r), (a, j), sib))
                hands[-1].start()
        for a in range(nb):
            kind, r, cc = meta[a]
            for j in range(3):
                blk = _slab(outs[a], kind, cids[j], r, cc, 1 - c)
                _remote(blk, blk, (fwd_s, fwd_r), (a, j), sib).wait_recv()
            blk = _slab(outs[a], kind, me, r, cc)
            _remote(blk, blk, (own_s, own_r), (a,), sib).wait_recv()
        for s in range(ns):
            kind, r, cc = meta[nb + s]
            for t, frm in enumerate([me] + cids):
                blk = _slab(outs[nb + s], kind, frm, r, cc)
                _remote(blk, blk, (sm_s, sm_r), (s, t), sib).wait_recv()
        for cp in sends(srcs, outs, sems) + hands:
            cp.wait_send()

    dma = pltpu.SemaphoreType.DMA
    nb1, ns1 = max(nb, 1), max(ns, 1)
    return CommJob([a for a, _ in big + small],
                   [jax.ShapeDtypeStruct(_full_shape(kind, r, cc), BF16) for (kind, r, cc) in meta],
                   [dma((nb1,)), dma((nb1,)), dma((nb1, 3)), dma((nb1, 3)), dma((nb1, 3)), dma((nb1, 3)),
                    dma((ns1, 4)), dma((ns1, 4))], start, finish)


def run_job(name, job):
    n_in, n_out = len(job.operands), len(job.out_shapes)

    def body(*refs):
        ins, outs, sems = refs[:n_in], refs[n_in:n_in + n_out], refs[n_in + n_out:]
        job.start(ins, outs, sems)
        job.finish(ins, outs, sems)

    return pl.pallas_call(body, name=name, in_specs=[HBM_SPEC] * n_in, out_specs=[HBM_SPEC] * n_out,
                          out_shape=job.out_shapes, scratch_shapes=job.scratch)(*job.operands)


def sibling_swap(name, arrays):
    n = len(arrays)

    def body(*refs):
        srcs, outs, sems = refs[:n], refs[n:2 * n], refs[2 * n:]
        x, y, c, _ = _place()
        cps = [_remote(srcs[a], outs[a], sems, (a,), (x, y, 1 - c)) for a in range(n)]
        for cp in cps:
            cp.start()
        for cp in cps:
            cp.wait_recv()
        for cp in cps:
            cp.wait_send()

    return pl.pallas_call(
        body, name=name, in_specs=[HBM_SPEC] * n, out_specs=[HBM_SPEC] * n,
        out_shape=[jax.ShapeDtypeStruct(a.shape, a.dtype) for a in arrays],
        scratch_shapes=[pltpu.SemaphoreType.DMA((n,)), pltpu.SemaphoreType.DMA((n,))],
    )(*arrays)


def exchange_job(sums):
    n = len(sums)

    def sends(srcs, outs, sems):
        ici_s, ici_r, sib_s, sib_r = sems
        x, y, c, chips = _place()
        me, sib = 2 * x + y, (x, y, 1 - c)
        cps = []
        for a, (_, kind, r, cc) in enumerate(sums):
            for j, chip in enumerate(chips):
                cid = 2 * chip[0] + chip[1]
                cps.append(_remote(_slab(srcs[a], kind, cid, r, cc, c), outs[a].at[me], (ici_s, ici_r), (a, j), (*chip, c)))
            cps.append(_remote(_slab(srcs[a], kind, me, r, cc, 1 - c), outs[a].at[me], (sib_s, sib_r), (a,), sib))
        return cps

    def start(srcs, outs, sems):
        for cp in sends(srcs, outs, sems):
            cp.start()

    def finish(srcs, outs, sems):
        ici_s, ici_r, sib_s, sib_r = sems
        x, y, c, chips = _place()
        me, sib = 2 * x + y, (x, y, 1 - c)
        for a in range(n):
            for j, chip in enumerate(chips):
                blk = outs[a].at[2 * chip[0] + chip[1]]
                _remote(blk, blk, (ici_s, ici_r), (a, j), sib).wait_recv()
            _remote(outs[a].at[me], outs[a].at[me], (sib_s, sib_r), (a,), sib).wait_recv()
        for cp in sends(srcs, outs, sems):
            cp.wait_send()

    dma = pltpu.SemaphoreType.DMA
    return CommJob([s[0] for s in sums], [jax.ShapeDtypeStruct((4, r // 2, cc), BF16) for (_, _, r, cc) in sums],
                   [dma((n, 3)), dma((n, 3)), dma((n,)), dma((n,))], start, finish)


def allreduce_small(vec):
    R, C = vec.shape

    def body(src, out, gathered, send_sems, recv_sems):
        x, y, c, _ = _place()
        me = 4 * x + 2 * y + c
        gathered[me] = src[...]
        peers = [(fx, fy, fc) for fx in (0, 1) for fy in (0, 1) for fc in (0, 1) if (fx, fy, fc) != (0, 0, 0)]
        sends = []
        for j, (fx, fy, fc) in enumerate(peers):
            to = (x ^ fx, y ^ fy, c ^ fc)
            cp = pltpu.make_async_remote_copy(
                src_ref=src, dst_ref=gathered.at[me], send_sem=send_sems.at[j], recv_sem=recv_sems.at[j],
                device_id=to, device_id_type=MESH_IDS)
            cp.start()
            sends.append(cp)
        for j, (fx, fy, fc) in enumerate(peers):
            frm = 4 * (x ^ fx) + 2 * (y ^ fy) + (c ^ fc)
            pltpu.make_async_remote_copy(
                src_ref=src, dst_ref=gathered.at[frm], send_sem=send_sems.at[j], recv_sem=recv_sems.at[j],
                device_id=(x, y, c), device_id_type=MESH_IDS).wait_recv()
        for cp in sends:
            cp.wait_send()
        acc = gathered[0]
        for k in range(1, 8):
            acc = acc + gathered[k]
        out[...] = acc

    vm = pl.BlockSpec(memory_space=pltpu.VMEM)
    return pl.pallas_call(
        body, name="allreduce_small", in_specs=[vm], out_specs=vm,
        out_shape=jax.ShapeDtypeStruct((R, C), F32),
        scratch_shapes=[pltpu.VMEM((8, R, C), F32), pltpu.SemaphoreType.DMA((7,)), pltpu.SemaphoreType.DMA((7,))],
        compiler_params=_cparams(),
    )(vec)


def pair_sum(name, mine, theirs):
    rows, cols = mine.shape
    tr = _pick(rows, max(16, (1 << 20) // cols), 16)
    return rowwise(name, lambda a, b: a.astype(F32) + b.astype(F32), [_cols(mine), _cols(theirs)], [], [(cols, BF16)], [], tr)[0]


def chip_sum(name, rb):
    _, rh, C = rb.shape
    tr = _pick(rh, max(16, (1 << 19) // C), 16)

    def body(r_ref, o_ref):
        acc = r_ref[0].astype(F32)
        for k in range(1, 4):
            acc = acc + r_ref[k].astype(F32)
        o_ref[...] = acc

    return pl.pallas_call(body, name=name, grid=(rh // tr,),
                          in_specs=[pl.BlockSpec((4, tr, C), lambda i: (0, i, 0))],
                          out_specs=pl.BlockSpec((tr, C), lambda i: (i, 0)),
                          out_shape=jax.ShapeDtypeStruct((rh, C), F32),
                          compiler_params=_cparams(("parallel",)))(rb)


def adamw_halves(name, w, mine, theirs, m, v):
    _, rows, cols = w.shape
    tr = _pick(rows // 2, max(8, (1 << 19) // cols), 8)
    nbh = rows // 2 // tr
    full = pl.BlockSpec((1, tr, cols), lambda i: (0, i, 0))
    half = pl.BlockSpec((tr, cols), lambda i: (i % nbh, 0))

    def body(w_ref, a_ref, b_ref, m_ref, v_ref, g_out, d_out, m_out, v_out):
        is_mine = (pl.program_id(0) // nbh) == lax.axis_index("c")
        g = jnp.where(is_mine, a_ref[...], b_ref[...])
        d, mn, vn = _adamw_math(w_ref[0], g, m_ref[0], v_ref[0])
        g_out[0] = g
        d_out[0] = d
        m_out[0] = mn
        v_out[0] = vn

    return pl.pallas_call(body, name="adamw_" + name, grid=(rows // tr,), in_specs=[full, half, half, full, full],
                          out_specs=[full] * 4, out_shape=[jax.ShapeDtypeStruct((1, rows, cols), F32)] * 4,
                          compiler_params=_cparams(("parallel",)))(w, mine, theirs, m, v)


BIG = {"ffn1_wg": "col", "ffn1_wu": "col", "ffn1_wd": "row", "w_in": "slab", "w_branch": "row", "w_out": "row",
       "ffn2_wg": "col", "ffn2_wu": "col", "ffn2_wd": "row"}
LORA = ["gla_w_a2", "rwkv_w_w2", "rwkv_w_a2", "rwkv_w_g2"]
REPLICATED = ["ffn1_norm", "mix_norm", "gla_b_a", "gla_gn_w", "rwkv_mu", "rwkv_w0", "rwkv_a0", "rwkv_k_k", "rwkv_k_a",
              "rwkv_r_k", "rwkv_lnx_w", "rwkv_lnx_b", "gate_b", "ffn2_norm", "final_norm"]
WEIGHTS = ["ffn1_norm", "ffn1_wg", "ffn1_wu", "ffn1_wd", "mix_norm", "w_in", "gla_w_a2", "gla_b_a", "gla_gn_w", "rwkv_mu",
           "rwkv_w0", "rwkv_w_w2", "rwkv_a0", "rwkv_w_a2", "rwkv_w_g2", "rwkv_k_k", "rwkv_k_a", "rwkv_r_k", "rwkv_lnx_w",
           "rwkv_lnx_b", "gate_b", "w_branch", "w_out", "ffn2_norm", "ffn2_wg", "ffn2_wu", "ffn2_wd", "final_norm"]


def kernel(x, ffn1_norm, ffn1_wg, ffn1_wu, ffn1_wd, mix_norm, w_in, gla_w_a2, gla_b_a, gla_gn_w, rwkv_mu, rwkv_w0, rwkv_w_w2, rwkv_a0, rwkv_w_a2, rwkv_w_g2, rwkv_k_k, rwkv_k_a, rwkv_r_k, rwkv_lnx_w, rwkv_lnx_b, gate_b, w_branch, w_out, ffn2_norm, ffn2_wg, ffn2_wu, ffn2_wd, final_norm, loss_target, m_ffn1_norm, m_ffn1_wg, m_ffn1_wu, m_ffn1_wd, m_mix_norm, m_w_in, m_gla_w_a2, m_gla_b_a, m_gla_gn_w, m_rwkv_mu, m_rwkv_w0, m_rwkv_w_w2, m_rwkv_a0, m_rwkv_w_a2, m_rwkv_w_g2, m_rwkv_k_k, m_rwkv_k_a, m_rwkv_r_k, m_rwkv_lnx_w, m_rwkv_lnx_b, m_gate_b, m_w_branch, m_w_out, m_ffn2_norm, m_ffn2_wg, m_ffn2_wu, m_ffn2_wd, m_final_norm, v_ffn1_norm, v_ffn1_wg, v_ffn1_wu, v_ffn1_wd, v_mix_norm, v_w_in, v_gla_w_a2, v_gla_b_a, v_gla_gn_w, v_rwkv_mu, v_rwkv_w0, v_rwkv_w_w2, v_rwkv_a0, v_rwkv_w_a2, v_rwkv_w_g2, v_rwkv_k_k, v_rwkv_k_a, v_rwkv_r_k, v_rwkv_lnx_w, v_rwkv_lnx_b, v_gate_b, v_w_branch, v_w_out, v_ffn2_norm, v_ffn2_wg, v_ffn2_wu, v_ffn2_wd, v_final_norm):
    args = dict(locals())
    wts = {n: args[n] for n in WEIGHTS}
    moms = {n: args["m_" + n] for n in WEIGHTS}
    vars_ = {n: args["v_" + n] for n in WEIGHTS}

    xs = x[0]
    tgt = loss_target[0]
    S, D = xs.shape
    FF = ffn1_wd.shape[1] * 4
    gheads = gla_b_a.shape[-1] // GLA_DK
    GQ, GV = gheads * GLA_DK, gheads * GLA_DV
    rheads = rwkv_r_k.shape[1]
    RW = rheads * RW_HD
    NT = RW // 128
    lo_g = gla_w_a2.shape[1]
    lo_w = rwkv_w_w2.shape[1]
    lo_a = rwkv_w_a2.shape[1]
    assert rwkv_w_g2.shape[1] == GATE_LORA and RW % 128 == 0

    ow = rw_window(RW)
    WIN = -(-ow["used"] // (2 * D)) * (2 * D)
    lay = dict(gate=WIN, v=WIN + 2 * D, r=WIN + 2 * D + GV, q=WIN + 2 * D + 2 * GV, k=WIN + 2 * D + 2 * GV + GQ, a=ow["a"])
    DP = lay["k"] + GQ
    DIN = w_in.shape[-1] * 4
    o_sizes = [GQ, GQ, GV, GV, lo_g, RW, RW, RW, lo_w, lo_a, GATE_LORA, 2 * D]
    o_offs = [sum(o_sizes[:i]) for i in range(len(o_sizes))]
    assert o_offs[-1] + o_sizes[-1] == DIN
    p_offs = [lay["q"], lay["k"], lay["v"], lay["r"], ow["a"], ow["rr"], ow["rk"], ow["rv"], ow["wd"], ow["ad"], ow["gd"], lay["gate"]]

    def to_padded(w):
        order = sorted(range(len(o_sizes)), key=lambda i: p_offs[i])
        parts, pos = [], 0
        for i in order:
            if p_offs[i] > pos:
                parts.append(jnp.zeros((w.shape[0], p_offs[i] - pos), w.dtype))
            parts.append(w[:, o_offs[i]:o_offs[i] + o_sizes[i]])
            pos = p_offs[i] + o_sizes[i]
        if pos < DP:
            parts.append(jnp.zeros((w.shape[0], DP - pos), w.dtype))
        return jnp.concatenate(parts, axis=1)

    def from_padded(w):
        return jnp.concatenate([w[:, p_offs[i]:p_offs[i] + o_sizes[i]] for i in range(len(o_sizes))], axis=1)

    def pad_rows(w, rows):
        return jnp.pad(w, ((0, rows - w.shape[0]), (0, 0)))

    mu = rwkv_mu[0]
    mu_parts = {"rr": mu[0:RW], "rk": mu[RW:2 * RW], "rv": mu[2 * RW:3 * RW], "wd": mu[3 * RW:3 * RW + lo_w],
                "ad": mu[3 * RW + lo_w:3 * RW + lo_w + lo_a], "gd": mu[3 * RW + lo_w + lo_a:]}
    mu_win = jnp.zeros((WIN,), F32)
    for key, val in mu_parts.items():
        mu_win = lax.dynamic_update_slice(mu_win, val, (ow[key],))
    mu_win = mu_win.reshape(1, WIN)

    shard_shapes = {n: wts[n].shape[1:] for n in list(BIG) + LORA}
    W = {}

    def shard(n):
        return (wts[n][0].astype(BF16), BIG[n])

    def mm_gather(a, b, out_dtype, name, gather, lora=()):
        out, got = matmul(a, b, "nn", out_dtype, name,
                          job=gather_job([shard(n) for n in gather], [(wts[n][0].astype(BF16), "col") for n in lora]))
        W.update(zip(list(gather) + list(lora), got))
        return out

    W["ffn1_wg"] = run_job("gather_ffn1_wg", gather_job([shard("ffn1_wg")]))[0]
    r_k = rwkv_r_k.reshape(1, RW)
    fin_g = final_norm.reshape(1, D)

    TR = min(128, S)
    def swiglu_act(tag, gx, ux):
        return rowwise(tag + "_act", lambda a, b: f_swiglu(*f32(a, b)), [_cols(gx), _cols(ux)], [], [(FF, BF16)], [], TR)[0]

    h1 = rowwise("rms1", lambda a, g: f_rms(a, g), [_cols(xs)], [ffn1_norm], [(D, BF16)], [], TR)[0]
    g1 = mm_gather(h1, W["ffn1_wg"], BF16, "ffn1_g", ["ffn1_wu"])
    u1 = mm_gather(h1, W["ffn1_wu"], BF16, "ffn1_u", ["ffn1_wd"])
    act1 = swiglu_act("ffn1", g1, u1)
    f1 = mm_gather(act1, W["ffn1_wd"], F32, "ffn1_d", ["w_in"], LORA)
    w_in_p = to_padded(W["w_in"].transpose(1, 0, 2).reshape(D, DIN))
    gla_a2_p = pad_rows(W["gla_w_a2"], LORA_PAD)
    w_w2_p = pad_rows(W["rwkv_w_w2"], LORA_PAD)
    w_a2_p = pad_rows(W["rwkv_w_a2"], LORA_PAD)
    w_g2 = W["rwkv_w_g2"]

    def res_rms(coef):
        def fn(a, f, g):
            x1 = a + coef * f
            return x1, f_rms(x1, g)
        return fn

    x1, h2 = rowwise("res_rms_mix", res_rms(0.5), [_cols(xs), _cols(f1)], [mix_norm], [(D, F32), (D, BF16)], [], TR)
    p = mm_gather(h2, w_in_p, F32, "w_in", ["w_branch", "w_out", "ffn2_wg"])
    wb_g, wb_r = W["w_branch"][:GV], W["w_branch"][GV:]

    o_gla, gla_states = gla_forward(p, lay, gla_a2_p, gla_b_a, gla_gn_w, gheads)
    xsh = token_shift_forward(p, mu_win, WIN, TR)
    pre_consts = [rwkv_w0, w_w2_p, rwkv_a0, w_a2_p, w_g2, rwkv_k_k, rwkv_k_a]
    pre_fn = functools.partial(f_rw_pre, rw=RW)

    def pre_f32(xw, w0, ww, a0, wa, wg_, kk_, ka_):
        return pre_fn(xw, w0, ww.astype(F32), a0, wa.astype(F32), wg_.astype(F32), kk_, ka_)

    r_, dec_, k2_, v_, kk_, b_, g_ = rowwise("rw_pre", pre_f32, [_cols(xsh)], pre_consts, [(RW, F32)] * 7, [], 128)

    def tiles(a):
        return a.reshape(S, NT, 128)

    y_t, rw_states = rwkv_scan_forward(*(tiles(a) for a in (r_, dec_, k2_, v_, kk_, b_)))
    y_ = y_t.reshape(S, RW)
    post_consts = [rwkv_lnx_w, rwkv_lnx_b, r_k]
    o_rw = rowwise("rw_post", f_rw_post, [_cols(a) for a in (y_, r_, k2_, v_, g_)], post_consts, [(RW, BF16)], [], TR)[0]

    yg = matmul(o_gla, wb_g, "nn", F32, "branch_gla")
    yr = matmul(o_rw, wb_r, "nn", F32, "branch_rw")
    merge_fn = functools.partial(f_merge, d=D)
    merged = rowwise("merge", merge_fn, [_cols(p, 2 * D, lay["gate"]), _cols(yg), _cols(yr)], [gate_b], [(D, BF16)], [], TR)[0]
    mix = matmul(merged, W["w_out"], "nn", F32, "w_out")
    x2, h3 = rowwise("res_rms_ffn2", res_rms(1.0), [_cols(x1), _cols(mix)], [ffn2_norm], [(D, F32), (D, BF16)], [], TR)
    g3 = mm_gather(h3, W["ffn2_wg"], BF16, "ffn2_g", ["ffn2_wu"])
    u3 = mm_gather(h3, W["ffn2_wu"], BF16, "ffn2_u", ["ffn2_wd"])
    act3 = swiglu_act("ffn2", g3, u3)
    f3 = matmul(act3, W["ffn2_wd"], "nn", F32, "ffn2_d")

    def final_fn(a, f, t, g):
        def loss_of(a, f, g):
            yv = f_rms(a + 0.5 * f, g)
            return 0.5 * jnp.sum(jnp.mean(jnp.square(yv - t), axis=-1))
        val, vjp = jax.vjp(loss_of, a, f, g)
        da, df, dg = vjp(jnp.ones((), F32))
        return da, df, jnp.full((1, 128), val, F32), dg

    dx2, df3, loss_acc, d_final = rowwise("final_loss", final_fn, [_cols(x2), _cols(f3), _cols(tgt)], [fin_g],
                                          [(D, F32), (D, BF16)], [(1, 128), (1, D)], TR)
    grads = {"final_norm": d_final.reshape(D)}

    received = {}

    def pair_up(n, dw):
        r, cc = shard_shapes[n]
        flat = (4 * r, cc) if BIG[n] == "slab" else dw.shape
        theirs = sibling_swap("swap_" + n, [dw])[0]
        return n, (pair_sum("pair_" + n, dw.reshape(flat), theirs.reshape(flat)).reshape(dw.shape), BIG[n], r, cc)

    def mm_exchange(a, b, mode, out_dtype, name, pending):
        out, got = matmul(a, b, mode, out_dtype, name, job=exchange_job([entry for _, entry in pending]))
        received.update(zip([n for n, _ in pending], got))
        return out

    def ffn_backward(tag, h, gx, ux, act, df, wg, wu, wd):
        dact = matmul(df, wd, "nt", F32, tag + "_dact")
        p_wd = pair_up(tag + "_wd", matmul(act, df, "tn", BF16, tag + "_dwd"))

        def fn(a, b, d):
            _, vjp = jax.vjp(f_swiglu, *f32(a, b))
            return vjp(d)

        dgx, dux = rowwise(tag + "_dact_bwd", fn, [_cols(gx), _cols(ux), _cols(dact)], [], [(FF, BF16)] * 2, [], TR)
        p_wg = pair_up(tag + "_wg", mm_exchange(h, dgx, "tn", BF16, tag + "_dwg", [p_wd]))
        p_wu = pair_up(tag + "_wu", mm_exchange(h, dux, "tn", BF16, tag + "_dwu", [p_wg]))
        dha = mm_exchange(dgx, wg, "nt", F32, tag + "_dh_g", [p_wu])
        dhb = matmul(dux, wu, "nt", F32, tag + "_dh_u")
        return dha, dhb

    def res_rms_bwd(name, coef, a, f, g, dx1, dha, dhb):
        def fn(a, f, dx1, dha, dhb, g):
            _, vjp = jax.vjp(res_rms(coef), a, f, g)
            return vjp((dx1, dha + dhb))

        return rowwise(name, fn, [_cols(a), _cols(f), _cols(dx1), _cols(dha), _cols(dhb)], [g],
                       [(D, F32), (D, BF16)], [(1, D)], TR)

    dh3a, dh3b = ffn_backward("ffn2", h3, g3, u3, act3, df3, W["ffn2_wg"], W["ffn2_wu"], W["ffn2_wd"])
    dx1, dmix, grads["ffn2_norm"] = res_rms_bwd("res_rms_ffn2_bwd", 1.0, x1, mix, ffn2_norm, dx2, dh3a, dh3b)

    p_wo = pair_up("w_out", matmul(merged, dmix, "tn", BF16, "d_w_out"))
    dmerged = mm_exchange(dmix, W["w_out"], "nt", F32, "d_merged", [p_wo])

    def merge_bwd(gp, a, b, d, gb):
        _, vjp = jax.vjp(merge_fn, gp, a, b, gb)
        return vjp(d)

    dgate, dyg, dyr, grads["gate_b"] = rowwise(
        "merge_bwd", merge_bwd, [_cols(p, 2 * D, lay["gate"]), _cols(yg), _cols(yr), _cols(dmerged)], [gate_b],
        [(2 * D, BF16), (D, BF16), (D, BF16)], [(1, 2 * D)], TR)
    do_gla = matmul(dyg, wb_g, "nt", BF16, "d_o_gla")
    do_rw = matmul(dyr, wb_r, "nt", F32, "d_o_rw")
    p_wb = pair_up("w_branch", jnp.concatenate([matmul(o_gla, dyg, "tn", BF16, "d_wb_gla"),
                                                matmul(o_rw, dyr, "tn", BF16, "d_wb_rw")], axis=0))

    def post_bwd(yv, rv, kv, vv, gv_, d, lw, lb, rk):
        _, vjp = jax.vjp(f_rw_post, yv, rv, kv, vv, gv_, lw, lb, rk)
        return vjp(d)

    dy_, dr_p, dk2_p, dv_p, dg_p, grads["rwkv_lnx_w"], grads["rwkv_lnx_b"], d_rk = rowwise(
        "rw_post_bwd", post_bwd, [_cols(a) for a in (y_, r_, k2_, v_, g_, do_rw)], post_consts,
        [(RW, F32)] * 5, [(1, RW)] * 3, 128)
    grads["rwkv_r_k"] = d_rk.reshape(rwkv_r_k.shape[1:])

    scan_cots = rwkv_scan_backward(*(tiles(a) for a in (r_, dec_, k2_, v_, kk_, b_)), rw_states, tiles(dy_))
    dr_s, dw_s, dk2_s, dv_s, dkk_s, db_s = (a.reshape(S, RW) for a in scan_cots)

    def pre_bwd(xw, c0, c1, c2, c3, c4, c5, c6, c7, c8, c9, w0, ww, a0, wa, wg_, kk_c, ka_c):
        _, vjp = jax.vjp(pre_fn, xw, w0, ww.astype(F32), a0, wa.astype(F32), wg_.astype(F32), kk_c, ka_c)
        return vjp((c0 + c6, c1, c2 + c7, c3 + c8, c4, c5, c9))

    dxsh, grads["rwkv_w0"], d_ww2, grads["rwkv_a0"], d_wa2, d_wg2, grads["rwkv_k_k"], grads["rwkv_k_a"] = rowwise(
        "rw_pre_bwd", pre_bwd,
        [_cols(xsh)] + [_cols(a) for a in (dr_s, dw_s, dk2_s, dv_s, dkk_s, db_s, dr_p, dk2_p, dv_p, dg_p)], pre_consts,
        [(WIN, F32)], [(1, RW), w_w2_p.shape, (1, RW), w_a2_p.shape, w_g2.shape, (1, RW), (1, RW)], 128)
    grads["rwkv_w_w2"], grads["rwkv_w_a2"], grads["rwkv_w_g2"] = d_ww2[:lo_w], d_wa2[:lo_a], d_wg2

    dq, dk, dv, dr, da, d_ga2, grads["gla_b_a"], grads["gla_gn_w"] = gla_backward(
        p, lay, gla_states, do_gla, gla_a2_p, gla_b_a, gla_gn_w, gheads)
    grads["gla_w_a2"] = d_ga2[:lo_g]

    dpw, dmu_win = token_shift_backward(p, dxsh, da, mu_win, WIN, ow["a"], TR)
    dmu = dmu_win[0]
    grads["rwkv_mu"] = jnp.concatenate([dmu[ow[k_]:ow[k_] + mu_parts[k_].shape[0]] for k_ in ("rr", "rk", "rv", "wd", "ad", "gd")]).reshape(1, -1)

    dp = jnp.concatenate([dpw, dgate, dv, dr, dq, dk], axis=1)
    d_w_in = from_padded(mm_exchange(h2, dp, "tn", BF16, "d_w_in", [p_wb]))
    p_wi = pair_up("w_in", d_w_in.reshape(D, 4, shard_shapes["w_in"][1]).transpose(1, 0, 2))
    dh2 = mm_exchange(dp, w_in_p, "nt", F32, "d_h2", [p_wi])
    zeros_d = jnp.zeros_like(dh2)
    dx0, df1, grads["mix_norm"] = res_rms_bwd("res_rms_mix_bwd", 0.5, xs, f1, mix_norm, dx1, dh2, zeros_d)

    dh1a, dh1b = ffn_backward("ffn1", h1, g1, u1, act1, df1, W["ffn1_wg"], W["ffn1_wu"], W["ffn1_wd"])

    def rms1_bwd(a, dha, dhb, dxa, g):
        _, vjp = jax.vjp(f_rms, a, g)
        da_, dg_ = vjp(dha + dhb)
        return da_ + dxa, dg_

    grad_x, grads["ffn1_norm"] = rowwise("rms1_bwd", rms1_bwd, [_cols(xs), _cols(dh1a), _cols(dh1b), _cols(dx0)],
                                         [ffn1_norm], [(D, F32)], [(1, D)], TR)

    names = list(BIG)
    halves = [chip_sum("chip_sum_" + n, received[n]) for n in names]
    others = sibling_swap("sibling_join", halves)
    final_grads, delta, new_m, new_v = {}, {}, {}, {}
    for n, h, o in zip(names, halves, others):
        final_grads[n], delta[n], new_m[n], new_v[n] = adamw_halves(n, wts[n], h, o, moms[n], vars_[n])

    rep_flat = jnp.concatenate([grads[n].reshape(-1) for n in REPLICATED + LORA])
    rep_rows = -(-rep_flat.shape[0] // 1024) * 8
    rep_sum = allreduce_small(jnp.pad(rep_flat, (0, rep_rows * 128 - rep_flat.shape[0])).reshape(rep_rows, 128)).reshape(-1)
    my_chip = 2 * lax.axis_index("x") + lax.axis_index("y")
    off = 0
    for n in REPLICATED + LORA:
        size = grads[n].size
        full = rep_sum[off:off + size].reshape(grads[n].shape)
        off += size
        if n in LORA:
            cc = shard_shapes[n][1]
            full = lax.dynamic_slice_in_dim(full, my_chip * cc, cc, axis=1)
        final_grads[n] = full.reshape(wts[n].shape)

    loss = lax.psum(loss_acc[0, 0], ("x", "y", "c"))

    small = REPLICATED + LORA

    def two(a):
        return a.reshape(-1, a.shape[-1])

    ds, ms_, vs_ = adamw_small([two(wts[n]) for n in small], [two(final_grads[n]) for n in small],
                               [two(moms[n]) for n in small], [two(vars_[n]) for n in small])
    for i, n in enumerate(small):
        shp = wts[n].shape
        delta[n], new_m[n], new_v[n] = ds[i].reshape(shp), ms_[i].reshape(shp), vs_[i].reshape(shp)

    return (loss, grad_x.reshape(x.shape), *[final_grads[n] for n in WEIGHTS], *[delta[n] for n in WEIGHTS],
            *[new_m[n] for n in WEIGHTS], *[new_v[n] for n in WEIGHTS])
```

```python
import functools
import math

import jax
import jax.numpy as jnp
from jax import lax
from jax.experimental import pallas as pl
from jax.experimental.pallas import tpu as pltpu

F32 = jnp.float32
BF16 = jnp.bfloat16
MESH_IDS = pl.DeviceIdType.MESH

NORM_EPS = 1e-6
GN_EPS = 64e-5
GLA_TAU = 16.0
CHUNK = 64
GLA_DK = 128
GLA_DV = 256
RW_HD = 64
LORA_PAD = 128
GATE_LORA = 256
ADAM_LR, ADAM_B1, ADAM_B2, ADAM_EPS, ADAM_WD, ADAM_STEP = 0.001, 0.9, 0.999, 1e-08, 0.01, 10

VMEM_LIMIT_BYTES = 56 * 1024 * 1024
HBM_SPEC = pl.BlockSpec(memory_space=pltpu.HBM)


def _cparams(sem=None):
    return pltpu.CompilerParams(dimension_semantics=sem, vmem_limit_bytes=VMEM_LIMIT_BYTES)


def _pick(n, target, mult=128):
    best = None
    for t in range(mult, min(n, target) + 1, mult):
        if n % t == 0:
            best = t
    return best if best is not None else n


_NN = (((1,), (0,)), ((), ()))
_NT = (((1,), (1,)), ((), ()))
_TN = (((0,), (0,)), ((), ()))


def _dg(a, b, dims):
    return lax.dot_general(a.astype(BF16), b.astype(BF16), dims, preferred_element_type=F32)


@jax.custom_vjp
def mm_nn(a, b):
    return _dg(a, b, _NN)


def _mm_nn_fwd(a, b):
    return _dg(a, b, _NN), (a, b)


def _mm_nn_bwd(res, g):
    a, b = res
    return _dg(g, b, _NT).astype(a.dtype), _dg(a, g, _TN).astype(b.dtype)


mm_nn.defvjp(_mm_nn_fwd, _mm_nn_bwd)


@jax.custom_vjp
def mm_nt(a, b):
    return _dg(a, b, _NT)


def _mm_nt_fwd(a, b):
    return _dg(a, b, _NT), (a, b)


def _mm_nt_bwd(res, g):
    a, b = res
    return _dg(g, b, _NN).astype(a.dtype), _dg(g, a, _TN).astype(b.dtype)


mm_nt.defvjp(_mm_nt_fwd, _mm_nt_bwd)


@jax.custom_vjp
def mm_tn(a, b):
    return _dg(a, b, _TN)


def _mm_tn_fwd(a, b):
    return _dg(a, b, _TN), (a, b)


def _mm_tn_bwd(res, g):
    a, b = res
    return _dg(b, g, _NT).astype(a.dtype), _dg(a, g, _NN).astype(b.dtype)


mm_tn.defvjp(_mm_tn_fwd, _mm_tn_bwd)


def _split3(x):
    h = x.astype(BF16)
    r = x - h.astype(F32)
    m = r.astype(BF16)
    l = (r - m.astype(F32)).astype(BF16)
    return h, m, l


def _block_ones(n, seg):
    i = lax.broadcasted_iota(jnp.int32, (n, n), 0) // seg
    j = lax.broadcasted_iota(jnp.int32, (n, n), 1) // seg
    return (i == j).astype(BF16)


def _segsum_raw(x, seg, terms):
    ones = _block_ones(128, seg)
    outs = []
    for j in range(x.shape[1] // 128):
        t = x[:, j * 128:(j + 1) * 128]
        parts = _split3(t)[:terms]
        acc = jnp.dot(parts[0], ones, preferred_element_type=F32)
        for p_ in parts[1:]:
            acc = acc + jnp.dot(p_, ones, preferred_element_type=F32)
        outs.append(acc)
    return outs[0] if len(outs) == 1 else jnp.concatenate(outs, axis=1)


@jax.custom_vjp
def segsum64(x):
    return _segsum_raw(x, RW_HD, 3)


segsum64.defvjp(lambda x: (_segsum_raw(x, RW_HD, 3), None), lambda _, g: (_segsum_raw(g, RW_HD, 3),))


def _tri(n, upper):
    i = lax.broadcasted_iota(jnp.int32, (n, n), 0)
    j = lax.broadcasted_iota(jnp.int32, (n, n), 1)
    return ((i <= j) if upper else (i >= j)).astype(BF16)


def _tri_mm(x, upper):
    t = _tri(x.shape[0], upper)
    h, m, l = _split3(x)
    return (jnp.dot(t, h, preferred_element_type=F32) + jnp.dot(t, m, preferred_element_type=F32)
            + jnp.dot(t, l, preferred_element_type=F32))


@jax.custom_vjp
def cumsum_rows(x):
    return _tri_mm(x, False)


cumsum_rows.defvjp(lambda x: (_tri_mm(x, False), None), lambda _, g: (_tri_mm(g, True),))


def _make_split(sizes, axis):
    offs = [sum(sizes[:i]) for i in range(len(sizes))]

    def cut(x):
        if axis == 1:
            return tuple(x[:, o:o + s] for o, s in zip(offs, sizes))
        return tuple(x[o:o + s, :] for o, s in zip(offs, sizes))

    @jax.custom_vjp
    def split(x):
        return cut(x)

    split.defvjp(lambda x: (cut(x), None), lambda _, gs: (jnp.concatenate(gs, axis=axis),))
    return split


@jax.custom_vjp
def log_sigmoid(z):
    return jnp.minimum(z, 0.0) - jnp.log(1.0 + jnp.exp(-jnp.abs(z)))


log_sigmoid.defvjp(lambda z: (log_sigmoid(z), z), lambda z, g: (g * (1.0 - jax.nn.sigmoid(z)),))


def silu(x):
    return x * jax.nn.sigmoid(x)


def matmul(a, b, mode, out_dtype, name, tm=1024, tn=512, tk=2048, job=None):
    if mode == "nn":
        (M, K), (K2, N) = a.shape, b.shape
    elif mode == "nt":
        (M, K), (N, K2) = a.shape, b.shape
    else:
        (K, M), (K2, N) = a.shape, b.shape
    assert K == K2, (name, a.shape, b.shape)
    tm, tn, tk = _pick(M, tm), _pick(N, tn), _pick(K, tk)
    grid = (M // tm, N // tn, K // tk)
    dims = {"nn": _NN, "nt": _NT, "tn": _TN}[mode]
    a_spec = pl.BlockSpec((tk, tm), lambda i, j, k: (k, i)) if mode == "tn" else pl.BlockSpec((tm, tk), lambda i, j, k: (i, k))
    b_spec = pl.BlockSpec((tn, tk), lambda i, j, k: (j, k)) if mode == "nt" else pl.BlockSpec((tk, tn), lambda i, j, k: (k, j))
    n_in = 0 if job is None else len(job.operands)
    n_out = 0 if job is None else len(job.out_shapes)

    def body(a_ref, b_ref, *rest):
        job_ins, o_ref, job_outs = rest[:n_in], rest[n_in], rest[n_in + 1:n_in + 1 + n_out]
        acc_ref, sems = rest[n_in + 1 + n_out], rest[n_in + 2 + n_out:]
        i, j, k = pl.program_id(0), pl.program_id(1), pl.program_id(2)
        if job is not None:
            @pl.when((i == 0) & (j == 0) & (k == 0))
            def _():
                job.start(job_ins, job_outs, sems)

        part = _dg(a_ref[...], b_ref[...], dims)

        @pl.when(k == 0)
        def _():
            acc_ref[...] = part

        @pl.when(k > 0)
        def _():
            acc_ref[...] += part

        @pl.when(k == grid[2] - 1)
        def _():
            o_ref[...] = acc_ref[...].astype(o_ref.dtype)

        if job is not None:
            @pl.when((i == grid[0] - 1) & (j == grid[1] - 1) & (k == grid[2] - 1))
            def _():
                job.finish(job_ins, job_outs, sems)

    main_out = jax.ShapeDtypeStruct((M, N), out_dtype)
    main_spec = pl.BlockSpec((tm, tn), lambda i, j, k: (i, j))
    if job is None:
        return pl.pallas_call(
            body, name=name, grid=grid, in_specs=[a_spec, b_spec], out_specs=main_spec, out_shape=main_out,
            scratch_shapes=[pltpu.VMEM((tm, tn), F32)],
            compiler_params=_cparams(("parallel", "parallel", "arbitrary")),
        )(a, b)
    res = pl.pallas_call(
        body, name=name, grid=grid, in_specs=[a_spec, b_spec] + [HBM_SPEC] * n_in,
        out_specs=[main_spec] + [HBM_SPEC] * n_out, out_shape=[main_out] + list(job.out_shapes),
        scratch_shapes=[pltpu.VMEM((tm, tn), F32)] + list(job.scratch),
        compiler_params=_cparams(("arbitrary", "arbitrary", "arbitrary")),
    )(a, b, *job.operands)
    return res[0], res[1:]


def _cols(arr, width=None, off=0):
    width = arr.shape[1] if width is None else width
    assert off % width == 0, (off, width)
    return (arr, width, off // width)


def rowwise(name, fn, rows, consts, row_outs, acc_outs, tr, extra_specs=()):
    S = rows[0][0].shape[0]
    tr = min(tr, S)
    assert S % tr == 0
    n_in = len(rows) + len(extra_specs) + len(consts)
    n_ro = len(row_outs)
    in_specs = [pl.BlockSpec((tr, w), functools.partial(lambda i, cb: (i, cb), cb=cb)) for (_, w, cb) in rows]
    in_specs += [spec for (_, spec) in extra_specs]
    in_specs += [pl.BlockSpec(c.shape, lambda i: (0, 0)) for c in consts]
    out_shape = [jax.ShapeDtypeStruct((S, w), dt) for (w, dt) in row_outs]
    out_shape += [jax.ShapeDtypeStruct(shp, F32) for shp in acc_outs]
    out_specs = [pl.BlockSpec((tr, w), lambda i: (i, 0)) for (w, _) in row_outs]
    out_specs += [pl.BlockSpec(shp, lambda i: (0, 0)) for shp in acc_outs]

    def body(*refs):
        ins = [r[...] for r in refs[:n_in]]
        outs = fn(*ins)
        outs = outs if isinstance(outs, (tuple, list)) else (outs,)
        assert len(outs) == n_ro + len(acc_outs), (name, len(outs))
        for o_ref, val in zip(refs[n_in:n_in + n_ro], outs[:n_ro]):
            o_ref[...] = val.astype(o_ref.dtype)
        i = pl.program_id(0)
        for a_ref, val in zip(refs[n_in + n_ro:], outs[n_ro:]):
            @pl.when(i == 0)
            def _(a_ref=a_ref, val=val):
                a_ref[...] = val.astype(F32)

            @pl.when(i > 0)
            def _(a_ref=a_ref, val=val):
                a_ref[...] += val.astype(F32)

    res = pl.pallas_call(
        body, name=name, grid=(S // tr,), in_specs=in_specs, out_specs=out_specs, out_shape=out_shape,
        compiler_params=_cparams(("arbitrary",) if acc_outs else ("parallel",)),
    )(*[r[0] for r in rows], *[e[0] for e in extra_specs], *consts)
    return res


def f32(*xs):
    return [x.astype(F32) for x in xs]


def f_rms(x, g):
    return x * lax.rsqrt(jnp.mean(x * x, axis=-1, keepdims=True) + NORM_EPS) * g


def f_swiglu(gx, ux):
    return silu(gx) * ux


def f_merge(gp, yg, yr, gate_b, d):
    gates = jax.nn.sigmoid(gp + gate_b)
    g1, g2 = _make_split((d, d), 1)(gates)
    return g1 * yg + g2 * yr


def rw_window(rw):
    o = dict(rr=0, rk=rw, rv=2 * rw, gd=3 * rw, wd=3 * rw + GATE_LORA)
    o["ad"] = o["wd"] + LORA_PAD
    o["a"] = o["ad"] + LORA_PAD
    o["used"] = o["a"] + LORA_PAD
    return o


def f_rw_pre(xs, w0, w_w2, a0, w_a2, w_g2, k_k, k_a, rw):
    win = xs.shape[1]
    o = rw_window(rw)
    sizes = (rw, rw, rw, GATE_LORA, LORA_PAD, LORA_PAD, win - o["a"])
    rr, rk, rv, gd, wd, ad, _ = _make_split(sizes, 1)(xs)
    w_raw = w0 + mm_nn(jnp.tanh(wd), w_w2)
    dec = jnp.exp(-jnp.exp(log_sigmoid(w_raw) - 0.5))
    a = jax.nn.sigmoid(a0 + mm_nn(ad, w_a2))
    g = mm_nn(jax.nn.sigmoid(gd), w_g2)
    kx = rk * k_k
    kk = kx / jnp.maximum(jnp.sqrt(segsum64(kx * kx)), 1e-12)
    k2 = rk * (1.0 + (a - 1.0) * k_a)
    return rr, dec, k2, rv, kk, kk * a, g


def f_rw_post(y, r, k2, v, g, lnx_w, lnx_b, r_k):
    mu = segsum64(y) * (1.0 / RW_HD)
    yc = y - mu
    var = segsum64(yc * yc) * (1.0 / RW_HD)
    yn = yc * lax.rsqrt(var + GN_EPS) * lnx_w + lnx_b
    bonus = segsum64(r * k2 * r_k) * v
    return (yn + bonus) * g


def f_gla_chunk(q, k, v, r, a, st_prev, w_a2, b_a, gn_w, heads):
    z = mm_nn(a, w_a2) + b_a
    la = log_sigmoid(z) * (1.0 / GLA_TAU)
    cum = cumsum_rows(la)
    total = jnp.sum(la, axis=0, keepdims=True)
    kdec = k * jnp.exp(total - cum)
    et = jnp.exp(total)
    qs = q * (GLA_DK ** -0.5)
    sk = _make_split((GLA_DK,) * heads, 1)
    sv = _make_split((GLA_DV,) * heads, 1)
    ss = _make_split((GLA_DV,) * heads, 0)
    kd_h, q_h, et_h, v_h, st_h = sk(kdec), sk(qs), sk(et), sv(v), ss(st_prev)
    outs, news = [], []
    for h in range(heads):
        st_new = st_h[h] * et_h[h] + mm_tn(v_h[h], kd_h[h])
        o = mm_nt(q_h[h], st_new)
        o = o * lax.rsqrt(jnp.mean(o * o, axis=-1, keepdims=True) + NORM_EPS) * gn_w
        outs.append(o)
        news.append(st_new)
    o_all = outs[0] if heads == 1 else jnp.concatenate(outs, axis=1)
    st_all = news[0] if heads == 1 else jnp.concatenate(news, axis=0)
    return o_all * silu(r), st_all


def gla_forward(p, lay, w_a2, b_a, gn_w, heads):
    S = p.shape[0]
    nc = S // CHUNK
    gq, gv = heads * GLA_DK, heads * GLA_DV

    def spec(width, off, rev=False):
        assert off % width == 0
        return pl.BlockSpec((CHUNK, width), functools.partial(lambda n, cb: (n, cb), cb=off // width))

    def body(q_ref, k_ref, v_ref, r_ref, a_ref, w_ref, b_ref, g_ref, o_ref, st_out_ref, st_sc):
        @pl.when(pl.program_id(0) == 0)
        def _():
            st_sc[...] = jnp.zeros_like(st_sc)

        st_prev = st_sc[...]
        st_out_ref[0] = st_prev
        o, st_new = f_gla_chunk(*f32(q_ref[...], k_ref[...], v_ref[...], r_ref[...], a_ref[...]), st_prev,
                                w_ref[...], b_ref[...], g_ref[...], heads)
        o_ref[...] = o.astype(o_ref.dtype)
        st_sc[...] = st_new

    return pl.pallas_call(
        body, name="gla_fwd", grid=(nc,),
        in_specs=[spec(gq, lay["q"]), spec(gq, lay["k"]), spec(gv, lay["v"]), spec(gv, lay["r"]), spec(LORA_PAD, lay["a"]),
                  pl.BlockSpec(w_a2.shape, lambda n: (0, 0)), pl.BlockSpec(b_a.shape, lambda n: (0, 0)),
                  pl.BlockSpec(gn_w.shape, lambda n: (0, 0))],
        out_specs=[pl.BlockSpec((CHUNK, gv), lambda n: (n, 0)), pl.BlockSpec((1, gv, GLA_DK), lambda n: (n, 0, 0))],
        out_shape=[jax.ShapeDtypeStruct((S, gv), BF16), jax.ShapeDtypeStruct((nc, gv, GLA_DK), F32)],
        scratch_shapes=[pltpu.VMEM((gv, GLA_DK), F32)],
        compiler_params=_cparams(("arbitrary",)),
    )(p, p, p, p, p, w_a2, b_a, gn_w)


def gla_backward(p, lay, states, d_out, w_a2, b_a, gn_w, heads):
    S = p.shape[0]
    nc = S // CHUNK
    gq, gv = heads * GLA_DK, heads * GLA_DV

    def spec(width, off):
        assert off % width == 0
        return pl.BlockSpec((CHUNK, width), functools.partial(lambda n, cb: (nc - 1 - n, cb), cb=off // width))

    def rev(width):
        return pl.BlockSpec((CHUNK, width), lambda n: (nc - 1 - n, 0))

    def whole(arr):
        return pl.BlockSpec(arr.shape, lambda n: (0, 0))

    def body(q_ref, k_ref, v_ref, r_ref, a_ref, st_ref, do_ref, w_ref, b_ref, g_ref,
             dq_ref, dk_ref, dv_ref, dr_ref, da_ref, dw_ref, db_ref, dg_ref, dst_sc):
        n = pl.program_id(0)

        @pl.when(n == 0)
        def _():
            dst_sc[...] = jnp.zeros_like(dst_sc)

        fn = functools.partial(f_gla_chunk, heads=heads)
        prim = (*f32(q_ref[...], k_ref[...], v_ref[...], r_ref[...], a_ref[...]), st_ref[0],
                w_ref[...].astype(F32), b_ref[...], g_ref[...])
        _, vjp = jax.vjp(fn, *prim)
        dq, dk, dv, dr, da, dst, dw, db, dg = vjp((do_ref[...].astype(F32), dst_sc[...]))
        for ref, val in ((dq_ref, dq), (dk_ref, dk), (dv_ref, dv), (dr_ref, dr), (da_ref, da)):
            ref[...] = val.astype(ref.dtype)
        dst_sc[...] = dst

        @pl.when(n == 0)
        def _():
            dw_ref[...] = dw
            db_ref[...] = db
            dg_ref[...] = dg

        @pl.when(n > 0)
        def _():
            dw_ref[...] += dw
            db_ref[...] += db
            dg_ref[...] += dg

    return pl.pallas_call(
        body, name="gla_bwd", grid=(nc,),
        in_specs=[spec(gq, lay["q"]), spec(gq, lay["k"]), spec(gv, lay["v"]), spec(gv, lay["r"]), spec(LORA_PAD, lay["a"]),
                  pl.BlockSpec((1, gv, GLA_DK), lambda n: (nc - 1 - n, 0, 0)), rev(gv),
                  whole(w_a2), whole(b_a), whole(gn_w)],
        out_specs=[rev(gq), rev(gq), rev(gv), rev(gv), rev(LORA_PAD), whole(w_a2), whole(b_a), whole(gn_w)],
        out_shape=[jax.ShapeDtypeStruct((S, gq), BF16), jax.ShapeDtypeStruct((S, gq), BF16),
                   jax.ShapeDtypeStruct((S, gv), BF16), jax.ShapeDtypeStruct((S, gv), BF16),
                   jax.ShapeDtypeStruct((S, LORA_PAD), F32),
                   jax.ShapeDtypeStruct(w_a2.shape, F32), jax.ShapeDtypeStruct(b_a.shape, F32),
                   jax.ShapeDtypeStruct(gn_w.shape, F32)],
        scratch_shapes=[pltpu.VMEM((gv, GLA_DK), F32)],
        compiler_params=_cparams(("arbitrary",)),
    )(p, p, p, p, p, states, d_out, w_a2, b_a, gn_w)


SCAN_BLOCK = 32


def _scan_helpers(nt):
    ones = _block_ones(128, RW_HD)
    rows = lax.broadcasted_iota(jnp.int32, (nt * RW_HD, 128), 0) % RW_HD
    lanes = lax.broadcasted_iota(jnp.int32, (nt * RW_HD, 128), 1) % RW_HD
    eye = rows == lanes

    def bc(ref, t):
        parts = [jnp.broadcast_to(ref[t, j:j + 1, :], (RW_HD, 128)) for j in range(nt)]
        return parts[0] if nt == 1 else jnp.concatenate(parts, axis=0)

    def seg1(x):
        return jnp.dot(x.astype(BF16), ones, preferred_element_type=F32)

    def column(ref, t):
        return seg1(jnp.where(eye, bc(ref, t), 0.0))

    def put_diag(ref, t, x):
        put_colsum(ref, t, jnp.where(eye, x, 0.0))

    def put_colsum(ref, t, x, sign=1.0):
        for j in range(nt):
            ref[t, j:j + 1, :] = sign * jnp.sum(x[j * RW_HD:(j + 1) * RW_HD, :], axis=0, keepdims=True)

    return bc, seg1, column, put_diag, put_colsum


def rwkv_scan_forward(r, w, k2, v, kk, b):
    S, nt, _ = r.shape
    tb = min(SCAN_BLOCK, S)

    def body(r_ref, w_ref, k2_ref, v_ref, kk_ref, b_ref, y_ref, st_ref, s_sc):
        @pl.when(pl.program_id(0) == 0)
        def _():
            s_sc[...] = jnp.zeros_like(s_sc)

        bc, seg1, column, put_diag, put_colsum = _scan_helpers(nt)

        def step(t, carry):
            s = s_sc[...]
            sa_e = seg1(s * bc(kk_ref, t))
            s = s * bc(w_ref, t) - sa_e * bc(b_ref, t) + column(v_ref, t) * bc(k2_ref, t)
            s_sc[...] = s
            st_ref[t] = s
            put_diag(y_ref, t, seg1(s * bc(r_ref, t)))
            return carry

        lax.fori_loop(0, tb, step, 0, unroll=8)

    row = pl.BlockSpec((tb, nt, 128), lambda i: (i, 0, 0))
    return pl.pallas_call(
        body, name="rwkv_scan_fwd", grid=(S // tb,),
        in_specs=[row] * 6,
        out_specs=[row, pl.BlockSpec((tb, nt * RW_HD, 128), lambda i: (i, 0, 0))],
        out_shape=[jax.ShapeDtypeStruct((S, nt, 128), F32), jax.ShapeDtypeStruct((S, nt * RW_HD, 128), F32)],
        scratch_shapes=[pltpu.VMEM((nt * RW_HD, 128), F32)],
        compiler_params=_cparams(("arbitrary",)),
    )(r, w, k2, v, kk, b)


def rwkv_scan_backward(r, w, k2, v, kk, b, states, dy):
    S, nt, _ = r.shape
    tb = min(SCAN_BLOCK, S)
    nb = S // tb

    def body(r_ref, w_ref, k2_ref, v_ref, kk_ref, b_ref, st_ref, edge_ref, dy_ref,
             dr_ref, dw_ref, dk2_ref, dv_ref, dkk_ref, db_ref, ds_sc, before_sc):
        @pl.when(pl.program_id(0) == 0)
        def _():
            ds_sc[...] = jnp.zeros_like(ds_sc)

        before_sc[...] = jnp.where(pl.program_id(0) == nb - 1, 0.0, edge_ref[0])

        bc, seg1, column, put_diag, put_colsum = _scan_helpers(nt)

        def step(i, carry):
            t = tb - 1 - i
            s_prev = jnp.where(t == 0, before_sc[...], st_ref[jnp.maximum(t - 1, 0)])
            r_e, w_e, k2_e, kk_e, b_e = (bc(ref, t) for ref in (r_ref, w_ref, k2_ref, kk_ref, b_ref))
            v_e = column(v_ref, t)
            sa_e = seg1(s_prev * kk_e)
            dy_e = column(dy_ref, t)
            put_colsum(dr_ref, t, st_ref[t] * dy_e)
            ds = ds_sc[...] + dy_e * r_e
            put_colsum(dw_ref, t, ds * s_prev)
            nsa_e = seg1(ds * b_e)
            put_colsum(db_ref, t, ds * sa_e, -1.0)
            put_diag(dv_ref, t, seg1(ds * k2_e))
            put_colsum(dk2_ref, t, ds * v_e)
            put_colsum(dkk_ref, t, s_prev * nsa_e, -1.0)
            ds_sc[...] = ds * w_e - nsa_e * kk_e
            return carry

        lax.fori_loop(0, tb, step, 0, unroll=8)

    row = pl.BlockSpec((tb, nt, 128), lambda i: (nb - 1 - i, 0, 0))
    return pl.pallas_call(
        body, name="rwkv_scan_bwd", grid=(nb,),
        in_specs=[row] * 6 + [pl.BlockSpec((tb, nt * RW_HD, 128), lambda i: (nb - 1 - i, 0, 0)),
                              pl.BlockSpec((1, nt * RW_HD, 128), lambda i: (jnp.maximum((nb - 1 - i) * tb - 1, 0), 0, 0)), row],
        out_specs=[row] * 6,
        out_shape=[jax.ShapeDtypeStruct((S, nt, 128), F32)] * 6,
        scratch_shapes=[pltpu.VMEM((nt * RW_HD, 128), F32), pltpu.VMEM((nt * RW_HD, 128), F32)],
        compiler_params=_cparams(("arbitrary",)),
    )(r, w, k2, v, kk, b, states, states, dy)


def _edge_spec(width, col_block, tr, n_rows, after):
    last = n_rows // 8 - 1
    if after:
        return pl.BlockSpec((8, width), lambda i: (jnp.minimum((i + 1) * (tr // 8), last), col_block))
    return pl.BlockSpec((8, width), lambda i: (jnp.maximum(i * (tr // 8) - 1, 0), col_block))


def _shifted_prev(p, prev8):
    first = jnp.where(pl.program_id(0) == 0, 0.0, prev8[7:8, :])
    rows = lax.broadcasted_iota(jnp.int32, p.shape, 0)
    return jnp.where(rows == 0, first, pltpu.roll(p, 1, axis=0))


def token_shift_forward(p, mu_win, win, tr):
    def fn(pw, prev8, mu):
        return pw + mu * (_shifted_prev(pw, prev8) - pw)

    return rowwise("token_shift_fwd", fn, [_cols(p, win, 0)], [mu_win], [(win, F32)], [], tr,
                   extra_specs=[(p, _edge_spec(win, 0, tr, p.shape[0], False))])[0]


def token_shift_backward(p, dxs, da_gla, mu_win, win, a_off, tr):
    S = p.shape[0]
    n = S // min(tr, S)

    def fn(pw, dx, da, prev8, next8, mu):
        trr = pw.shape[0]
        last = jnp.where(pl.program_id(0) == n - 1, 0.0, next8[0:1, :])
        rows = lax.broadcasted_iota(jnp.int32, dx.shape, 0)
        dnext = jnp.where(rows == trr - 1, last, pltpu.roll(dx, trr - 1, axis=0))
        dp = (1.0 - mu) * dx + mu * dnext
        dp = jnp.concatenate([dp[:, :a_off], dp[:, a_off:a_off + LORA_PAD] + da, dp[:, a_off + LORA_PAD:]], axis=1)
        dmu = jnp.sum(dx * (_shifted_prev(pw, prev8) - pw), axis=0, keepdims=True)
        return dp, dmu

    return rowwise("token_shift_bwd", fn, [_cols(p, win, 0), _cols(dxs), _cols(da_gla)], [mu_win],
                   [(win, BF16)], [(1, win)], tr,
                   extra_specs=[(p, _edge_spec(win, 0, tr, S, False)), (dxs, _edge_spec(win, 0, tr, S, True))])


def _adamw_math(w, g, m, v):
    m = ADAM_B1 * m + (1.0 - ADAM_B1) * g
    v = ADAM_B2 * v + (1.0 - ADAM_B2) * (g * g)
    m_hat = m / (1.0 - ADAM_B1 ** ADAM_STEP)
    v_hat = v / (1.0 - ADAM_B2 ** ADAM_STEP)
    delta = -ADAM_LR * (m_hat / (jnp.sqrt(v_hat) + ADAM_EPS) + ADAM_WD * w)
    return delta, m, v


def adamw_small(ws, gs, ms, vs):
    n = len(ws)

    def body(*refs):
        for i in range(n):
            d, m, v = _adamw_math(refs[i][...], refs[n + i][...], refs[2 * n + i][...], refs[3 * n + i][...])
            refs[4 * n + i][...] = d
            refs[5 * n + i][...] = m
            refs[6 * n + i][...] = v

    shapes = [jax.ShapeDtypeStruct(w.shape, F32) for w in ws]
    outs = pl.pallas_call(body, name="adamw_small", out_shape=shapes * 3, compiler_params=_cparams())(*ws, *gs, *ms, *vs)
    return outs[:n], outs[n:2 * n], outs[2 * n:]


def _place():
    x, y, c = lax.axis_index("x"), lax.axis_index("y"), lax.axis_index("c")
    chips = [(1 - x, y), (x, 1 - y), (1 - x, 1 - y)]
    return x, y, c, chips


def _full_shape(kind, r, c):
    return {"col": (r, 4 * c), "row": (4 * r, c), "slab": (4, r, c)}[kind]


def _slab(ref, kind, k, r, c, half=None):
    n, off = (r, 0) if half is None else (r // 2, half * (r // 2))
    if kind == "col":
        return ref.at[pl.ds(off, n), pl.ds(k * c, c)]
    if kind == "row":
        return ref.at[pl.ds(k * r + off, n), :]
    return ref.at[k, pl.ds(off, n), :]


def _remote(src, dst, sems, idx, to):
    return pltpu.make_async_remote_copy(src_ref=src, dst_ref=dst, send_sem=sems[0].at[idx], recv_sem=sems[1].at[idx],
                                        device_id=to, device_id_type=MESH_IDS)


class CommJob:
    def __init__(self, operands, out_shapes, scratch, start, finish):
        self.operands, self.out_shapes, self.scratch, self.start, self.finish = operands, out_shapes, scratch, start, finish


def gather_job(big, small=()):
    big, small = list(big), list(small)
    nb, ns = len(big), len(small)
    meta = [(kind, *a.shape) for a, kind in big + small]

    def sends(srcs, outs, sems):
        own_s, own_r, ici_s, ici_r, _, _, sm_s, sm_r = sems
        x, y, c, chips = _place()
        me, sib = 2 * x + y, (x, y, 1 - c)
        cps = []
        for a in range(nb):
            kind, r, cc = meta[a]
            for j, chip in enumerate(chips):
                cps.append(_remote(srcs[a].at[pl.ds(c * (r // 2), r // 2)], _slab(outs[a], kind, me, r, cc, c),
                                   (ici_s, ici_r), (a, j), (*chip, c)))
        for a in range(nb):
            kind, r, cc = meta[a]
            cps.append(_remote(srcs[a], _slab(outs[a], kind, me, r, cc), (own_s, own_r), (a,), sib))
        for s in range(ns):
            kind, r, cc = meta[nb + s]
            for t, to in enumerate([sib] + [(*chip, c) for chip in chips]):
                cps.append(_remote(srcs[nb + s], _slab(outs[nb + s], kind, me, r, cc), (sm_s, sm_r), (s, t), to))
        return cps

    def start(srcs, outs, sems):
        for cp in sends(srcs, outs, sems):
            cp.start()

    def finish(srcs, outs, sems):
        own_s, own_r, ici_s, ici_r, fwd_s, fwd_r, sm_s, sm_r = sems
        x, y, c, chips = _place()
        me, sib = 2 * x + y, (x, y, 1 - c)
        cids = [2 * chip[0] + chip[1] for chip in chips]
        hands = []
        for a in range(nb):
            kind, r, cc = meta[a]
            for j in range(3):
                blk = _slab(outs[a], kind, cids[j], r, cc, c)
                _remote(blk, blk, (ici_s, ici_r), (a, j), sib).wait_recv()
                hands.append(_remote(blk, blk, (fwd_s, fwd_r), (a, j), sib))
                hands[-1].start()
        for a in range(nb):
            kind, r, cc = meta[a]
            for j in range(3):
                blk = _slab(outs[a], kind, cids[j], r, cc, 1 - c)
                _remote(blk, blk, (fwd_s, fwd_r), (a, j), sib).wait_recv()
            blk = _slab(outs[a], kind, me, r, cc)
            _remote(blk, blk, (own_s, own_r), (a,), sib).wait_recv()
        for s in range(ns):
            kind, r, cc = meta[nb + s]
            for t, frm in enumerate([me] + cids):
                blk = _slab(outs[nb + s], kind, frm, r, cc)
                _remote(blk, blk, (sm_s, sm_r), (s, t), sib).wait_recv()
        for cp in sends(srcs, outs, sems) + hands:
            cp.wait_send()

    dma = pltpu.SemaphoreType.DMA
    nb1, ns1 = max(nb, 1), max(ns, 1)
    return CommJob([a for a, _ in big + small],
                   [jax.ShapeDtypeStruct(_full_shape(kind, r, cc), BF16) for (kind, r, cc) in meta],
                   [dma((nb1,)), dma((nb1,)), dma((nb1, 3)), dma((nb1, 3)), dma((nb1, 3)), dma((nb1, 3)),
                    dma((ns1, 4)), dma((ns1, 4))], start, finish)


def run_job(name, job):
    n_in, n_out = len(job.operands), len(job.out_shapes)

    def body(*refs):
        ins, outs, sems = refs[:n_in], refs[n_in:n_in + n_out], refs[n_in + n_out:]
        job.start(ins, outs, sems)
        job.finish(ins, outs, sems)

    return pl.pallas_call(body, name=name, in_specs=[HBM_SPEC] * n_in, out_specs=[HBM_SPEC] * n_out,
                          out_shape=job.out_shapes, scratch_shapes=job.scratch)(*job.operands)


def sibling_swap(name, arrays):
    n = len(arrays)

    def body(*refs):
        srcs, outs, sems = refs[:n], refs[n:2 * n], refs[2 * n:]
        x, y, c, _ = _place()
        cps = [_remote(srcs[a], outs[a], sems, (a,), (x, y, 1 - c)) for a in range(n)]
        for cp in cps:
            cp.start()
        for cp in cps:
            cp.wait_recv()
        for cp in cps:
            cp.wait_send()

    return pl.pallas_call(
        body, name=name, in_specs=[HBM_SPEC] * n, out_specs=[HBM_SPEC] * n,
        out_shape=[jax.ShapeDtypeStruct(a.shape, a.dtype) for a in arrays],
        scratch_shapes=[pltpu.SemaphoreType.DMA((n,)), pltpu.SemaphoreType.DMA((n,))],
    )(*arrays)


def exchange_job(sums):
    n = len(sums)

    def sends(srcs, outs, sems):
        ici_s, ici_r, sib_s, sib_r = sems
        x, y, c, chips = _place()
        me, sib = 2 * x + y, (x, y, 1 - c)
        cps = []
        for a, (_, kind, r, cc) in enumerate(sums):
            for j, chip in enumerate(chips):
                cid = 2 * chip[0] + chip[1]
                cps.append(_remote(_slab(srcs[a], kind, cid, r, cc, c), outs[a].at[me], (ici_s, ici_r), (a, j), (*chip, c)))
            cps.append(_remote(_slab(srcs[a], kind, me, r, cc, 1 - c), outs[a].at[me], (sib_s, sib_r), (a,), sib))
        return cps

    def start(srcs, outs, sems):
        for cp in sends(srcs, outs, sems):
            cp.start()

    def finish(srcs, outs, sems):
        ici_s, ici_r, sib_s, sib_r = sems
        x, y, c, chips = _place()
        me, sib = 2 * x + y, (x, y, 1 - c)
        for a in range(n):
            for j, chip in enumerate(chips):
                blk = outs[a].at[2 * chip[0] + chip[1]]
                _remote(blk, blk, (ici_s, ici_r), (a, j), sib).wait_recv()
            _remote(outs[a].at[me], outs[a].at[me], (sib_s, sib_r), (a,), sib).wait_recv()
        for cp in sends(srcs, outs, sems):
            cp.wait_send()

    dma = pltpu.SemaphoreType.DMA
    return CommJob([s[0] for s in sums], [jax.ShapeDtypeStruct((4, r // 2, cc), BF16) for (_, _, r, cc) in sums],
                   [dma((n, 3)), dma((n, 3)), dma((n,)), dma((n,))], start, finish)


def allreduce_small(vec):
    R, C = vec.shape

    def body(src, out, gathered, send_sems, recv_sems):
        x, y, c, _ = _place()
        me = 4 * x + 2 * y + c
        gathered[me] = src[...]
        peers = [(fx, fy, fc) for fx in (0, 1) for fy in (0, 1) for fc in (0, 1) if (fx, fy, fc) != (0, 0, 0)]
        sends = []
        for j, (fx, fy, fc) in enumerate(peers):
            to = (x ^ fx, y ^ fy, c ^ fc)
            cp = pltpu.make_async_remote_copy(
                src_ref=src, dst_ref=gathered.at[me], send_sem=send_sems.at[j], recv_sem=recv_sems.at[j],
                device_id=to, device_id_type=MESH_IDS)
            cp.start()
            sends.append(cp)
        for j, (fx, fy, fc) in enumerate(peers):
            frm = 4 * (x ^ fx) + 2 * (y ^ fy) + (c ^ fc)
            pltpu.make_async_remote_copy(
                src_ref=src, dst_ref=gathered.at[frm], send_sem=send_sems.at[j], recv_sem=recv_sems.at[j],
                device_id=(x, y, c), device_id_type=MESH_IDS).wait_recv()
        for cp in sends:
            cp.wait_send()
        acc = gathered[0]
        for k in range(1, 8):
            acc = acc + gathered[k]
        out[...] = acc

    vm = pl.BlockSpec(memory_space=pltpu.VMEM)
    return pl.pallas_call(
        body, name="allreduce_small", in_specs=[vm], out_specs=vm,
        out_shape=jax.ShapeDtypeStruct((R, C), F32),
        scratch_shapes=[pltpu.VMEM((8, R, C), F32), pltpu.SemaphoreType.DMA((7,)), pltpu.SemaphoreType.DMA((7,))],
        compiler_params=_cparams(),
    )(vec)


def pair_sum(name, mine, theirs):
    rows, cols = mine.shape
    tr = _pick(rows, max(16, (1 << 20) // cols), 16)
    return rowwise(name, lambda a, b: a.astype(F32) + b.astype(F32), [_cols(mine), _cols(theirs)], [], [(cols, BF16)], [], tr)[0]


def chip_sum(name, rb):
    _, rh, C = rb.shape
    tr = _pick(rh, max(16, (1 << 19) // C), 16)

    def body(r_ref, o_ref):
        acc = r_ref[0].astype(F32)
        for k in range(1, 4):
            acc = acc + r_ref[k].astype(F32)
        o_ref[...] = acc

    return pl.pallas_call(body, name=name, grid=(rh // tr,),
                          in_specs=[pl.BlockSpec((4, tr, C), lambda i: (0, i, 0))],
                          out_specs=pl.BlockSpec((tr, C), lambda i: (i, 0)),
                          out_shape=jax.ShapeDtypeStruct((rh, C), F32),
                          compiler_params=_cparams(("parallel",)))(rb)


def adamw_halves(name, w, mine, theirs, m, v):
    rows, cols = w.shape
    tr = _pick(rows // 2, max(8, (1 << 19) // cols), 8)
    nbh = rows // 2 // tr
    full = pl.BlockSpec((tr, cols), lambda i: (i, 0))
    half = pl.BlockSpec((tr, cols), lambda i: (i % nbh, 0))

    def body(w_ref, a_ref, b_ref, m_ref, v_ref, g_out, d_out, m_out, v_out):
        is_mine = (pl.program_id(0) // nbh) == lax.axis_index("c")
        g = jnp.where(is_mine, a_ref[...], b_ref[...])
        d, mn, vn = _adamw_math(w_ref[...], g, m_ref[...], v_ref[...])
        g_out[...] = g
        d_out[...] = d
        m_out[...] = mn
        v_out[...] = vn

    return pl.pallas_call(body, name="adamw_" + name, grid=(rows // tr,), in_specs=[full, half, half, full, full],
                          out_specs=[full] * 4, out_shape=[jax.ShapeDtypeStruct((rows, cols), F32)] * 4,
                          compiler_params=_cparams(("parallel",)))(w, mine, theirs, m, v)


BIG = {"ffn1_wg": "col", "ffn1_wu": "col", "ffn1_wd": "row", "w_in": "slab", "w_branch": "row", "w_out": "row",
       "ffn2_wg": "col", "ffn2_wu": "col", "ffn2_wd": "row"}
LORA = ["gla_w_a2", "rwkv_w_w2", "rwkv_w_a2", "rwkv_w_g2"]
REPLICATED = ["ffn1_norm", "mix_norm", "gla_b_a", "gla_gn_w", "rwkv_mu", "rwkv_w0", "rwkv_a0", "rwkv_k_k", "rwkv_k_a",
              "rwkv_r_k", "rwkv_lnx_w", "rwkv_lnx_b", "gate_b", "ffn2_norm", "final_norm"]
WEIGHTS = ["ffn1_norm", "ffn1_wg", "ffn1_wu", "ffn1_wd", "mix_norm", "w_in", "gla_w_a2", "gla_b_a", "gla_gn_w", "rwkv_mu",
           "rwkv_w0", "rwkv_w_w2", "rwkv_a0", "rwkv_w_a2", "rwkv_w_g2", "rwkv_k_k", "rwkv_k_a", "rwkv_r_k", "rwkv_lnx_w",
           "rwkv_lnx_b", "gate_b", "w_branch", "w_out", "ffn2_norm", "ffn2_wg", "ffn2_wu", "ffn2_wd", "final_norm"]


def kernel(x, ffn1_norm, ffn1_wg, ffn1_wu, ffn1_wd, mix_norm, w_in, gla_w_a2, gla_b_a, gla_gn_w, rwkv_mu, rwkv_w0, rwkv_w_w2, rwkv_a0, rwkv_w_a2, rwkv_w_g2, rwkv_k_k, rwkv_k_a, rwkv_r_k, rwkv_lnx_w, rwkv_lnx_b, gate_b, w_branch, w_out, ffn2_norm, ffn2_wg, ffn2_wu, ffn2_wd, final_norm, loss_target, m_ffn1_norm, m_ffn1_wg, m_ffn1_wu, m_ffn1_wd, m_mix_norm, m_w_in, m_gla_w_a2, m_gla_b_a, m_gla_gn_w, m_rwkv_mu, m_rwkv_w0, m_rwkv_w_w2, m_rwkv_a0, m_rwkv_w_a2, m_rwkv_w_g2, m_rwkv_k_k, m_rwkv_k_a, m_rwkv_r_k, m_rwkv_lnx_w, m_rwkv_lnx_b, m_gate_b, m_w_branch, m_w_out, m_ffn2_norm, m_ffn2_wg, m_ffn2_wu, m_ffn2_wd, m_final_norm, v_ffn1_norm, v_ffn1_wg, v_ffn1_wu, v_ffn1_wd, v_mix_norm, v_w_in, v_gla_w_a2, v_gla_b_a, v_gla_gn_w, v_rwkv_mu, v_rwkv_w0, v_rwkv_w_w2, v_rwkv_a0, v_rwkv_w_a2, v_rwkv_w_g2, v_rwkv_k_k, v_rwkv_k_a, v_rwkv_r_k, v_rwkv_lnx_w, v_rwkv_lnx_b, v_gate_b, v_w_branch, v_w_out, v_ffn2_norm, v_ffn2_wg, v_ffn2_wu, v_ffn2_wd, v_final_norm):
    args = dict(locals())
    wts = {n: args[n] for n in WEIGHTS}
    moms = {n: args["m_" + n] for n in WEIGHTS}
    vars_ = {n: args["v_" + n] for n in WEIGHTS}

    xs = x[0]
    tgt = loss_target[0]
    S, D = xs.shape
    FF = ffn1_wd.shape[1] * 4
    gheads = gla_b_a.shape[-1] // GLA_DK
    GQ, GV = gheads * GLA_DK, gheads * GLA_DV
    rheads = rwkv_r_k.shape[1]
    RW = rheads * RW_HD
    NT = RW // 128
    lo_g = gla_w_a2.shape[1]
    lo_w = rwkv_w_w2.shape[1]
    lo_a = rwkv_w_a2.shape[1]
    assert rwkv_w_g2.shape[1] == GATE_LORA and RW % 128 == 0

    ow = rw_window(RW)
    WIN = -(-ow["used"] // (2 * D)) * (2 * D)
    lay = dict(gate=WIN, v=WIN + 2 * D, r=WIN + 2 * D + GV, q=WIN + 2 * D + 2 * GV, k=WIN + 2 * D + 2 * GV + GQ, a=ow["a"])
    DP = lay["k"] + GQ
    DIN = w_in.shape[-1] * 4
    o_sizes = [GQ, GQ, GV, GV, lo_g, RW, RW, RW, lo_w, lo_a, GATE_LORA, 2 * D]
    o_offs = [sum(o_sizes[:i]) for i in range(len(o_sizes))]
    assert o_offs[-1] + o_sizes[-1] == DIN
    p_offs = [lay["q"], lay["k"], lay["v"], lay["r"], ow["a"], ow["rr"], ow["rk"], ow["rv"], ow["wd"], ow["ad"], ow["gd"], lay["gate"]]

    def to_padded(w):
        order = sorted(range(len(o_sizes)), key=lambda i: p_offs[i])
        parts, pos = [], 0
        for i in order:
            if p_offs[i] > pos:
                parts.append(jnp.zeros((w.shape[0], p_offs[i] - pos), w.dtype))
            parts.append(w[:, o_offs[i]:o_offs[i] + o_sizes[i]])
            pos = p_offs[i] + o_sizes[i]
        if pos < DP:
            parts.append(jnp.zeros((w.shape[0], DP - pos), w.dtype))
        return jnp.concatenate(parts, axis=1)

    def from_padded(w):
        return jnp.concatenate([w[:, p_offs[i]:p_offs[i] + o_sizes[i]] for i in range(len(o_sizes))], axis=1)

    def pad_rows(w, rows):
        return jnp.pad(w, ((0, rows - w.shape[0]), (0, 0)))

    mu = rwkv_mu[0]
    mu_parts = {"rr": mu[0:RW], "rk": mu[RW:2 * RW], "rv": mu[2 * RW:3 * RW], "wd": mu[3 * RW:3 * RW + lo_w],
                "ad": mu[3 * RW + lo_w:3 * RW + lo_w + lo_a], "gd": mu[3 * RW + lo_w + lo_a:]}
    mu_win = jnp.zeros((WIN,), F32)
    for key, val in mu_parts.items():
        mu_win = lax.dynamic_update_slice(mu_win, val, (ow[key],))
    mu_win = mu_win.reshape(1, WIN)

    shard_shapes = {n: wts[n].shape[1:] for n in list(BIG) + LORA}
    W = {}

    def shard(n):
        return (wts[n][0].astype(BF16), BIG[n])

    def mm_gather(a, b, out_dtype, name, gather, lora=()):
        out, got = matmul(a, b, "nn", out_dtype, name,
                          job=gather_job([shard(n) for n in gather], [(wts[n][0].astype(BF16), "col") for n in lora]))
        W.update(zip(list(gather) + list(lora), got))
        return out

    W["ffn1_wg"] = run_job("gather_ffn1_wg", gather_job([shard("ffn1_wg")]))[0]
    r_k = rwkv_r_k.reshape(1, RW)
    fin_g = final_norm.reshape(1, D)

    TR = min(128, S)
    def swiglu_act(tag, gx, ux):
        return rowwise(tag + "_act", lambda a, b: f_swiglu(*f32(a, b)), [_cols(gx), _cols(ux)], [], [(FF, BF16)], [], TR)[0]

    h1 = rowwise("rms1", lambda a, g: f_rms(a, g), [_cols(xs)], [ffn1_norm], [(D, BF16)], [], TR)[0]
    g1 = mm_gather(h1, W["ffn1_wg"], BF16, "ffn1_g", ["ffn1_wu"])
    u1 = mm_gather(h1, W["ffn1_wu"], BF16, "ffn1_u", ["ffn1_wd"])
    act1 = swiglu_act("ffn1", g1, u1)
    f1 = mm_gather(act1, W["ffn1_wd"], F32, "ffn1_d", ["w_in"], LORA)
    w_in_p = to_padded(W["w_in"].transpose(1, 0, 2).reshape(D, DIN))
    gla_a2_p = pad_rows(W["gla_w_a2"], LORA_PAD)
    w_w2_p = pad_rows(W["rwkv_w_w2"], LORA_PAD)
    w_a2_p = pad_rows(W["rwkv_w_a2"], LORA_PAD)
    w_g2 = W["rwkv_w_g2"]

    def res_rms(coef):
        def fn(a, f, g):
            x1 = a + coef * f
            return x1, f_rms(x1, g)
        return fn

    x1, h2 = rowwise("res_rms_mix", res_rms(0.5), [_cols(xs), _cols(f1)], [mix_norm], [(D, F32), (D, BF16)], [], TR)
    p = mm_gather(h2, w_in_p, F32, "w_in", ["w_branch", "w_out", "ffn2_wg"])
    wb_g, wb_r = W["w_branch"][:GV], W["w_branch"][GV:]

    o_gla, gla_states = gla_forward(p, lay, gla_a2_p, gla_b_a, gla_gn_w, gheads)
    xsh = token_shift_forward(p, mu_win, WIN, TR)
    pre_consts = [rwkv_w0, w_w2_p, rwkv_a0, w_a2_p, w_g2, rwkv_k_k, rwkv_k_a]
    pre_fn = functools.partial(f_rw_pre, rw=RW)

    def pre_f32(xw, w0, ww, a0, wa, wg_, kk_, ka_):
        return pre_fn(xw, w0, ww.astype(F32), a0, wa.astype(F32), wg_.astype(F32), kk_, ka_)

    r_, dec_, k2_, v_, kk_, b_, g_ = rowwise("rw_pre", pre_f32, [_cols(xsh)], pre_consts, [(RW, F32)] * 7, [], 128)

    def tiles(a):
        return a.reshape(S, NT, 128)

    y_t, rw_states = rwkv_scan_forward(*(tiles(a) for a in (r_, dec_, k2_, v_, kk_, b_)))
    y_ = y_t.reshape(S, RW)
    post_consts = [rwkv_lnx_w, rwkv_lnx_b, r_k]
    o_rw = rowwise("rw_post", f_rw_post, [_cols(a) for a in (y_, r_, k2_, v_, g_)], post_consts, [(RW, BF16)], [], TR)[0]

    yg = matmul(o_gla, wb_g, "nn", F32, "branch_gla")
    yr = matmul(o_rw, wb_r, "nn", F32, "branch_rw")
    merge_fn = functools.partial(f_merge, d=D)
    merged = rowwise("merge", merge_fn, [_cols(p, 2 * D, lay["gate"]), _cols(yg), _cols(yr)], [gate_b], [(D, BF16)], [], TR)[0]
    mix = matmul(merged, W["w_out"], "nn", F32, "w_out")
    x2, h3 = rowwise("res_rms_ffn2", res_rms(1.0), [_cols(x1), _cols(mix)], [ffn2_norm], [(D, F32), (D, BF16)], [], TR)
    g3 = mm_gather(h3, W["ffn2_wg"], BF16, "ffn2_g", ["ffn2_wu"])
    u3 = mm_gather(h3, W["ffn2_wu"], BF16, "ffn2_u", ["ffn2_wd"])
    act3 = swiglu_act("ffn2", g3, u3)
    f3 = matmul(act3, W["ffn2_wd"], "nn", F32, "ffn2_d")

    def final_fn(a, f, t, g):
        def loss_of(a, f, g):
            yv = f_rms(a + 0.5 * f, g)
            return 0.5 * jnp.sum(jnp.mean(jnp.square(yv - t), axis=-1))
        val, vjp = jax.vjp(loss_of, a, f, g)
        da, df, dg = vjp(jnp.ones((), F32))
        return da, df, jnp.full((1, 128), val, F32), dg

    dx2, df3, loss_acc, d_final = rowwise("final_loss", final_fn, [_cols(x2), _cols(f3), _cols(tgt)], [fin_g],
                                          [(D, F32), (D, BF16)], [(1, 128), (1, D)], TR)
    grads = {"final_norm": d_final.reshape(D)}

    received = {}

    def pair_up(n, dw):
        r, cc = shard_shapes[n]
        flat = (4 * r, cc) if BIG[n] == "slab" else dw.shape
        theirs = sibling_swap("swap_" + n, [dw])[0]
        return n, (pair_sum("pair_" + n, dw.reshape(flat), theirs.reshape(flat)).reshape(dw.shape), BIG[n], r, cc)

    def mm_exchange(a, b, mode, out_dtype, name, pending):
        out, got = matmul(a, b, mode, out_dtype, name, job=exchange_job([entry for _, entry in pending]))
        received.update(zip([n for n, _ in pending], got))
        return out

    def ffn_backward(tag, h, gx, ux, act, df, wg, wu, wd):
        dact = matmul(df, wd, "nt", F32, tag + "_dact")
        p_wd = pair_up(tag + "_wd", matmul(act, df, "tn", BF16, tag + "_dwd"))

        def fn(a, b, d):
            _, vjp = jax.vjp(f_swiglu, *f32(a, b))
            return vjp(d)

        dgx, dux = rowwise(tag + "_dact_bwd", fn, [_cols(gx), _cols(ux), _cols(dact)], [], [(FF, BF16)] * 2, [], TR)
        p_wg = pair_up(tag + "_wg", mm_exchange(h, dgx, "tn", BF16, tag + "_dwg", [p_wd]))
        p_wu = pair_up(tag + "_wu", mm_exchange(h, dux, "tn", BF16, tag + "_dwu", [p_wg]))
        dha = mm_exchange(dgx, wg, "nt", F32, tag + "_dh_g", [p_wu])
        dhb = matmul(dux, wu, "nt", F32, tag + "_dh_u")
        return dha, dhb

    def res_rms_bwd(name, coef, a, f, g, dx1, dha, dhb):
        def fn(a, f, dx1, dha, dhb, g):
            _, vjp = jax.vjp(res_rms(coef), a, f, g)
            return vjp((dx1, dha + dhb))

        return rowwise(name, fn, [_cols(a), _cols(f), _cols(dx1), _cols(dha), _cols(dhb)], [g],
                       [(D, F32), (D, BF16)], [(1, D)], TR)

    dh3a, dh3b = ffn_backward("ffn2", h3, g3, u3, act3, df3, W["ffn2_wg"], W["ffn2_wu"], W["ffn2_wd"])
    dx1, dmix, grads["ffn2_norm"] = res_rms_bwd("res_rms_ffn2_bwd", 1.0, x1, mix, ffn2_norm, dx2, dh3a, dh3b)

    p_wo = pair_up("w_out", matmul(merged, dmix, "tn", BF16, "d_w_out"))
    dmerged = mm_exchange(dmix, W["w_out"], "nt", F32, "d_merged", [p_wo])

    def merge_bwd(gp, a, b, d, gb):
        _, vjp = jax.vjp(merge_fn, gp, a, b, gb)
        return vjp(d)

    dgate, dyg, dyr, grads["gate_b"] = rowwise(
        "merge_bwd", merge_bwd, [_cols(p, 2 * D, lay["gate"]), _cols(yg), _cols(yr), _cols(dmerged)], [gate_b],
        [(2 * D, BF16), (D, BF16), (D, BF16)], [(1, 2 * D)], TR)
    do_gla = matmul(dyg, wb_g, "nt", BF16, "d_o_gla")
    do_rw = matmul(dyr, wb_r, "nt", F32, "d_o_rw")
    p_wb = pair_up("w_branch", jnp.concatenate([matmul(o_gla, dyg, "tn", BF16, "d_wb_gla"),
                                                matmul(o_rw, dyr, "tn", BF16, "d_wb_rw")], axis=0))

    def post_bwd(yv, rv, kv, vv, gv_, d, lw, lb, rk):
        _, vjp = jax.vjp(f_rw_post, yv, rv, kv, vv, gv_, lw, lb, rk)
        return vjp(d)

    dy_, dr_p, dk2_p, dv_p, dg_p, grads["rwkv_lnx_w"], grads["rwkv_lnx_b"], d_rk = rowwise(
        "rw_post_bwd", post_bwd, [_cols(a) for a in (y_, r_, k2_, v_, g_, do_rw)], post_consts,
        [(RW, F32)] * 5, [(1, RW)] * 3, 128)
    grads["rwkv_r_k"] = d_rk.reshape(rwkv_r_k.shape[1:])

    scan_cots = rwkv_scan_backward(*(tiles(a) for a in (r_, dec_, k2_, v_, kk_, b_)), rw_states, tiles(dy_))
    dr_s, dw_s, dk2_s, dv_s, dkk_s, db_s = (a.reshape(S, RW) for a in scan_cots)

    def pre_bwd(xw, c0, c1, c2, c3, c4, c5, c6, c7, c8, c9, w0, ww, a0, wa, wg_, kk_c, ka_c):
        _, vjp = jax.vjp(pre_fn, xw, w0, ww.astype(F32), a0, wa.astype(F32), wg_.astype(F32), kk_c, ka_c)
        return vjp((c0 + c6, c1, c2 + c7, c3 + c8, c4, c5, c9))

    dxsh, grads["rwkv_w0"], d_ww2, grads["rwkv_a0"], d_wa2, d_wg2, grads["rwkv_k_k"], grads["rwkv_k_a"] = rowwise(
        "rw_pre_bwd", pre_bwd,
        [_cols(xsh)] + [_cols(a) for a in (dr_s, dw_s, dk2_s, dv_s, dkk_s, db_s, dr_p, dk2_p, dv_p, dg_p)], pre_consts,
        [(WIN, F32)], [(1, RW), w_w2_p.shape, (1, RW), w_a2_p.shape, w_g2.shape, (1, RW), (1, RW)], 128)
    grads["rwkv_w_w2"], grads["rwkv_w_a2"], grads["rwkv_w_g2"] = d_ww2[:lo_w], d_wa2[:lo_a], d_wg2

    dq, dk, dv, dr, da, d_ga2, grads["gla_b_a"], grads["gla_gn_w"] = gla_backward(
        p, lay, gla_states, do_gla, gla_a2_p, gla_b_a, gla_gn_w, gheads)
    grads["gla_w_a2"] = d_ga2[:lo_g]

    dpw, dmu_win = token_shift_backward(p, dxsh, da, mu_win, WIN, ow["a"], TR)
    dmu = dmu_win[0]
    grads["rwkv_mu"] = jnp.concatenate([dmu[ow[k_]:ow[k_] + mu_parts[k_].shape[0]] for k_ in ("rr", "rk", "rv", "wd", "ad", "gd")]).reshape(1, -1)

    dp = jnp.concatenate([dpw, dgate, dv, dr, dq, dk], axis=1)
    d_w_in = from_padded(mm_exchange(h2, dp, "tn", BF16, "d_w_in", [p_wb]))
    p_wi = pair_up("w_in", d_w_in.reshape(D, 4, shard_shapes["w_in"][1]).transpose(1, 0, 2))
    dh2 = mm_exchange(dp, w_in_p, "nt", F32, "d_h2", [p_wi])
    zeros_d = jnp.zeros_like(dh2)
    dx0, df1, grads["mix_norm"] = res_rms_bwd("res_rms_mix_bwd", 0.5, xs, f1, mix_norm, dx1, dh2, zeros_d)

    dh1a, dh1b = ffn_backward("ffn1", h1, g1, u1, act1, df1, W["ffn1_wg"], W["ffn1_wu"], W["ffn1_wd"])

    def rms1_bwd(a, dha, dhb, dxa, g):
        _, vjp = jax.vjp(f_rms, a, g)
        da_, dg_ = vjp(dha + dhb)
        return da_ + dxa, dg_

    grad_x, grads["ffn1_norm"] = rowwise("rms1_bwd", rms1_bwd, [_cols(xs), _cols(dh1a), _cols(dh1b), _cols(dx0)],
                                         [ffn1_norm], [(D, F32)], [(1, D)], TR)

    names = list(BIG)
    halves = [chip_sum("chip_sum_" + n, received[n]) for n in names]
    others = sibling_swap("sibling_join", halves)
    final_grads, delta, new_m, new_v = {}, {}, {}, {}
    for n, h, o in zip(names, halves, others):
        res = adamw_halves(n, wts[n][0], h, o, moms[n][0], vars_[n][0])
        final_grads[n], delta[n], new_m[n], new_v[n] = (a.reshape(wts[n].shape) for a in res)

    rep_flat = jnp.concatenate([grads[n].reshape(-1) for n in REPLICATED + LORA])
    rep_rows = -(-rep_flat.shape[0] // 1024) * 8
    rep_sum = allreduce_small(jnp.pad(rep_flat, (0, rep_rows * 128 - rep_flat.shape[0])).reshape(rep_rows, 128)).reshape(-1)
    my_chip = 2 * lax.axis_index("x") + lax.axis_index("y")
    off = 0
    for n in REPLICATED + LORA:
        size = grads[n].size
        full = rep_sum[off:off + size].reshape(grads[n].shape)
        off += size
        if n in LORA:
            cc = shard_shapes[n][1]
            full = lax.dynamic_slice_in_dim(full, my_chip * cc, cc, axis=1)
        final_grads[n] = full.reshape(wts[n].shape)

    loss = lax.psum(loss_acc[0, 0], ("x", "y", "c"))

    small = REPLICATED + LORA

    def two(a):
        return a.reshape(-1, a.shape[-1])

    ds, ms_, vs_ = adamw_small([two(wts[n]) for n in small], [two(final_grads[n]) for n in small],
                               [two(moms[n]) for n in small], [two(vars_[n]) for n in small])
    for i, n in enumerate(small):
        shp = wts[n].shape
        delta[n], new_m[n], new_v[n] = ds[i].reshape(shp), ms_[i].reshape(shp), vs_[i].reshape(shp)

    return (loss, grad_x.reshape(x.shape), *[final_grads[n] for n in WEIGHTS], *[delta[n] for n in WEIGHTS],
            *[new_m[n] for n in WEIGHTS], *[new_v[n] for n in WEIGHTS])
```

```python
import functools
import math

import jax
import jax.numpy as jnp
from jax import lax
from jax.experimental import pallas as pl
from jax.experimental.pallas import tpu as pltpu

F32 = jnp.float32
BF16 = jnp.bfloat16
MESH_IDS = pl.DeviceIdType.MESH

NORM_EPS = 1e-6
GN_EPS = 64e-5
GLA_TAU = 16.0
CHUNK = 64
GLA_DK = 128
GLA_DV = 256
RW_HD = 64
LORA_PAD = 128
GATE_LORA = 256
ADAM_LR, ADAM_B1, ADAM_B2, ADAM_EPS, ADAM_WD, ADAM_STEP = 0.001, 0.9, 0.999, 1e-08, 0.01, 10

VMEM_LIMIT_BYTES = 56 * 1024 * 1024
HBM_SPEC = pl.BlockSpec(memory_space=pltpu.HBM)


def _cparams(sem=None):
    return pltpu.CompilerParams(dimension_semantics=sem, vmem_limit_bytes=VMEM_LIMIT_BYTES)


def _pick(n, target, mult=128):
    best = None
    for t in range(mult, min(n, target) + 1, mult):
        if n % t == 0:
            best = t
    return best if best is not None else n


_NN = (((1,), (0,)), ((), ()))
_NT = (((1,), (1,)), ((), ()))
_TN = (((0,), (0,)), ((), ()))


def _dg(a, b, dims):
    return lax.dot_general(a.astype(BF16), b.astype(BF16), dims, preferred_element_type=F32)


@jax.custom_vjp
def mm_nn(a, b):
    return _dg(a, b, _NN)


def _mm_nn_fwd(a, b):
    return _dg(a, b, _NN), (a, b)


def _mm_nn_bwd(res, g):
    a, b = res
    return _dg(g, b, _NT).astype(a.dtype), _dg(a, g, _TN).astype(b.dtype)


mm_nn.defvjp(_mm_nn_fwd, _mm_nn_bwd)


@jax.custom_vjp
def mm_nt(a, b):
    return _dg(a, b, _NT)


def _mm_nt_fwd(a, b):
    return _dg(a, b, _NT), (a, b)


def _mm_nt_bwd(res, g):
    a, b = res
    return _dg(g, b, _NN).astype(a.dtype), _dg(g, a, _TN).astype(b.dtype)


mm_nt.defvjp(_mm_nt_fwd, _mm_nt_bwd)


@jax.custom_vjp
def mm_tn(a, b):
    return _dg(a, b, _TN)


def _mm_tn_fwd(a, b):
    return _dg(a, b, _TN), (a, b)


def _mm_tn_bwd(res, g):
    a, b = res
    return _dg(b, g, _NT).astype(a.dtype), _dg(a, g, _NN).astype(b.dtype)


mm_tn.defvjp(_mm_tn_fwd, _mm_tn_bwd)


def _split3(x):
    h = x.astype(BF16)
    r = x - h.astype(F32)
    m = r.astype(BF16)
    l = (r - m.astype(F32)).astype(BF16)
    return h, m, l


def _block_ones(n, seg):
    i = lax.broadcasted_iota(jnp.int32, (n, n), 0) // seg
    j = lax.broadcasted_iota(jnp.int32, (n, n), 1) // seg
    return (i == j).astype(BF16)


def _segsum_raw(x, seg, terms):
    ones = _block_ones(128, seg)
    outs = []
    for j in range(x.shape[1] // 128):
        t = x[:, j * 128:(j + 1) * 128]
        parts = _split3(t)[:terms]
        acc = jnp.dot(parts[0], ones, preferred_element_type=F32)
        for p_ in parts[1:]:
            acc = acc + jnp.dot(p_, ones, preferred_element_type=F32)
        outs.append(acc)
    return outs[0] if len(outs) == 1 else jnp.concatenate(outs, axis=1)


@jax.custom_vjp
def segsum64(x):
    return _segsum_raw(x, RW_HD, 3)


segsum64.defvjp(lambda x: (_segsum_raw(x, RW_HD, 3), None), lambda _, g: (_segsum_raw(g, RW_HD, 3),))


def _tri(n, upper):
    i = lax.broadcasted_iota(jnp.int32, (n, n), 0)
    j = lax.broadcasted_iota(jnp.int32, (n, n), 1)
    return ((i <= j) if upper else (i >= j)).astype(BF16)


def _tri_mm(x, upper):
    t = _tri(x.shape[0], upper)
    h, m, l = _split3(x)
    return (jnp.dot(t, h, preferred_element_type=F32) + jnp.dot(t, m, preferred_element_type=F32)
            + jnp.dot(t, l, preferred_element_type=F32))


@jax.custom_vjp
def cumsum_rows(x):
    return _tri_mm(x, False)


cumsum_rows.defvjp(lambda x: (_tri_mm(x, False), None), lambda _, g: (_tri_mm(g, True),))


def _make_split(sizes, axis):
    offs = [sum(sizes[:i]) for i in range(len(sizes))]

    def cut(x):
        if axis == 1:
            return tuple(x[:, o:o + s] for o, s in zip(offs, sizes))
        return tuple(x[o:o + s, :] for o, s in zip(offs, sizes))

    @jax.custom_vjp
    def split(x):
        return cut(x)

    split.defvjp(lambda x: (cut(x), None), lambda _, gs: (jnp.concatenate(gs, axis=axis),))
    return split


@jax.custom_vjp
def log_sigmoid(z):
    return jnp.minimum(z, 0.0) - jnp.log(1.0 + jnp.exp(-jnp.abs(z)))


log_sigmoid.defvjp(lambda z: (log_sigmoid(z), z), lambda z, g: (g * (1.0 - jax.nn.sigmoid(z)),))


def silu(x):
    return x * jax.nn.sigmoid(x)


def matmul(a, b, mode, out_dtype, name, tm=1024, tn=512, tk=2048, job=None, extras=(), epilogue=None):
    if mode == "nn":
        (M, K), (K2, N) = a.shape, b.shape
    elif mode == "nt":
        (M, K), (N, K2) = a.shape, b.shape
    else:
        (K, M), (K2, N) = a.shape, b.shape
    assert K == K2, (name, a.shape, b.shape)
    tm, tn, tk = _pick(M, tm), _pick(N, tn), _pick(K, tk)
    grid = (M // tm, N // tn, K // tk)
    dims = {"nn": _NN, "nt": _NT, "tn": _TN}[mode]
    a_spec = pl.BlockSpec((tk, tm), lambda i, j, k: (k, i)) if mode == "tn" else pl.BlockSpec((tm, tk), lambda i, j, k: (i, k))
    b_spec = pl.BlockSpec((tn, tk), lambda i, j, k: (j, k)) if mode == "nt" else pl.BlockSpec((tk, tn), lambda i, j, k: (k, j))
    n_in = 0 if job is None else len(job.operands)
    n_out = 0 if job is None else len(job.out_shapes)
    n_ex = len(extras)
    main_dtypes = [out_dtype] if epilogue is None else list(out_dtype)
    n_main = len(main_dtypes)

    def body(a_ref, b_ref, *rest):
        ex_refs, rest = rest[:n_ex], rest[n_ex:]
        job_ins, o_refs, job_outs = rest[:n_in], rest[n_in:n_in + n_main], rest[n_in + n_main:n_in + n_main + n_out]
        acc_ref, sems = rest[n_in + n_main + n_out], rest[n_in + n_main + n_out + 1:]
        i, j, k = pl.program_id(0), pl.program_id(1), pl.program_id(2)
        if job is not None:
            @pl.when((i == 0) & (j == 0) & (k == 0))
            def _():
                job.start(job_ins, job_outs, sems)

        part = _dg(a_ref[...], b_ref[...], dims)

        @pl.when(k == 0)
        def _():
            acc_ref[...] = part

        @pl.when(k > 0)
        def _():
            acc_ref[...] += part

        @pl.when(k == grid[2] - 1)
        def _():
            acc = acc_ref[...]
            vals = (acc,) if epilogue is None else epilogue(acc, *[e[...] for e in ex_refs])
            for o_ref, val in zip(o_refs, vals):
                o_ref[...] = val.astype(o_ref.dtype)

        if job is not None:
            @pl.when((i == grid[0] - 1) & (j == grid[1] - 1) & (k == grid[2] - 1))
            def _():
                job.finish(job_ins, job_outs, sems)

    main_spec = pl.BlockSpec((tm, tn), lambda i, j, k: (i, j))
    job_operands = [] if job is None else list(job.operands)
    res = pl.pallas_call(
        body, name=name, grid=grid, in_specs=[a_spec, b_spec] + [main_spec] * n_ex + [HBM_SPEC] * n_in,
        out_specs=[main_spec] * n_main + [HBM_SPEC] * n_out,
        out_shape=[jax.ShapeDtypeStruct((M, N), dt) for dt in main_dtypes] + ([] if job is None else list(job.out_shapes)),
        scratch_shapes=[pltpu.VMEM((tm, tn), F32)] + ([] if job is None else list(job.scratch)),
        compiler_params=_cparams(("parallel", "parallel", "arbitrary") if job is None else ("arbitrary",) * 3),
    )(a, b, *extras, *job_operands)
    mains = res[0] if epilogue is None else tuple(res[:n_main])
    return mains if job is None else (mains, res[n_main:])


def _cols(arr, width=None, off=0):
    width = arr.shape[1] if width is None else width
    assert off % width == 0, (off, width)
    return (arr, width, off // width)


def rowwise(name, fn, rows, consts, row_outs, acc_outs, tr, extra_specs=()):
    S = rows[0][0].shape[0]
    tr = min(tr, S)
    assert S % tr == 0
    n_in = len(rows) + len(extra_specs) + len(consts)
    n_ro = len(row_outs)
    in_specs = [pl.BlockSpec((tr, w), functools.partial(lambda i, cb: (i, cb), cb=cb)) for (_, w, cb) in rows]
    in_specs += [spec for (_, spec) in extra_specs]
    in_specs += [pl.BlockSpec(c.shape, lambda i: (0, 0)) for c in consts]
    out_shape = [jax.ShapeDtypeStruct((S, w), dt) for (w, dt) in row_outs]
    out_shape += [jax.ShapeDtypeStruct(shp, F32) for shp in acc_outs]
    out_specs = [pl.BlockSpec((tr, w), lambda i: (i, 0)) for (w, _) in row_outs]
    out_specs += [pl.BlockSpec(shp, lambda i: (0, 0)) for shp in acc_outs]

    def body(*refs):
        ins = [r[...] for r in refs[:n_in]]
        outs = fn(*ins)
        outs = outs if isinstance(outs, (tuple, list)) else (outs,)
        assert len(outs) == n_ro + len(acc_outs), (name, len(outs))
        for o_ref, val in zip(refs[n_in:n_in + n_ro], outs[:n_ro]):
            o_ref[...] = val.astype(o_ref.dtype)
        i = pl.program_id(0)
        for a_ref, val in zip(refs[n_in + n_ro:], outs[n_ro:]):
            @pl.when(i == 0)
            def _(a_ref=a_ref, val=val):
                a_ref[...] = val.astype(F32)

            @pl.when(i > 0)
            def _(a_ref=a_ref, val=val):
                a_ref[...] += val.astype(F32)

    res = pl.pallas_call(
        body, name=name, grid=(S // tr,), in_specs=in_specs, out_specs=out_specs, out_shape=out_shape,
        compiler_params=_cparams(("arbitrary",) if acc_outs else ("parallel",)),
    )(*[r[0] for r in rows], *[e[0] for e in extra_specs], *consts)
    return res


def f32(*xs):
    return [x.astype(F32) for x in xs]


def f_rms(x, g):
    return x * lax.rsqrt(jnp.mean(x * x, axis=-1, keepdims=True) + NORM_EPS) * g


def f_swiglu(gx, ux):
    return silu(gx) * ux


def f_merge(gp, yg, yr, gate_b, d):
    gates = jax.nn.sigmoid(gp + gate_b)
    g1, g2 = _make_split((d, d), 1)(gates)
    return g1 * yg + g2 * yr


def rw_window(rw):
    o = dict(rr=0, rk=rw, rv=2 * rw, gd=3 * rw, wd=3 * rw + GATE_LORA)
    o["ad"] = o["wd"] + LORA_PAD
    o["a"] = o["ad"] + LORA_PAD
    o["used"] = o["a"] + LORA_PAD
    return o


def f_rw_pre(xs, w0, w_w2, a0, w_a2, w_g2, k_k, k_a, rw):
    win = xs.shape[1]
    o = rw_window(rw)
    sizes = (rw, rw, rw, GATE_LORA, LORA_PAD, LORA_PAD, win - o["a"])
    rr, rk, rv, gd, wd, ad, _ = _make_split(sizes, 1)(xs)
    w_raw = w0 + mm_nn(jnp.tanh(wd), w_w2)
    dec = jnp.exp(-jnp.exp(log_sigmoid(w_raw) - 0.5))
    a = jax.nn.sigmoid(a0 + mm_nn(ad, w_a2))
    g = mm_nn(jax.nn.sigmoid(gd), w_g2)
    kx = rk * k_k
    kk = kx / jnp.maximum(jnp.sqrt(segsum64(kx * kx)), 1e-12)
    k2 = rk * (1.0 + (a - 1.0) * k_a)
    return rr, dec, k2, rv, kk, kk * a, g


def f_rw_post(y, r, k2, v, g, lnx_w, lnx_b, r_k):
    mu = segsum64(y) * (1.0 / RW_HD)
    yc = y - mu
    var = segsum64(yc * yc) * (1.0 / RW_HD)
    yn = yc * lax.rsqrt(var + GN_EPS) * lnx_w + lnx_b
    bonus = segsum64(r * k2 * r_k) * v
    return (yn + bonus) * g


def f_gla_chunk(q, k, v, r, a, st_prev, w_a2, b_a, gn_w, heads):
    z = mm_nn(a, w_a2) + b_a
    la = log_sigmoid(z) * (1.0 / GLA_TAU)
    cum = cumsum_rows(la)
    total = jnp.sum(la, axis=0, keepdims=True)
    kdec = k * jnp.exp(total - cum)
    et = jnp.exp(total)
    qs = q * (GLA_DK ** -0.5)
    sk = _make_split((GLA_DK,) * heads, 1)
    sv = _make_split((GLA_DV,) * heads, 1)
    ss = _make_split((GLA_DV,) * heads, 0)
    kd_h, q_h, et_h, v_h, st_h = sk(kdec), sk(qs), sk(et), sv(v), ss(st_prev)
    outs, news = [], []
    for h in range(heads):
        st_new = st_h[h] * et_h[h] + mm_tn(v_h[h], kd_h[h])
        o = mm_nt(q_h[h], st_new)
        o = o * lax.rsqrt(jnp.mean(o * o, axis=-1, keepdims=True) + NORM_EPS) * gn_w
        outs.append(o)
        news.append(st_new)
    o_all = outs[0] if heads == 1 else jnp.concatenate(outs, axis=1)
    st_all = news[0] if heads == 1 else jnp.concatenate(news, axis=0)
    return o_all * silu(r), st_all


def gla_forward(p, lay, w_a2, b_a, gn_w, heads):
    S = p.shape[0]
    nc = S // CHUNK
    gq, gv = heads * GLA_DK, heads * GLA_DV

    def spec(width, off, rev=False):
        assert off % width == 0
        return pl.BlockSpec((CHUNK, width), functools.partial(lambda n, cb: (n, cb), cb=off // width))

    def body(q_ref, k_ref, v_ref, r_ref, a_ref, w_ref, b_ref, g_ref, o_ref, st_out_ref, st_sc):
        @pl.when(pl.program_id(0) == 0)
        def _():
            st_sc[...] = jnp.zeros_like(st_sc)

        st_prev = st_sc[...]
        st_out_ref[0] = st_prev
        o, st_new = f_gla_chunk(*f32(q_ref[...], k_ref[...], v_ref[...], r_ref[...], a_ref[...]), st_prev,
                                w_ref[...], b_ref[...], g_ref[...], heads)
        o_ref[...] = o.astype(o_ref.dtype)
        st_sc[...] = st_new

    return pl.pallas_call(
        body, name="gla_fwd", grid=(nc,),
        in_specs=[spec(gq, lay["q"]), spec(gq, lay["k"]), spec(gv, lay["v"]), spec(gv, lay["r"]), spec(LORA_PAD, lay["a"]),
                  pl.BlockSpec(w_a2.shape, lambda n: (0, 0)), pl.BlockSpec(b_a.shape, lambda n: (0, 0)),
                  pl.BlockSpec(gn_w.shape, lambda n: (0, 0))],
        out_specs=[pl.BlockSpec((CHUNK, gv), lambda n: (n, 0)), pl.BlockSpec((1, gv, GLA_DK), lambda n: (n, 0, 0))],
        out_shape=[jax.ShapeDtypeStruct((S, gv), BF16), jax.ShapeDtypeStruct((nc, gv, GLA_DK), F32)],
        scratch_shapes=[pltpu.VMEM((gv, GLA_DK), F32)],
        compiler_params=_cparams(("arbitrary",)),
    )(p, p, p, p, p, w_a2, b_a, gn_w)


def gla_backward(p, lay, states, d_out, w_a2, b_a, gn_w, heads):
    S = p.shape[0]
    nc = S // CHUNK
    gq, gv = heads * GLA_DK, heads * GLA_DV

    def spec(width, off):
        assert off % width == 0
        return pl.BlockSpec((CHUNK, width), functools.partial(lambda n, cb: (nc - 1 - n, cb), cb=off // width))

    def rev(width):
        return pl.BlockSpec((CHUNK, width), lambda n: (nc - 1 - n, 0))

    def whole(arr):
        return pl.BlockSpec(arr.shape, lambda n: (0, 0))

    def body(q_ref, k_ref, v_ref, r_ref, a_ref, st_ref, do_ref, w_ref, b_ref, g_ref,
             dq_ref, dk_ref, dv_ref, dr_ref, da_ref, dw_ref, db_ref, dg_ref, dst_sc):
        n = pl.program_id(0)

        @pl.when(n == 0)
        def _():
            dst_sc[...] = jnp.zeros_like(dst_sc)

        fn = functools.partial(f_gla_chunk, heads=heads)
        prim = (*f32(q_ref[...], k_ref[...], v_ref[...], r_ref[...], a_ref[...]), st_ref[0],
                w_ref[...].astype(F32), b_ref[...], g_ref[...])
        _, vjp = jax.vjp(fn, *prim)
        dq, dk, dv, dr, da, dst, dw, db, dg = vjp((do_ref[...].astype(F32), dst_sc[...]))
        for ref, val in ((dq_ref, dq), (dk_ref, dk), (dv_ref, dv), (dr_ref, dr), (da_ref, da)):
            ref[...] = val.astype(ref.dtype)
        dst_sc[...] = dst

        @pl.when(n == 0)
        def _():
            dw_ref[...] = dw
            db_ref[...] = db
            dg_ref[...] = dg

        @pl.when(n > 0)
        def _():
            dw_ref[...] += dw
            db_ref[...] += db
            dg_ref[...] += dg

    return pl.pallas_call(
        body, name="gla_bwd", grid=(nc,),
        in_specs=[spec(gq, lay["q"]), spec(gq, lay["k"]), spec(gv, lay["v"]), spec(gv, lay["r"]), spec(LORA_PAD, lay["a"]),
                  pl.BlockSpec((1, gv, GLA_DK), lambda n: (nc - 1 - n, 0, 0)), rev(gv),
                  whole(w_a2), whole(b_a), whole(gn_w)],
        out_specs=[rev(gq), rev(gq), rev(gv), rev(gv), rev(LORA_PAD), whole(w_a2), whole(b_a), whole(gn_w)],
        out_shape=[jax.ShapeDtypeStruct((S, gq), BF16), jax.ShapeDtypeStruct((S, gq), BF16),
                   jax.ShapeDtypeStruct((S, gv), BF16), jax.ShapeDtypeStruct((S, gv), BF16),
                   jax.ShapeDtypeStruct((S, LORA_PAD), F32),
                   jax.ShapeDtypeStruct(w_a2.shape, F32), jax.ShapeDtypeStruct(b_a.shape, F32),
                   jax.ShapeDtypeStruct(gn_w.shape, F32)],
        scratch_shapes=[pltpu.VMEM((gv, GLA_DK), F32)],
        compiler_params=_cparams(("arbitrary",)),
    )(p, p, p, p, p, states, d_out, w_a2, b_a, gn_w)


SCAN_BLOCK = 32


def _scan_helpers(nt):
    ones = _block_ones(128, RW_HD)
    rows = lax.broadcasted_iota(jnp.int32, (nt * RW_HD, 128), 0) % RW_HD
    lanes = lax.broadcasted_iota(jnp.int32, (nt * RW_HD, 128), 1) % RW_HD
    eye = rows == lanes

    def bc(ref, t):
        parts = [jnp.broadcast_to(ref[t, j:j + 1, :], (RW_HD, 128)) for j in range(nt)]
        return parts[0] if nt == 1 else jnp.concatenate(parts, axis=0)

    def seg1(x):
        return jnp.dot(x.astype(BF16), ones, preferred_element_type=F32)

    def column(ref, t):
        return seg1(jnp.where(eye, bc(ref, t), 0.0))

    def put_diag(ref, t, x):
        put_colsum(ref, t, jnp.where(eye, x, 0.0))

    def put_colsum(ref, t, x, sign=1.0):
        for j in range(nt):
            ref[t, j:j + 1, :] = sign * jnp.sum(x[j * RW_HD:(j + 1) * RW_HD, :], axis=0, keepdims=True)

    return bc, seg1, column, put_diag, put_colsum


def rwkv_scan_forward(r, w, k2, v, kk, b):
    S, nt, _ = r.shape
    tb = min(SCAN_BLOCK, S)

    def body(r_ref, w_ref, k2_ref, v_ref, kk_ref, b_ref, y_ref, st_ref, s_sc):
        @pl.when(pl.program_id(0) == 0)
        def _():
            s_sc[...] = jnp.zeros_like(s_sc)

        bc, seg1, column, put_diag, put_colsum = _scan_helpers(nt)

        def step(t, carry):
            s = s_sc[...]
            sa_e = seg1(s * bc(kk_ref, t))
            s = s * bc(w_ref, t) - sa_e * bc(b_ref, t) + column(v_ref, t) * bc(k2_ref, t)
            s_sc[...] = s
            st_ref[t] = s
            put_diag(y_ref, t, seg1(s * bc(r_ref, t)))
            return carry

        lax.fori_loop(0, tb, step, 0, unroll=8)

    row = pl.BlockSpec((tb, nt, 128), lambda i: (i, 0, 0))
    return pl.pallas_call(
        body, name="rwkv_scan_fwd", grid=(S // tb,),
        in_specs=[row] * 6,
        out_specs=[row, pl.BlockSpec((tb, nt * RW_HD, 128), lambda i: (i, 0, 0))],
        out_shape=[jax.ShapeDtypeStruct((S, nt, 128), F32), jax.ShapeDtypeStruct((S, nt * RW_HD, 128), F32)],
        scratch_shapes=[pltpu.VMEM((nt * RW_HD, 128), F32)],
        compiler_params=_cparams(("arbitrary",)),
    )(r, w, k2, v, kk, b)


def rwkv_scan_backward(r, w, k2, v, kk, b, states, dy):
    S, nt, _ = r.shape
    tb = min(SCAN_BLOCK, S)
    nb = S // tb

    def body(r_ref, w_ref, k2_ref, v_ref, kk_ref, b_ref, st_ref, edge_ref, dy_ref,
             dr_ref, dw_ref, dk2_ref, dv_ref, dkk_ref, db_ref, ds_sc, before_sc):
        @pl.when(pl.program_id(0) == 0)
        def _():
            ds_sc[...] = jnp.zeros_like(ds_sc)

        before_sc[...] = jnp.where(pl.program_id(0) == nb - 1, 0.0, edge_ref[0])

        bc, seg1, column, put_diag, put_colsum = _scan_helpers(nt)

        def step(i, carry):
            t = tb - 1 - i
            s_prev = jnp.where(t == 0, before_sc[...], st_ref[jnp.maximum(t - 1, 0)])
            r_e, w_e, k2_e, kk_e, b_e = (bc(ref, t) for ref in (r_ref, w_ref, k2_ref, kk_ref, b_ref))
            v_e = column(v_ref, t)
            sa_e = seg1(s_prev * kk_e)
            dy_e = column(dy_ref, t)
            put_colsum(dr_ref, t, st_ref[t] * dy_e)
            ds = ds_sc[...] + dy_e * r_e
            put_colsum(dw_ref, t, ds * s_prev)
            nsa_e = seg1(ds * b_e)
            put_colsum(db_ref, t, ds * sa_e, -1.0)
            put_diag(dv_ref, t, seg1(ds * k2_e))
            put_colsum(dk2_ref, t, ds * v_e)
            put_colsum(dkk_ref, t, s_prev * nsa_e, -1.0)
            ds_sc[...] = ds * w_e - nsa_e * kk_e
            return carry

        lax.fori_loop(0, tb, step, 0, unroll=8)

    row = pl.BlockSpec((tb, nt, 128), lambda i: (nb - 1 - i, 0, 0))
    return pl.pallas_call(
        body, name="rwkv_scan_bwd", grid=(nb,),
        in_specs=[row] * 6 + [pl.BlockSpec((tb, nt * RW_HD, 128), lambda i: (nb - 1 - i, 0, 0)),
                              pl.BlockSpec((1, nt * RW_HD, 128), lambda i: (jnp.maximum((nb - 1 - i) * tb - 1, 0), 0, 0)), row],
        out_specs=[row] * 6,
        out_shape=[jax.ShapeDtypeStruct((S, nt, 128), F32)] * 6,
        scratch_shapes=[pltpu.VMEM((nt * RW_HD, 128), F32), pltpu.VMEM((nt * RW_HD, 128), F32)],
        compiler_params=_cparams(("arbitrary",)),
    )(r, w, k2, v, kk, b, states, states, dy)


def _edge_spec(width, col_block, tr, n_rows, after):
    last = n_rows // 8 - 1
    if after:
        return pl.BlockSpec((8, width), lambda i: (jnp.minimum((i + 1) * (tr // 8), last), col_block))
    return pl.BlockSpec((8, width), lambda i: (jnp.maximum(i * (tr // 8) - 1, 0), col_block))


def _shifted_prev(p, prev8):
    first = jnp.where(pl.program_id(0) == 0, 0.0, prev8[7:8, :])
    rows = lax.broadcasted_iota(jnp.int32, p.shape, 0)
    return jnp.where(rows == 0, first, pltpu.roll(p, 1, axis=0))


def token_shift_forward(p, mu_win, win, tr):
    def fn(pw, prev8, mu):
        return pw + mu * (_shifted_prev(pw, prev8) - pw)

    return rowwise("token_shift_fwd", fn, [_cols(p, win, 0)], [mu_win], [(win, F32)], [], tr,
                   extra_specs=[(p, _edge_spec(win, 0, tr, p.shape[0], False))])[0]


def token_shift_backward(p, dxs, da_gla, mu_win, win, a_off, tr):
    S = p.shape[0]
    n = S // min(tr, S)

    def fn(pw, dx, da, prev8, next8, mu):
        trr = pw.shape[0]
        last = jnp.where(pl.program_id(0) == n - 1, 0.0, next8[0:1, :])
        rows = lax.broadcasted_iota(jnp.int32, dx.shape, 0)
        dnext = jnp.where(rows == trr - 1, last, pltpu.roll(dx, trr - 1, axis=0))
        dp = (1.0 - mu) * dx + mu * dnext
        dp = jnp.concatenate([dp[:, :a_off], dp[:, a_off:a_off + LORA_PAD] + da, dp[:, a_off + LORA_PAD:]], axis=1)
        dmu = jnp.sum(dx * (_shifted_prev(pw, prev8) - pw), axis=0, keepdims=True)
        return dp, dmu

    return rowwise("token_shift_bwd", fn, [_cols(p, win, 0), _cols(dxs), _cols(da_gla)], [mu_win],
                   [(win, BF16)], [(1, win)], tr,
                   extra_specs=[(p, _edge_spec(win, 0, tr, S, False)), (dxs, _edge_spec(win, 0, tr, S, True))])


def _adamw_math(w, g, m, v):
    m = ADAM_B1 * m + (1.0 - ADAM_B1) * g
    v = ADAM_B2 * v + (1.0 - ADAM_B2) * (g * g)
    m_hat = m / (1.0 - ADAM_B1 ** ADAM_STEP)
    v_hat = v / (1.0 - ADAM_B2 ** ADAM_STEP)
    delta = -ADAM_LR * (m_hat / (jnp.sqrt(v_hat) + ADAM_EPS) + ADAM_WD * w)
    return delta, m, v


def adamw_small(ws, gs, ms, vs):
    n = len(ws)

    def body(*refs):
        for i in range(n):
            d, m, v = _adamw_math(refs[i][...], refs[n + i][...], refs[2 * n + i][...], refs[3 * n + i][...])
            refs[4 * n + i][...] = d
            refs[5 * n + i][...] = m
            refs[6 * n + i][...] = v

    shapes = [jax.ShapeDtypeStruct(w.shape, F32) for w in ws]
    outs = pl.pallas_call(body, name="adamw_small", out_shape=shapes * 3, compiler_params=_cparams())(*ws, *gs, *ms, *vs)
    return outs[:n], outs[n:2 * n], outs[2 * n:]


def _place():
    x, y, c = lax.axis_index("x"), lax.axis_index("y"), lax.axis_index("c")
    chips = [(1 - x, y), (x, 1 - y), (1 - x, 1 - y)]
    return x, y, c, chips


def _full_shape(kind, r, c):
    return {"col": (r, 4 * c), "row": (4 * r, c), "slab": (4, r, c)}[kind]


def _slab(ref, kind, k, r, c, half=None):
    n, off = (r, 0) if half is None else (r // 2, half * (r // 2))
    if kind == "col":
        return ref.at[pl.ds(off, n), pl.ds(k * c, c)]
    if kind == "row":
        return ref.at[pl.ds(k * r + off, n), :]
    return ref.at[k, pl.ds(off, n), :]


def _remote(src, dst, sems, idx, to):
    return pltpu.make_async_remote_copy(src_ref=src, dst_ref=dst, send_sem=sems[0].at[idx], recv_sem=sems[1].at[idx],
                                        device_id=to, device_id_type=MESH_IDS)


class CommJob:
    def __init__(self, operands, out_shapes, scratch, start, finish):
        self.operands, self.out_shapes, self.scratch, self.start, self.finish = operands, out_shapes, scratch, start, finish


def gather_job(big, small=()):
    big, small = list(big), list(small)
    nb, ns = len(big), len(small)
    meta = [(kind, *a.shape) for a, kind in big + small]

    def sends(srcs, outs, sems):
        own_s, own_r, ici_s, ici_r, _, _, sm_s, sm_r = sems
        x, y, c, chips = _place()
        me, sib = 2 * x + y, (x, y, 1 - c)
        cps = []
        for a in range(nb):
            kind, r, cc = meta[a]
            for j, chip in enumerate(chips):
                cps.append(_remote(srcs[a].at[pl.ds(c * (r // 2), r // 2)], _slab(outs[a], kind, me, r, cc, c),
                                   (ici_s, ici_r), (a, j), (*chip, c)))
        for a in range(nb):
            kind, r, cc = meta[a]
            cps.append(_remote(srcs[a], _slab(outs[a], kind, me, r, cc), (own_s, own_r), (a,), sib))
        for s in range(ns):
            kind, r, cc = meta[nb + s]
            for t, to in enumerate([sib] + [(*chip, c) for chip in chips]):
                cps.append(_remote(srcs[nb + s], _slab(outs[nb + s], kind, me, r, cc), (sm_s, sm_r), (s, t), to))
        return cps

    def start(srcs, outs, sems):
        for cp in sends(srcs, outs, sems):
            cp.start()

    def finish(srcs, outs, sems):
        own_s, own_r, ici_s, ici_r, fwd_s, fwd_r, sm_s, sm_r = sems
        x, y, c, chips = _place()
        me, sib = 2 * x + y, (x, y, 1 - c)
        cids = [2 * chip[0] + chip[1] for chip in chips]
        hands = []
        for a in range(nb):
            kind, r, cc = meta[a]
            for j in range(3):
                blk = _slab(outs[a], kind, cids[j], r, cc, c)
                _remote(blk, blk, (ici_s, ici_r), (a, j), sib).wait_recv()
                hands.append(_remote(blk, blk, (fwd_s, fwd_r), (a, j), sib))
                hands[-1].start()
        for a in range(nb):
            kind, r, cc = meta[a]
            for j in range(3):
                blk = _slab(outs[a], kind, cids[j], r, cc, 1 - c)
                _remote(blk, blk, (fwd_s, fwd_r), (a, j), sib).wait_recv()
            blk = _slab(outs[a], kind, me, r, cc)
            _remote(blk, blk, (own_s, own_r), (a,), sib).wait_recv()
        for s in range(ns):
            kind, r, cc = meta[nb + s]
            for t, frm in enumerate([me] + cids):
                blk = _slab(outs[nb + s], kind, frm, r, cc)
                _remote(blk, blk, (sm_s, sm_r), (s, t), sib).wait_recv()
        for cp in sends(srcs, outs, sems) + hands:
            cp.wait_send()

    dma = pltpu.SemaphoreType.DMA
    nb1, ns1 = max(nb, 1), max(ns, 1)
    return CommJob([a for a, _ in big + small],
                   [jax.ShapeDtypeStruct(_full_shape(kind, r, cc), BF16) for (kind, r, cc) in meta],
                   [dma((nb1,)), dma((nb1,)), dma((nb1, 3)), dma((nb1, 3)), dma((nb1, 3)), dma((nb1, 3)),
                    dma((ns1, 4)), dma((ns1, 4))], start, finish)


def run_job(name, job):
    n_in, n_out = len(job.operands), len(job.out_shapes)

    def body(*refs):
        ins, outs, sems = refs[:n_in], refs[n_in:n_in + n_out], refs[n_in + n_out:]
        job.start(ins, outs, sems)
        job.finish(ins, outs, sems)

    return pl.pallas_call(body, name=name, in_specs=[HBM_SPEC] * n_in, out_specs=[HBM_SPEC] * n_out,
                          out_shape=job.out_shapes, scratch_shapes=job.scratch)(*job.operands)


def sibling_swap(name, arrays):
    n = len(arrays)

    def body(*refs):
        srcs, outs, sems = refs[:n], refs[n:2 * n], refs[2 * n:]
        x, y, c, _ = _place()
        cps = [_remote(srcs[a], outs[a], sems, (a,), (x, y, 1 - c)) for a in range(n)]
        for cp in cps:
            cp.start()
        for cp in cps:
            cp.wait_recv()
        for cp in cps:
            cp.wait_send()

    return pl.pallas_call(
        body, name=name, in_specs=[HBM_SPEC] * n, out_specs=[HBM_SPEC] * n,
        out_shape=[jax.ShapeDtypeStruct(a.shape, a.dtype) for a in arrays],
        scratch_shapes=[pltpu.SemaphoreType.DMA((n,)), pltpu.SemaphoreType.DMA((n,))],
    )(*arrays)


def exchange_job(sums):
    n = len(sums)

    def sends(srcs, outs, sems):
        ici_s, ici_r, sib_s, sib_r = sems
        x, y, c, chips = _place()
        me, sib = 2 * x + y, (x, y, 1 - c)
        cps = []
        for a, (_, kind, r, cc) in enumerate(sums):
            for j, chip in enumerate(chips):
                cid = 2 * chip[0] + chip[1]
                cps.append(_remote(_slab(srcs[a], kind, cid, r, cc, c), outs[a].at[me], (ici_s, ici_r), (a, j), (*chip, c)))
            cps.append(_remote(_slab(srcs[a], kind, me, r, cc, 1 - c), outs[a].at[me], (sib_s, sib_r), (a,), sib))
        return cps

    def start(srcs, outs, sems):
        for cp in sends(srcs, outs, sems):
            cp.start()

    def finish(srcs, outs, sems):
        ici_s, ici_r, sib_s, sib_r = sems
        x, y, c, chips = _place()
        me, sib = 2 * x + y, (x, y, 1 - c)
        for a in range(n):
            for j, chip in enumerate(chips):
                blk = outs[a].at[2 * chip[0] + chip[1]]
                _remote(blk, blk, (ici_s, ici_r), (a, j), sib).wait_recv()
            _remote(outs[a].at[me], outs[a].at[me], (sib_s, sib_r), (a,), sib).wait_recv()
        for cp in sends(srcs, outs, sems):
            cp.wait_send()

    dma = pltpu.SemaphoreType.DMA
    return CommJob([s[0] for s in sums], [jax.ShapeDtypeStruct((4, r // 2, cc), BF16) for (_, _, r, cc) in sums],
                   [dma((n, 3)), dma((n, 3)), dma((n,)), dma((n,))], start, finish)


def allreduce_small(vec):
    R, C = vec.shape

    def body(src, out, gathered, send_sems, recv_sems):
        x, y, c, _ = _place()
        me = 4 * x + 2 * y + c
        gathered[me] = src[...]
        peers = [(fx, fy, fc) for fx in (0, 1) for fy in (0, 1) for fc in (0, 1) if (fx, fy, fc) != (0, 0, 0)]
        sends = []
        for j, (fx, fy, fc) in enumerate(peers):
            to = (x ^ fx, y ^ fy, c ^ fc)
            cp = pltpu.make_async_remote_copy(
                src_ref=src, dst_ref=gathered.at[me], send_sem=send_sems.at[j], recv_sem=recv_sems.at[j],
                device_id=to, device_id_type=MESH_IDS)
            cp.start()
            sends.append(cp)
        for j, (fx, fy, fc) in enumerate(peers):
            frm = 4 * (x ^ fx) + 2 * (y ^ fy) + (c ^ fc)
            pltpu.make_async_remote_copy(
                src_ref=src, dst_ref=gathered.at[frm], send_sem=send_sems.at[j], recv_sem=recv_sems.at[j],
                device_id=(x, y, c), device_id_type=MESH_IDS).wait_recv()
        for cp in sends:
            cp.wait_send()
        acc = gathered[0]
        for k in range(1, 8):
            acc = acc + gathered[k]
        out[...] = acc

    vm = pl.BlockSpec(memory_space=pltpu.VMEM)
    return pl.pallas_call(
        body, name="allreduce_small", in_specs=[vm], out_specs=vm,
        out_shape=jax.ShapeDtypeStruct((R, C), F32),
        scratch_shapes=[pltpu.VMEM((8, R, C), F32), pltpu.SemaphoreType.DMA((7,)), pltpu.SemaphoreType.DMA((7,))],
        compiler_params=_cparams(),
    )(vec)


def pair_sum(name, mine, theirs):
    rows, cols = mine.shape
    tr = _pick(rows, max(16, (1 << 20) // cols), 16)
    return rowwise(name, lambda a, b: a.astype(F32) + b.astype(F32), [_cols(mine), _cols(theirs)], [], [(cols, BF16)], [], tr)[0]


def chip_sum(name, rb):
    _, rh, C = rb.shape
    tr = _pick(rh, max(16, (1 << 19) // C), 16)

    def body(r_ref, o_ref):
        acc = r_ref[0].astype(F32)
        for k in range(1, 4):
            acc = acc + r_ref[k].astype(F32)
        o_ref[...] = acc

    return pl.pallas_call(body, name=name, grid=(rh // tr,),
                          in_specs=[pl.BlockSpec((4, tr, C), lambda i: (0, i, 0))],
                          out_specs=pl.BlockSpec((tr, C), lambda i: (i, 0)),
                          out_shape=jax.ShapeDtypeStruct((rh, C), F32),
                          compiler_params=_cparams(("parallel",)))(rb)


def adamw_halves(name, w, mine, theirs, m, v):
    rows, cols = w.shape
    tr = _pick(rows // 2, max(8, (1 << 19) // cols), 8)
    nbh = rows // 2 // tr
    full = pl.BlockSpec((tr, cols), lambda i: (i, 0))
    half = pl.BlockSpec((tr, cols), lambda i: (i % nbh, 0))

    def body(w_ref, a_ref, b_ref, m_ref, v_ref, g_out, d_out, m_out, v_out):
        is_mine = (pl.program_id(0) // nbh) == lax.axis_index("c")
        g = jnp.where(is_mine, a_ref[...], b_ref[...])
        d, mn, vn = _adamw_math(w_ref[...], g, m_ref[...], v_ref[...])
        g_out[...] = g
        d_out[...] = d
        m_out[...] = mn
        v_out[...] = vn

    return pl.pallas_call(body, name="adamw_" + name, grid=(rows // tr,), in_specs=[full, half, half, full, full],
                          out_specs=[full] * 4, out_shape=[jax.ShapeDtypeStruct((rows, cols), F32)] * 4,
                          compiler_params=_cparams(("parallel",)))(w, mine, theirs, m, v)


BIG = {"ffn1_wg": "col", "ffn1_wu": "col", "ffn1_wd": "row", "w_in": "slab", "w_branch": "row", "w_out": "row",
       "ffn2_wg": "col", "ffn2_wu": "col", "ffn2_wd": "row"}
LORA = ["gla_w_a2", "rwkv_w_w2", "rwkv_w_a2", "rwkv_w_g2"]
REPLICATED = ["ffn1_norm", "mix_norm", "gla_b_a", "gla_gn_w", "rwkv_mu", "rwkv_w0", "rwkv_a0", "rwkv_k_k", "rwkv_k_a",
              "rwkv_r_k", "rwkv_lnx_w", "rwkv_lnx_b", "gate_b", "ffn2_norm", "final_norm"]
WEIGHTS = ["ffn1_norm", "ffn1_wg", "ffn1_wu", "ffn1_wd", "mix_norm", "w_in", "gla_w_a2", "gla_b_a", "gla_gn_w", "rwkv_mu",
           "rwkv_w0", "rwkv_w_w2", "rwkv_a0", "rwkv_w_a2", "rwkv_w_g2", "rwkv_k_k", "rwkv_k_a", "rwkv_r_k", "rwkv_lnx_w",
           "rwkv_lnx_b", "gate_b", "w_branch", "w_out", "ffn2_norm", "ffn2_wg", "ffn2_wu", "ffn2_wd", "final_norm"]


def kernel(x, ffn1_norm, ffn1_wg, ffn1_wu, ffn1_wd, mix_norm, w_in, gla_w_a2, gla_b_a, gla_gn_w, rwkv_mu, rwkv_w0, rwkv_w_w2, rwkv_a0, rwkv_w_a2, rwkv_w_g2, rwkv_k_k, rwkv_k_a, rwkv_r_k, rwkv_lnx_w, rwkv_lnx_b, gate_b, w_branch, w_out, ffn2_norm, ffn2_wg, ffn2_wu, ffn2_wd, final_norm, loss_target, m_ffn1_norm, m_ffn1_wg, m_ffn1_wu, m_ffn1_wd, m_mix_norm, m_w_in, m_gla_w_a2, m_gla_b_a, m_gla_gn_w, m_rwkv_mu, m_rwkv_w0, m_rwkv_w_w2, m_rwkv_a0, m_rwkv_w_a2, m_rwkv_w_g2, m_rwkv_k_k, m_rwkv_k_a, m_rwkv_r_k, m_rwkv_lnx_w, m_rwkv_lnx_b, m_gate_b, m_w_branch, m_w_out, m_ffn2_norm, m_ffn2_wg, m_ffn2_wu, m_ffn2_wd, m_final_norm, v_ffn1_norm, v_ffn1_wg, v_ffn1_wu, v_ffn1_wd, v_mix_norm, v_w_in, v_gla_w_a2, v_gla_b_a, v_gla_gn_w, v_rwkv_mu, v_rwkv_w0, v_rwkv_w_w2, v_rwkv_a0, v_rwkv_w_a2, v_rwkv_w_g2, v_rwkv_k_k, v_rwkv_k_a, v_rwkv_r_k, v_rwkv_lnx_w, v_rwkv_lnx_b, v_gate_b, v_w_branch, v_w_out, v_ffn2_norm, v_ffn2_wg, v_ffn2_wu, v_ffn2_wd, v_final_norm):
    args = dict(locals())
    wts = {n: args[n] for n in WEIGHTS}
    moms = {n: args["m_" + n] for n in WEIGHTS}
    vars_ = {n: args["v_" + n] for n in WEIGHTS}

    xs = x[0]
    tgt = loss_target[0]
    S, D = xs.shape
    FF = ffn1_wd.shape[1] * 4
    gheads = gla_b_a.shape[-1] // GLA_DK
    GQ, GV = gheads * GLA_DK, gheads * GLA_DV
    rheads = rwkv_r_k.shape[1]
    RW = rheads * RW_HD
    NT = RW // 128
    lo_g = gla_w_a2.shape[1]
    lo_w = rwkv_w_w2.shape[1]
    lo_a = rwkv_w_a2.shape[1]
    assert rwkv_w_g2.shape[1] == GATE_LORA and RW % 128 == 0

    ow = rw_window(RW)
    WIN = -(-ow["used"] // (2 * D)) * (2 * D)
    lay = dict(gate=WIN, v=WIN + 2 * D, r=WIN + 2 * D + GV, q=WIN + 2 * D + 2 * GV, k=WIN + 2 * D + 2 * GV + GQ, a=ow["a"])
    DP = lay["k"] + GQ
    DIN = w_in.shape[-1] * 4
    o_sizes = [GQ, GQ, GV, GV, lo_g, RW, RW, RW, lo_w, lo_a, GATE_LORA, 2 * D]
    o_offs = [sum(o_sizes[:i]) for i in range(len(o_sizes))]
    assert o_offs[-1] + o_sizes[-1] == DIN
    p_offs = [lay["q"], lay["k"], lay["v"], lay["r"], ow["a"], ow["rr"], ow["rk"], ow["rv"], ow["wd"], ow["ad"], ow["gd"], lay["gate"]]

    def to_padded(w):
        order = sorted(range(len(o_sizes)), key=lambda i: p_offs[i])
        parts, pos = [], 0
        for i in order:
            if p_offs[i] > pos:
                parts.append(jnp.zeros((w.shape[0], p_offs[i] - pos), w.dtype))
            parts.append(w[:, o_offs[i]:o_offs[i] + o_sizes[i]])
            pos = p_offs[i] + o_sizes[i]
        if pos < DP:
            parts.append(jnp.zeros((w.shape[0], DP - pos), w.dtype))
        return jnp.concatenate(parts, axis=1)

    def from_padded(w):
        return jnp.concatenate([w[:, p_offs[i]:p_offs[i] + o_sizes[i]] for i in range(len(o_sizes))], axis=1)

    def pad_rows(w, rows):
        return jnp.pad(w, ((0, rows - w.shape[0]), (0, 0)))

    mu = rwkv_mu[0]
    mu_parts = {"rr": mu[0:RW], "rk": mu[RW:2 * RW], "rv": mu[2 * RW:3 * RW], "wd": mu[3 * RW:3 * RW + lo_w],
                "ad": mu[3 * RW + lo_w:3 * RW + lo_w + lo_a], "gd": mu[3 * RW + lo_w + lo_a:]}
    mu_win = jnp.zeros((WIN,), F32)
    for key, val in mu_parts.items():
        mu_win = lax.dynamic_update_slice(mu_win, val, (ow[key],))
    mu_win = mu_win.reshape(1, WIN)

    shard_shapes = {n: wts[n].shape[1:] for n in list(BIG) + LORA}
    W = {}

    def shard(n):
        return (wts[n][0].astype(BF16), BIG[n])

    def mm_gather(a, b, out_dtype, name, gather, lora=(), **epilogue):
        out, got = matmul(a, b, "nn", out_dtype, name, **epilogue,
                          job=gather_job([shard(n) for n in gather], [(wts[n][0].astype(BF16), "col") for n in lora]))
        W.update(zip(list(gather) + list(lora), got))
        return out

    W["ffn1_wg"] = run_job("gather_ffn1_wg", gather_job([shard("ffn1_wg")]))[0]
    r_k = rwkv_r_k.reshape(1, RW)
    fin_g = final_norm.reshape(1, D)

    TR = min(128, S)
    def up_and_act(acc, g):
        return acc, f_swiglu(g.astype(F32), acc)

    h1 = rowwise("rms1", lambda a, g: f_rms(a, g), [_cols(xs)], [ffn1_norm], [(D, BF16)], [], TR)[0]
    g1 = mm_gather(h1, W["ffn1_wg"], BF16, "ffn1_g", ["ffn1_wu"])
    u1, act1 = mm_gather(h1, W["ffn1_wu"], [BF16, BF16], "ffn1_u", ["ffn1_wd"], extras=[g1], epilogue=up_and_act)
    f1 = mm_gather(act1, W["ffn1_wd"], F32, "ffn1_d", ["w_in"], LORA)
    w_in_p = to_padded(W["w_in"].transpose(1, 0, 2).reshape(D, DIN))
    gla_a2_p = pad_rows(W["gla_w_a2"], LORA_PAD)
    w_w2_p = pad_rows(W["rwkv_w_w2"], LORA_PAD)
    w_a2_p = pad_rows(W["rwkv_w_a2"], LORA_PAD)
    w_g2 = W["rwkv_w_g2"]

    def res_rms(coef):
        def fn(a, f, g):
            x1 = a + coef * f
            return x1, f_rms(x1, g)
        return fn

    x1, h2 = rowwise("res_rms_mix", res_rms(0.5), [_cols(xs), _cols(f1)], [mix_norm], [(D, F32), (D, BF16)], [], TR)
    p = mm_gather(h2, w_in_p, F32, "w_in", ["w_branch", "w_out", "ffn2_wg"])
    wb_g, wb_r = W["w_branch"][:GV], W["w_branch"][GV:]

    o_gla, gla_states = gla_forward(p, lay, gla_a2_p, gla_b_a, gla_gn_w, gheads)
    xsh = token_shift_forward(p, mu_win, WIN, TR)
    pre_consts = [rwkv_w0, w_w2_p, rwkv_a0, w_a2_p, w_g2, rwkv_k_k, rwkv_k_a]
    pre_fn = functools.partial(f_rw_pre, rw=RW)

    def pre_f32(xw, w0, ww, a0, wa, wg_, kk_, ka_):
        return pre_fn(xw, w0, ww.astype(F32), a0, wa.astype(F32), wg_.astype(F32), kk_, ka_)

    r_, dec_, k2_, v_, kk_, b_, g_ = rowwise("rw_pre", pre_f32, [_cols(xsh)], pre_consts, [(RW, F32)] * 7, [], 128)

    def tiles(a):
        return a.reshape(S, NT, 128)

    y_t, rw_states = rwkv_scan_forward(*(tiles(a) for a in (r_, dec_, k2_, v_, kk_, b_)))
    y_ = y_t.reshape(S, RW)
    post_consts = [rwkv_lnx_w, rwkv_lnx_b, r_k]
    o_rw = rowwise("rw_post", f_rw_post, [_cols(a) for a in (y_, r_, k2_, v_, g_)], post_consts, [(RW, BF16)], [], TR)[0]

    yg = matmul(o_gla, wb_g, "nn", F32, "branch_gla")
    yr = matmul(o_rw, wb_r, "nn", F32, "branch_rw")
    merge_fn = functools.partial(f_merge, d=D)
    merged = rowwise("merge", merge_fn, [_cols(p, 2 * D, lay["gate"]), _cols(yg), _cols(yr)], [gate_b], [(D, BF16)], [], TR)[0]
    mix = matmul(merged, W["w_out"], "nn", F32, "w_out")
    x2, h3 = rowwise("res_rms_ffn2", res_rms(1.0), [_cols(x1), _cols(mix)], [ffn2_norm], [(D, F32), (D, BF16)], [], TR)
    g3 = mm_gather(h3, W["ffn2_wg"], BF16, "ffn2_g", ["ffn2_wu"])
    u3, act3 = mm_gather(h3, W["ffn2_wu"], [BF16, BF16], "ffn2_u", ["ffn2_wd"], extras=[g3], epilogue=up_and_act)
    f3 = matmul(act3, W["ffn2_wd"], "nn", F32, "ffn2_d")

    def final_fn(a, f, t, g):
        def loss_of(a, f, g):
            yv = f_rms(a + 0.5 * f, g)
            return 0.5 * jnp.sum(jnp.mean(jnp.square(yv - t), axis=-1))
        val, vjp = jax.vjp(loss_of, a, f, g)
        da, df, dg = vjp(jnp.ones((), F32))
        return da, df, jnp.full((1, 128), val, F32), dg

    dx2, df3, loss_acc, d_final = rowwise("final_loss", final_fn, [_cols(x2), _cols(f3), _cols(tgt)], [fin_g],
                                          [(D, F32), (D, BF16)], [(1, 128), (1, D)], TR)
    grads = {"final_norm": d_final.reshape(D)}

    received = {}

    def pair_up(n, dw):
        r, cc = shard_shapes[n]
        flat = (4 * r, cc) if BIG[n] == "slab" else dw.shape
        theirs = sibling_swap("swap_" + n, [dw])[0]
        return n, (pair_sum("pair_" + n, dw.reshape(flat), theirs.reshape(flat)).reshape(dw.shape), BIG[n], r, cc)

    def mm_exchange(a, b, mode, out_dtype, name, pending):
        out, got = matmul(a, b, mode, out_dtype, name, job=exchange_job([entry for _, entry in pending]))
        received.update(zip([n for n, _ in pending], got))
        return out

    def ffn_backward(tag, h, gx, ux, act, df, wg, wu, wd):
        def through_act(dact, g, u):
            _, vjp = jax.vjp(f_swiglu, *f32(g, u))
            return vjp(dact)

        dgx, dux = matmul(df, wd, "nt", [BF16, BF16], tag + "_dact", extras=[gx, ux], epilogue=through_act)
        p_wd = pair_up(tag + "_wd", matmul(act, df, "tn", BF16, tag + "_dwd"))
        p_wg = pair_up(tag + "_wg", mm_exchange(h, dgx, "tn", BF16, tag + "_dwg", [p_wd]))
        p_wu = pair_up(tag + "_wu", mm_exchange(h, dux, "tn", BF16, tag + "_dwu", [p_wg]))
        dha = mm_exchange(dgx, wg, "nt", F32, tag + "_dh_g", [p_wu])
        dhb = matmul(dux, wu, "nt", F32, tag + "_dh_u")
        return dha, dhb

    def res_rms_bwd(name, coef, a, f, g, dx1, dha, dhb):
        def fn(a, f, dx1, dha, dhb, g):
            _, vjp = jax.vjp(res_rms(coef), a, f, g)
            return vjp((dx1, dha + dhb))

        return rowwise(name, fn, [_cols(a), _cols(f), _cols(dx1), _cols(dha), _cols(dhb)], [g],
                       [(D, F32), (D, BF16)], [(1, D)], TR)

    dh3a, dh3b = ffn_backward("ffn2", h3, g3, u3, act3, df3, W["ffn2_wg"], W["ffn2_wu"], W["ffn2_wd"])
    dx1, dmix, grads["ffn2_norm"] = res_rms_bwd("res_rms_ffn2_bwd", 1.0, x1, mix, ffn2_norm, dx2, dh3a, dh3b)

    p_wo = pair_up("w_out", matmul(merged, dmix, "tn", BF16, "d_w_out"))
    dmerged = mm_exchange(dmix, W["w_out"], "nt", F32, "d_merged", [p_wo])

    def merge_bwd(gp, a, b, d, gb):
        _, vjp = jax.vjp(merge_fn, gp, a, b, gb)
        return vjp(d)

    dgate, dyg, dyr, grads["gate_b"] = rowwise(
        "merge_bwd", merge_bwd, [_cols(p, 2 * D, lay["gate"]), _cols(yg), _cols(yr), _cols(dmerged)], [gate_b],
        [(2 * D, BF16), (D, BF16), (D, BF16)], [(1, 2 * D)], TR)
    do_gla = matmul(dyg, wb_g, "nt", BF16, "d_o_gla")
    do_rw = matmul(dyr, wb_r, "nt", F32, "d_o_rw")
    p_wb = pair_up("w_branch", jnp.concatenate([matmul(o_gla, dyg, "tn", BF16, "d_wb_gla"),
                                                matmul(o_rw, dyr, "tn", BF16, "d_wb_rw")], axis=0))

    def post_bwd(yv, rv, kv, vv, gv_, d, lw, lb, rk):
        _, vjp = jax.vjp(f_rw_post, yv, rv, kv, vv, gv_, lw, lb, rk)
        return vjp(d)

    dy_, dr_p, dk2_p, dv_p, dg_p, grads["rwkv_lnx_w"], grads["rwkv_lnx_b"], d_rk = rowwise(
        "rw_post_bwd", post_bwd, [_cols(a) for a in (y_, r_, k2_, v_, g_, do_rw)], post_consts,
        [(RW, F32)] * 5, [(1, RW)] * 3, 128)
    grads["rwkv_r_k"] = d_rk.reshape(rwkv_r_k.shape[1:])

    scan_cots = rwkv_scan_backward(*(tiles(a) for a in (r_, dec_, k2_, v_, kk_, b_)), rw_states, tiles(dy_))
    dr_s, dw_s, dk2_s, dv_s, dkk_s, db_s = (a.reshape(S, RW) for a in scan_cots)

    def pre_bwd(xw, c0, c1, c2, c3, c4, c5, c6, c7, c8, c9, w0, ww, a0, wa, wg_, kk_c, ka_c):
        _, vjp = jax.vjp(pre_fn, xw, w0, ww.astype(F32), a0, wa.astype(F32), wg_.astype(F32), kk_c, ka_c)
        return vjp((c0 + c6, c1, c2 + c7, c3 + c8, c4, c5, c9))

    dxsh, grads["rwkv_w0"], d_ww2, grads["rwkv_a0"], d_wa2, d_wg2, grads["rwkv_k_k"], grads["rwkv_k_a"] = rowwise(
        "rw_pre_bwd", pre_bwd,
        [_cols(xsh)] + [_cols(a) for a in (dr_s, dw_s, dk2_s, dv_s, dkk_s, db_s, dr_p, dk2_p, dv_p, dg_p)], pre_consts,
        [(WIN, F32)], [(1, RW), w_w2_p.shape, (1, RW), w_a2_p.shape, w_g2.shape, (1, RW), (1, RW)], 128)
    grads["rwkv_w_w2"], grads["rwkv_w_a2"], grads["rwkv_w_g2"] = d_ww2[:lo_w], d_wa2[:lo_a], d_wg2

    dq, dk, dv, dr, da, d_ga2, grads["gla_b_a"], grads["gla_gn_w"] = gla_backward(
        p, lay, gla_states, do_gla, gla_a2_p, gla_b_a, gla_gn_w, gheads)
    grads["gla_w_a2"] = d_ga2[:lo_g]

    dpw, dmu_win = token_shift_backward(p, dxsh, da, mu_win, WIN, ow["a"], TR)
    dmu = dmu_win[0]
    grads["rwkv_mu"] = jnp.concatenate([dmu[ow[k_]:ow[k_] + mu_parts[k_].shape[0]] for k_ in ("rr", "rk", "rv", "wd", "ad", "gd")]).reshape(1, -1)

    dp = jnp.concatenate([dpw, dgate, dv, dr, dq, dk], axis=1)
    d_w_in = from_padded(mm_exchange(h2, dp, "tn", BF16, "d_w_in", [p_wb]))
    p_wi = pair_up("w_in", d_w_in.reshape(D, 4, shard_shapes["w_in"][1]).transpose(1, 0, 2))
    dh2 = mm_exchange(dp, w_in_p, "nt", F32, "d_h2", [p_wi])
    zeros_d = jnp.zeros_like(dh2)
    dx0, df1, grads["mix_norm"] = res_rms_bwd("res_rms_mix_bwd", 0.5, xs, f1, mix_norm, dx1, dh2, zeros_d)

    dh1a, dh1b = ffn_backward("ffn1", h1, g1, u1, act1, df1, W["ffn1_wg"], W["ffn1_wu"], W["ffn1_wd"])

    def rms1_bwd(a, dha, dhb, dxa, g):
        _, vjp = jax.vjp(f_rms, a, g)
        da_, dg_ = vjp(dha + dhb)
        return da_ + dxa, dg_

    grad_x, grads["ffn1_norm"] = rowwise("rms1_bwd", rms1_bwd, [_cols(xs), _cols(dh1a), _cols(dh1b), _cols(dx0)],
                                         [ffn1_norm], [(D, F32)], [(1, D)], TR)

    names = list(BIG)
    halves = [chip_sum("chip_sum_" + n, received[n]) for n in names]
    others = sibling_swap("sibling_join", halves)
    final_grads, delta, new_m, new_v = {}, {}, {}, {}
    for n, h, o in zip(names, halves, others):
        res = adamw_halves(n, wts[n][0], h, o, moms[n][0], vars_[n][0])
        final_grads[n], delta[n], new_m[n], new_v[n] = (a.reshape(wts[n].shape) for a in res)

    rep_flat = jnp.concatenate([grads[n].reshape(-1) for n in REPLICATED + LORA])
    rep_rows = -(-rep_flat.shape[0] // 1024) * 8
    rep_sum = allreduce_small(jnp.pad(rep_flat, (0, rep_rows * 128 - rep_flat.shape[0])).reshape(rep_rows, 128)).reshape(-1)
    my_chip = 2 * lax.axis_index("x") + lax.axis_index("y")
    off = 0
    for n in REPLICATED + LORA:
        size = grads[n].size
        full = rep_sum[off:off + size].reshape(grads[n].shape)
        off += size
        if n in LORA:
            cc = shard_shapes[n][1]
            full = lax.dynamic_slice_in_dim(full, my_chip * cc, cc, axis=1)
        final_grads[n] = full.reshape(wts[n].shape)

    loss = lax.psum(loss_acc[0, 0], ("x", "y", "c"))

    small = REPLICATED + LORA

    def two(a):
        return a.reshape(-1, a.shape[-1])

    ds, ms_, vs_ = adamw_small([two(wts[n]) for n in small], [two(final_grads[n]) for n in small],
                               [two(moms[n]) for n in small], [two(vars_[n]) for n in small])
    for i, n in enumerate(small):
        shp = wts[n].shape
        delta[n], new_m[n], new_v[n] = ds[i].reshape(shp), ms_[i].reshape(shp), vs_[i].reshape(shp)

    return (loss, grad_x.reshape(x.shape), *[final_grads[n] for n in WEIGHTS], *[delta[n] for n in WEIGHTS],
            *[new_m[n] for n in WEIGHTS], *[new_v[n] for n in WEIGHTS])
```

```python
import functools
import math

import jax
import jax.numpy as jnp
from jax import lax
from jax.experimental import pallas as pl
from jax.experimental.pallas import tpu as pltpu

F32 = jnp.float32
BF16 = jnp.bfloat16
MESH_IDS = pl.DeviceIdType.MESH

NORM_EPS = 1e-6
GN_EPS = 64e-5
GLA_TAU = 16.0
CHUNK = 64
GLA_DK = 128
GLA_DV = 256
RW_HD = 64
LORA_PAD = 128
GATE_LORA = 256
ADAM_LR, ADAM_B1, ADAM_B2, ADAM_EPS, ADAM_WD, ADAM_STEP = 0.001, 0.9, 0.999, 1e-08, 0.01, 10

VMEM_LIMIT_BYTES = 56 * 1024 * 1024
HBM_SPEC = pl.BlockSpec(memory_space=pltpu.HBM)


def _cparams(sem=None):
    return pltpu.CompilerParams(dimension_semantics=sem, vmem_limit_bytes=VMEM_LIMIT_BYTES)


def _pick(n, target, mult=128):
    best = None
    for t in range(mult, min(n, target) + 1, mult):
        if n % t == 0:
            best = t
    return best if best is not None else n


_NN = (((1,), (0,)), ((), ()))
_NT = (((1,), (1,)), ((), ()))
_TN = (((0,), (0,)), ((), ()))


def _dg(a, b, dims):
    return lax.dot_general(a.astype(BF16), b.astype(BF16), dims, preferred_element_type=F32)


@jax.custom_vjp
def mm_nn(a, b):
    return _dg(a, b, _NN)


def _mm_nn_fwd(a, b):
    return _dg(a, b, _NN), (a, b)


def _mm_nn_bwd(res, g):
    a, b = res
    return _dg(g, b, _NT).astype(a.dtype), _dg(a, g, _TN).astype(b.dtype)


mm_nn.defvjp(_mm_nn_fwd, _mm_nn_bwd)


@jax.custom_vjp
def mm_nt(a, b):
    return _dg(a, b, _NT)


def _mm_nt_fwd(a, b):
    return _dg(a, b, _NT), (a, b)


def _mm_nt_bwd(res, g):
    a, b = res
    return _dg(g, b, _NN).astype(a.dtype), _dg(g, a, _TN).astype(b.dtype)


mm_nt.defvjp(_mm_nt_fwd, _mm_nt_bwd)


@jax.custom_vjp
def mm_tn(a, b):
    return _dg(a, b, _TN)


def _mm_tn_fwd(a, b):
    return _dg(a, b, _TN), (a, b)


def _mm_tn_bwd(res, g):
    a, b = res
    return _dg(b, g, _NT).astype(a.dtype), _dg(a, g, _NN).astype(b.dtype)


mm_tn.defvjp(_mm_tn_fwd, _mm_tn_bwd)


def _split3(x):
    h = x.astype(BF16)
    r = x - h.astype(F32)
    m = r.astype(BF16)
    l = (r - m.astype(F32)).astype(BF16)
    return h, m, l


def _block_ones(n, seg):
    i = lax.broadcasted_iota(jnp.int32, (n, n), 0) // seg
    j = lax.broadcasted_iota(jnp.int32, (n, n), 1) // seg
    return (i == j).astype(BF16)


def _segsum_raw(x, seg, terms):
    ones = _block_ones(128, seg)
    outs = []
    for j in range(x.shape[1] // 128):
        t = x[:, j * 128:(j + 1) * 128]
        parts = _split3(t)[:terms]
        acc = jnp.dot(parts[0], ones, preferred_element_type=F32)
        for p_ in parts[1:]:
            acc = acc + jnp.dot(p_, ones, preferred_element_type=F32)
        outs.append(acc)
    return outs[0] if len(outs) == 1 else jnp.concatenate(outs, axis=1)


@jax.custom_vjp
def segsum64(x):
    return _segsum_raw(x, RW_HD, 3)


segsum64.defvjp(lambda x: (_segsum_raw(x, RW_HD, 3), None), lambda _, g: (_segsum_raw(g, RW_HD, 3),))


def _tri(n, upper):
    i = lax.broadcasted_iota(jnp.int32, (n, n), 0)
    j = lax.broadcasted_iota(jnp.int32, (n, n), 1)
    return ((i <= j) if upper else (i >= j)).astype(BF16)


def _tri_mm(x, upper):
    t = _tri(x.shape[0], upper)
    h, m, l = _split3(x)
    return (jnp.dot(t, h, preferred_element_type=F32) + jnp.dot(t, m, preferred_element_type=F32)
            + jnp.dot(t, l, preferred_element_type=F32))


@jax.custom_vjp
def cumsum_rows(x):
    return _tri_mm(x, False)


cumsum_rows.defvjp(lambda x: (_tri_mm(x, False), None), lambda _, g: (_tri_mm(g, True),))


def _make_split(sizes, axis):
    offs = [sum(sizes[:i]) for i in range(len(sizes))]

    def cut(x):
        if axis == 1:
            return tuple(x[:, o:o + s] for o, s in zip(offs, sizes))
        return tuple(x[o:o + s, :] for o, s in zip(offs, sizes))

    @jax.custom_vjp
    def split(x):
        return cut(x)

    split.defvjp(lambda x: (cut(x), None), lambda _, gs: (jnp.concatenate(gs, axis=axis),))
    return split


@jax.custom_vjp
def log_sigmoid(z):
    return jnp.minimum(z, 0.0) - jnp.log(1.0 + jnp.exp(-jnp.abs(z)))


log_sigmoid.defvjp(lambda z: (log_sigmoid(z), z), lambda z, g: (g * (1.0 - jax.nn.sigmoid(z)),))


def silu(x):
    return x * jax.nn.sigmoid(x)


def matmul(a, b, mode, out_dtype, name, tm=1024, tn=512, tk=2048, job=None, extras=(), epilogue=None):
    if mode == "nn":
        (M, K), (K2, N) = a.shape, b.shape
    elif mode == "nt":
        (M, K), (N, K2) = a.shape, b.shape
    else:
        (K, M), (K2, N) = a.shape, b.shape
    assert K == K2, (name, a.shape, b.shape)
    tm, tn, tk = _pick(M, tm), _pick(N, tn), _pick(K, tk)
    grid = (M // tm, N // tn, K // tk)
    dims = {"nn": _NN, "nt": _NT, "tn": _TN}[mode]
    a_spec = pl.BlockSpec((tk, tm), lambda i, j, k: (k, i)) if mode == "tn" else pl.BlockSpec((tm, tk), lambda i, j, k: (i, k))
    b_spec = pl.BlockSpec((tn, tk), lambda i, j, k: (j, k)) if mode == "nt" else pl.BlockSpec((tk, tn), lambda i, j, k: (k, j))
    n_in = 0 if job is None else len(job.operands)
    n_out = 0 if job is None else len(job.out_shapes)
    n_ex = len(extras)
    main_dtypes = [out_dtype] if epilogue is None else list(out_dtype)
    n_main = len(main_dtypes)

    def body(a_ref, b_ref, *rest):
        ex_refs, rest = rest[:n_ex], rest[n_ex:]
        job_ins, o_refs, job_outs = rest[:n_in], rest[n_in:n_in + n_main], rest[n_in + n_main:n_in + n_main + n_out]
        acc_ref, sems = rest[n_in + n_main + n_out], rest[n_in + n_main + n_out + 1:]
        i, j, k = pl.program_id(0), pl.program_id(1), pl.program_id(2)
        if job is not None:
            @pl.when((i == 0) & (j == 0) & (k == 0))
            def _():
                job.start(job_ins, job_outs, sems)

        part = _dg(a_ref[...], b_ref[...], dims)

        @pl.when(k == 0)
        def _():
            acc_ref[...] = part

        @pl.when(k > 0)
        def _():
            acc_ref[...] += part

        @pl.when(k == grid[2] - 1)
        def _():
            acc = acc_ref[...]
            vals = (acc,) if epilogue is None else epilogue(acc, *[e[...] for e in ex_refs])
            for o_ref, val in zip(o_refs, vals):
                o_ref[...] = val.astype(o_ref.dtype)

        if job is not None:
            @pl.when((i == grid[0] - 1) & (j == grid[1] - 1) & (k == grid[2] - 1))
            def _():
                job.finish(job_ins, job_outs, sems)

    main_spec = pl.BlockSpec((tm, tn), lambda i, j, k: (i, j))
    job_operands = [] if job is None else list(job.operands)
    aliases = {} if job is None else {2 + n_ex + op: n_main + out for op, out in job.aliases.items()}
    res = pl.pallas_call(
        body, name=name, grid=grid, in_specs=[a_spec, b_spec] + [main_spec] * n_ex + [HBM_SPEC] * n_in,
        out_specs=[main_spec] * n_main + [HBM_SPEC] * n_out,
        out_shape=[jax.ShapeDtypeStruct((M, N), dt) for dt in main_dtypes] + ([] if job is None else list(job.out_shapes)),
        scratch_shapes=[pltpu.VMEM((tm, tn), F32)] + ([] if job is None else list(job.scratch)),
        input_output_aliases=aliases,
        compiler_params=_cparams(("parallel", "parallel", "arbitrary") if job is None else ("arbitrary",) * 3),
    )(a, b, *extras, *job_operands)
    mains = res[0] if epilogue is None else tuple(res[:n_main])
    return mains if job is None else (mains, res[n_main:])


def _cols(arr, width=None, off=0):
    width = arr.shape[1] if width is None else width
    assert off % width == 0, (off, width)
    return (arr, width, off // width)


def rowwise(name, fn, rows, consts, row_outs, acc_outs, tr, extra_specs=()):
    S = rows[0][0].shape[0]
    tr = min(tr, S)
    assert S % tr == 0
    n_in = len(rows) + len(extra_specs) + len(consts)
    n_ro = len(row_outs)
    in_specs = [pl.BlockSpec((tr, w), functools.partial(lambda i, cb: (i, cb), cb=cb)) for (_, w, cb) in rows]
    in_specs += [spec for (_, spec) in extra_specs]
    in_specs += [pl.BlockSpec(c.shape, lambda i: (0, 0)) for c in consts]
    out_shape = [jax.ShapeDtypeStruct((S, w), dt) for (w, dt) in row_outs]
    out_shape += [jax.ShapeDtypeStruct(shp, F32) for shp in acc_outs]
    out_specs = [pl.BlockSpec((tr, w), lambda i: (i, 0)) for (w, _) in row_outs]
    out_specs += [pl.BlockSpec(shp, lambda i: (0, 0)) for shp in acc_outs]

    def body(*refs):
        ins = [r[...] for r in refs[:n_in]]
        outs = fn(*ins)
        outs = outs if isinstance(outs, (tuple, list)) else (outs,)
        assert len(outs) == n_ro + len(acc_outs), (name, len(outs))
        for o_ref, val in zip(refs[n_in:n_in + n_ro], outs[:n_ro]):
            o_ref[...] = val.astype(o_ref.dtype)
        i = pl.program_id(0)
        for a_ref, val in zip(refs[n_in + n_ro:], outs[n_ro:]):
            @pl.when(i == 0)
            def _(a_ref=a_ref, val=val):
                a_ref[...] = val.astype(F32)

            @pl.when(i > 0)
            def _(a_ref=a_ref, val=val):
                a_ref[...] += val.astype(F32)

    res = pl.pallas_call(
        body, name=name, grid=(S // tr,), in_specs=in_specs, out_specs=out_specs, out_shape=out_shape,
        compiler_params=_cparams(("arbitrary",) if acc_outs else ("parallel",)),
    )(*[r[0] for r in rows], *[e[0] for e in extra_specs], *consts)
    return res


def f32(*xs):
    return [x.astype(F32) for x in xs]


def f_rms(x, g):
    return x * lax.rsqrt(jnp.mean(x * x, axis=-1, keepdims=True) + NORM_EPS) * g


def f_swiglu(gx, ux):
    return silu(gx) * ux


def f_merge(gp, yg, yr, gate_b, d):
    gates = jax.nn.sigmoid(gp + gate_b)
    g1, g2 = _make_split((d, d), 1)(gates)
    return g1 * yg + g2 * yr


def rw_window(rw):
    o = dict(rr=0, rk=rw, rv=2 * rw, gd=3 * rw, wd=3 * rw + GATE_LORA)
    o["ad"] = o["wd"] + LORA_PAD
    o["a"] = o["ad"] + LORA_PAD
    o["used"] = o["a"] + LORA_PAD
    return o


def f_rw_pre(xs, w0, w_w2, a0, w_a2, w_g2, k_k, k_a, rw):
    win = xs.shape[1]
    o = rw_window(rw)
    sizes = (rw, rw, rw, GATE_LORA, LORA_PAD, LORA_PAD, win - o["a"])
    rr, rk, rv, gd, wd, ad, _ = _make_split(sizes, 1)(xs)
    w_raw = w0 + mm_nn(jnp.tanh(wd), w_w2)
    dec = jnp.exp(-jnp.exp(log_sigmoid(w_raw) - 0.5))
    a = jax.nn.sigmoid(a0 + mm_nn(ad, w_a2))
    g = mm_nn(jax.nn.sigmoid(gd), w_g2)
    kx = rk * k_k
    kk = kx / jnp.maximum(jnp.sqrt(segsum64(kx * kx)), 1e-12)
    k2 = rk * (1.0 + (a - 1.0) * k_a)
    return rr, dec, k2, rv, kk, kk * a, g


def f_rw_post(y, r, k2, v, g, lnx_w, lnx_b, r_k):
    mu = segsum64(y) * (1.0 / RW_HD)
    yc = y - mu
    var = segsum64(yc * yc) * (1.0 / RW_HD)
    yn = yc * lax.rsqrt(var + GN_EPS) * lnx_w + lnx_b
    bonus = segsum64(r * k2 * r_k) * v
    return (yn + bonus) * g


def f_gla_chunk(q, k, v, r, a, st_prev, w_a2, b_a, gn_w, heads):
    z = mm_nn(a, w_a2) + b_a
    la = log_sigmoid(z) * (1.0 / GLA_TAU)
    cum = cumsum_rows(la)
    total = jnp.sum(la, axis=0, keepdims=True)
    kdec = k * jnp.exp(total - cum)
    et = jnp.exp(total)
    qs = q * (GLA_DK ** -0.5)
    sk = _make_split((GLA_DK,) * heads, 1)
    sv = _make_split((GLA_DV,) * heads, 1)
    ss = _make_split((GLA_DV,) * heads, 0)
    kd_h, q_h, et_h, v_h, st_h = sk(kdec), sk(qs), sk(et), sv(v), ss(st_prev)
    outs, news = [], []
    for h in range(heads):
        st_new = st_h[h] * et_h[h] + mm_tn(v_h[h], kd_h[h])
        o = mm_nt(q_h[h], st_new)
        o = o * lax.rsqrt(jnp.mean(o * o, axis=-1, keepdims=True) + NORM_EPS) * gn_w
        outs.append(o)
        news.append(st_new)
    o_all = outs[0] if heads == 1 else jnp.concatenate(outs, axis=1)
    st_all = news[0] if heads == 1 else jnp.concatenate(news, axis=0)
    return o_all * silu(r), st_all


def gla_forward(p, lay, w_a2, b_a, gn_w, heads):
    S = p.shape[0]
    nc = S // CHUNK
    gq, gv = heads * GLA_DK, heads * GLA_DV

    def spec(width, off, rev=False):
        assert off % width == 0
        return pl.BlockSpec((CHUNK, width), functools.partial(lambda n, cb: (n, cb), cb=off // width))

    def body(q_ref, k_ref, v_ref, r_ref, a_ref, w_ref, b_ref, g_ref, o_ref, st_out_ref, st_sc):
        @pl.when(pl.program_id(0) == 0)
        def _():
            st_sc[...] = jnp.zeros_like(st_sc)

        st_prev = st_sc[...]
        st_out_ref[0] = st_prev
        o, st_new = f_gla_chunk(*f32(q_ref[...], k_ref[...], v_ref[...], r_ref[...], a_ref[...]), st_prev,
                                w_ref[...], b_ref[...], g_ref[...], heads)
        o_ref[...] = o.astype(o_ref.dtype)
        st_sc[...] = st_new

    return pl.pallas_call(
        body, name="gla_fwd", grid=(nc,),
        in_specs=[spec(gq, lay["q"]), spec(gq, lay["k"]), spec(gv, lay["v"]), spec(gv, lay["r"]), spec(LORA_PAD, lay["a"]),
                  pl.BlockSpec(w_a2.shape, lambda n: (0, 0)), pl.BlockSpec(b_a.shape, lambda n: (0, 0)),
                  pl.BlockSpec(gn_w.shape, lambda n: (0, 0))],
        out_specs=[pl.BlockSpec((CHUNK, gv), lambda n: (n, 0)), pl.BlockSpec((1, gv, GLA_DK), lambda n: (n, 0, 0))],
        out_shape=[jax.ShapeDtypeStruct((S, gv), BF16), jax.ShapeDtypeStruct((nc, gv, GLA_DK), F32)],
        scratch_shapes=[pltpu.VMEM((gv, GLA_DK), F32)],
        compiler_params=_cparams(("arbitrary",)),
    )(p, p, p, p, p, w_a2, b_a, gn_w)


def gla_backward(p, lay, states, d_out, w_a2, b_a, gn_w, heads):
    S = p.shape[0]
    nc = S // CHUNK
    gq, gv = heads * GLA_DK, heads * GLA_DV

    def spec(width, off):
        assert off % width == 0
        return pl.BlockSpec((CHUNK, width), functools.partial(lambda n, cb: (nc - 1 - n, cb), cb=off // width))

    def rev(width):
        return pl.BlockSpec((CHUNK, width), lambda n: (nc - 1 - n, 0))

    def whole(arr):
        return pl.BlockSpec(arr.shape, lambda n: (0, 0))

    def body(q_ref, k_ref, v_ref, r_ref, a_ref, st_ref, do_ref, w_ref, b_ref, g_ref,
             dq_ref, dk_ref, dv_ref, dr_ref, da_ref, dw_ref, db_ref, dg_ref, dst_sc):
        n = pl.program_id(0)

        @pl.when(n == 0)
        def _():
            dst_sc[...] = jnp.zeros_like(dst_sc)

        fn = functools.partial(f_gla_chunk, heads=heads)
        prim = (*f32(q_ref[...], k_ref[...], v_ref[...], r_ref[...], a_ref[...]), st_ref[0],
                w_ref[...].astype(F32), b_ref[...], g_ref[...])
        _, vjp = jax.vjp(fn, *prim)
        dq, dk, dv, dr, da, dst, dw, db, dg = vjp((do_ref[...].astype(F32), dst_sc[...]))
        for ref, val in ((dq_ref, dq), (dk_ref, dk), (dv_ref, dv), (dr_ref, dr), (da_ref, da)):
            ref[...] = val.astype(ref.dtype)
        dst_sc[...] = dst

        @pl.when(n == 0)
        def _():
            dw_ref[...] = dw
            db_ref[...] = db
            dg_ref[...] = dg

        @pl.when(n > 0)
        def _():
            dw_ref[...] += dw
            db_ref[...] += db
            dg_ref[...] += dg

    return pl.pallas_call(
        body, name="gla_bwd", grid=(nc,),
        in_specs=[spec(gq, lay["q"]), spec(gq, lay["k"]), spec(gv, lay["v"]), spec(gv, lay["r"]), spec(LORA_PAD, lay["a"]),
                  pl.BlockSpec((1, gv, GLA_DK), lambda n: (nc - 1 - n, 0, 0)), rev(gv),
                  whole(w_a2), whole(b_a), whole(gn_w)],
        out_specs=[rev(gq), rev(gq), rev(gv), rev(gv), rev(LORA_PAD), whole(w_a2), whole(b_a), whole(gn_w)],
        out_shape=[jax.ShapeDtypeStruct((S, gq), BF16), jax.ShapeDtypeStruct((S, gq), BF16),
                   jax.ShapeDtypeStruct((S, gv), BF16), jax.ShapeDtypeStruct((S, gv), BF16),
                   jax.ShapeDtypeStruct((S, LORA_PAD), F32),
                   jax.ShapeDtypeStruct(w_a2.shape, F32), jax.ShapeDtypeStruct(b_a.shape, F32),
                   jax.ShapeDtypeStruct(gn_w.shape, F32)],
        scratch_shapes=[pltpu.VMEM((gv, GLA_DK), F32)],
        compiler_params=_cparams(("arbitrary",)),
    )(p, p, p, p, p, states, d_out, w_a2, b_a, gn_w)


SCAN_BLOCK = 32


def _scan_helpers(nt):
    ones = _block_ones(128, RW_HD)
    rows = lax.broadcasted_iota(jnp.int32, (nt * RW_HD, 128), 0) % RW_HD
    lanes = lax.broadcasted_iota(jnp.int32, (nt * RW_HD, 128), 1) % RW_HD
    eye = rows == lanes

    def bc(ref, t):
        parts = [jnp.broadcast_to(ref[t, j:j + 1, :], (RW_HD, 128)) for j in range(nt)]
        return parts[0] if nt == 1 else jnp.concatenate(parts, axis=0)

    def seg1(x):
        return jnp.dot(x.astype(BF16), ones, preferred_element_type=F32)

    def column(ref, t):
        return seg1(jnp.where(eye, bc(ref, t), 0.0))

    def put_diag(ref, t, x):
        put_colsum(ref, t, jnp.where(eye, x, 0.0))

    def put_colsum(ref, t, x, sign=1.0):
        for j in range(nt):
            ref[t, j:j + 1, :] = sign * jnp.sum(x[j * RW_HD:(j + 1) * RW_HD, :], axis=0, keepdims=True)

    return bc, seg1, column, put_diag, put_colsum


def rwkv_scan_forward(r, w, k2, v, kk, b):
    S, nt, _ = r.shape
    tb = min(SCAN_BLOCK, S)

    def body(r_ref, w_ref, k2_ref, v_ref, kk_ref, b_ref, y_ref, st_ref, s_sc):
        @pl.when(pl.program_id(0) == 0)
        def _():
            s_sc[...] = jnp.zeros_like(s_sc)

        bc, seg1, column, put_diag, put_colsum = _scan_helpers(nt)

        def step(t, carry):
            s = s_sc[...]
            sa_e = seg1(s * bc(kk_ref, t))
            s = s * bc(w_ref, t) - sa_e * bc(b_ref, t) + column(v_ref, t) * bc(k2_ref, t)
            s_sc[...] = s
            st_ref[t] = s
            put_diag(y_ref, t, seg1(s * bc(r_ref, t)))
            return carry

        lax.fori_loop(0, tb, step, 0, unroll=8)

    row = pl.BlockSpec((tb, nt, 128), lambda i: (i, 0, 0))
    return pl.pallas_call(
        body, name="rwkv_scan_fwd", grid=(S // tb,),
        in_specs=[row] * 6,
        out_specs=[row, pl.BlockSpec((tb, nt * RW_HD, 128), lambda i: (i, 0, 0))],
        out_shape=[jax.ShapeDtypeStruct((S, nt, 128), F32), jax.ShapeDtypeStruct((S, nt * RW_HD, 128), F32)],
        scratch_shapes=[pltpu.VMEM((nt * RW_HD, 128), F32)],
        compiler_params=_cparams(("arbitrary",)),
    )(r, w, k2, v, kk, b)


def rwkv_scan_backward(r, w, k2, v, kk, b, states, dy):
    S, nt, _ = r.shape
    tb = min(SCAN_BLOCK, S)
    nb = S // tb

    def body(r_ref, w_ref, k2_ref, v_ref, kk_ref, b_ref, st_ref, edge_ref, dy_ref,
             dr_ref, dw_ref, dk2_ref, dv_ref, dkk_ref, db_ref, ds_sc, before_sc):
        @pl.when(pl.program_id(0) == 0)
        def _():
            ds_sc[...] = jnp.zeros_like(ds_sc)

        before_sc[...] = jnp.where(pl.program_id(0) == nb - 1, 0.0, edge_ref[0])

        bc, seg1, column, put_diag, put_colsum = _scan_helpers(nt)

        def step(i, carry):
            t = tb - 1 - i
            s_prev = jnp.where(t == 0, before_sc[...], st_ref[jnp.maximum(t - 1, 0)])
            r_e, w_e, k2_e, kk_e, b_e = (bc(ref, t) for ref in (r_ref, w_ref, k2_ref, kk_ref, b_ref))
            v_e = column(v_ref, t)
            sa_e = seg1(s_prev * kk_e)
            dy_e = column(dy_ref, t)
            put_colsum(dr_ref, t, st_ref[t] * dy_e)
            ds = ds_sc[...] + dy_e * r_e
            put_colsum(dw_ref, t, ds * s_prev)
            nsa_e = seg1(ds * b_e)
            put_colsum(db_ref, t, ds * sa_e, -1.0)
            put_diag(dv_ref, t, seg1(ds * k2_e))
            put_colsum(dk2_ref, t, ds * v_e)
            put_colsum(dkk_ref, t, s_prev * nsa_e, -1.0)
            ds_sc[...] = ds * w_e - nsa_e * kk_e
            return carry

        lax.fori_loop(0, tb, step, 0, unroll=8)

    row = pl.BlockSpec((tb, nt, 128), lambda i: (nb - 1 - i, 0, 0))
    return pl.pallas_call(
        body, name="rwkv_scan_bwd", grid=(nb,),
        in_specs=[row] * 6 + [pl.BlockSpec((tb, nt * RW_HD, 128), lambda i: (nb - 1 - i, 0, 0)),
                              pl.BlockSpec((1, nt * RW_HD, 128), lambda i: (jnp.maximum((nb - 1 - i) * tb - 1, 0), 0, 0)), row],
        out_specs=[row] * 6,
        out_shape=[jax.ShapeDtypeStruct((S, nt, 128), F32)] * 6,
        scratch_shapes=[pltpu.VMEM((nt * RW_HD, 128), F32), pltpu.VMEM((nt * RW_HD, 128), F32)],
        compiler_params=_cparams(("arbitrary",)),
    )(r, w, k2, v, kk, b, states, states, dy)


def _edge_spec(width, col_block, tr, n_rows, after):
    last = n_rows // 8 - 1
    if after:
        return pl.BlockSpec((8, width), lambda i: (jnp.minimum((i + 1) * (tr // 8), last), col_block))
    return pl.BlockSpec((8, width), lambda i: (jnp.maximum(i * (tr // 8) - 1, 0), col_block))


def _shifted_prev(p, prev8):
    first = jnp.where(pl.program_id(0) == 0, 0.0, prev8[7:8, :])
    rows = lax.broadcasted_iota(jnp.int32, p.shape, 0)
    return jnp.where(rows == 0, first, pltpu.roll(p, 1, axis=0))


def token_shift_forward(p, mu_win, win, tr):
    def fn(pw, prev8, mu):
        return pw + mu * (_shifted_prev(pw, prev8) - pw)

    return rowwise("token_shift_fwd", fn, [_cols(p, win, 0)], [mu_win], [(win, F32)], [], tr,
                   extra_specs=[(p, _edge_spec(win, 0, tr, p.shape[0], False))])[0]


def token_shift_backward(p, dxs, da_gla, mu_win, win, a_off, tr):
    S = p.shape[0]
    n = S // min(tr, S)

    def fn(pw, dx, da, prev8, next8, mu):
        trr = pw.shape[0]
        last = jnp.where(pl.program_id(0) == n - 1, 0.0, next8[0:1, :])
        rows = lax.broadcasted_iota(jnp.int32, dx.shape, 0)
        dnext = jnp.where(rows == trr - 1, last, pltpu.roll(dx, trr - 1, axis=0))
        dp = (1.0 - mu) * dx + mu * dnext
        dp = jnp.concatenate([dp[:, :a_off], dp[:, a_off:a_off + LORA_PAD] + da, dp[:, a_off + LORA_PAD:]], axis=1)
        dmu = jnp.sum(dx * (_shifted_prev(pw, prev8) - pw), axis=0, keepdims=True)
        return dp, dmu

    return rowwise("token_shift_bwd", fn, [_cols(p, win, 0), _cols(dxs), _cols(da_gla)], [mu_win],
                   [(win, BF16)], [(1, win)], tr,
                   extra_specs=[(p, _edge_spec(win, 0, tr, S, False)), (dxs, _edge_spec(win, 0, tr, S, True))])


def _adamw_math(w, g, m, v):
    m = ADAM_B1 * m + (1.0 - ADAM_B1) * g
    v = ADAM_B2 * v + (1.0 - ADAM_B2) * (g * g)
    m_hat = m / (1.0 - ADAM_B1 ** ADAM_STEP)
    v_hat = v / (1.0 - ADAM_B2 ** ADAM_STEP)
    delta = -ADAM_LR * (m_hat / (jnp.sqrt(v_hat) + ADAM_EPS) + ADAM_WD * w)
    return delta, m, v


def adamw_small(ws, gs, ms, vs):
    n = len(ws)

    def body(*refs):
        for i in range(n):
            d, m, v = _adamw_math(refs[i][...], refs[n + i][...], refs[2 * n + i][...], refs[3 * n + i][...])
            refs[4 * n + i][...] = d
            refs[5 * n + i][...] = m
            refs[6 * n + i][...] = v

    shapes = [jax.ShapeDtypeStruct(w.shape, F32) for w in ws]
    outs = pl.pallas_call(body, name="adamw_small", out_shape=shapes * 3, compiler_params=_cparams())(*ws, *gs, *ms, *vs)
    return outs[:n], outs[n:2 * n], outs[2 * n:]


def _place():
    x, y, c = lax.axis_index("x"), lax.axis_index("y"), lax.axis_index("c")
    chips = [(1 - x, y), (x, 1 - y), (1 - x, 1 - y)]
    return x, y, c, chips


def _full_shape(kind, r, c):
    return {"col": (r, 4 * c), "row": (4 * r, c), "slab": (4, r, c)}[kind]


def _slab(ref, kind, k, r, c, half=None):
    n, off = (r, 0) if half is None else (r // 2, half * (r // 2))
    if kind == "col":
        return ref.at[pl.ds(off, n), pl.ds(k * c, c)]
    if kind == "row":
        return ref.at[pl.ds(k * r + off, n), :]
    return ref.at[k, pl.ds(off, n), :]


def _remote(src, dst, sems, idx, to):
    return pltpu.make_async_remote_copy(src_ref=src, dst_ref=dst, send_sem=sems[0].at[idx], recv_sem=sems[1].at[idx],
                                        device_id=to, device_id_type=MESH_IDS)


class CommJob:
    def __init__(self, operands, out_shapes, scratch, start, finish, aliases=None):
        self.operands, self.out_shapes, self.scratch, self.start, self.finish = operands, out_shapes, scratch, start, finish
        self.aliases = aliases or {}


def device_gather_job(vec, zeros8):
    flips = [(fx, fy, fc) for fx in (0, 1) for fy in (0, 1) for fc in (0, 1) if (fx, fy, fc) != (0, 0, 0)]

    def sends(srcs, outs, sems):
        x, y, c, _ = _place()
        return [_remote(srcs[0], outs[0].at[4 * x + 2 * y + c], sems, (j,), (x ^ fx, y ^ fy, c ^ fc))
                for j, (fx, fy, fc) in enumerate(flips)]

    def start(srcs, outs, sems):
        for cp in sends(srcs, outs, sems):
            cp.start()

    def finish(srcs, outs, sems):
        x, y, c, _ = _place()
        for j, (fx, fy, fc) in enumerate(flips):
            blk = outs[0].at[4 * (x ^ fx) + 2 * (y ^ fy) + (c ^ fc)]
            _remote(blk, blk, sems, (j,), (x, y, c)).wait_recv()
        for cp in sends(srcs, outs, sems):
            cp.wait_send()

    dma = pltpu.SemaphoreType.DMA
    return CommJob([vec, zeros8], [jax.ShapeDtypeStruct(zeros8.shape, F32)], [dma((7,)), dma((7,))], start, finish, {1: 0})


def sum_devices(gathered, own):
    _, R, C = gathered.shape
    tr = _pick(R, 512, 8)

    def body(g_ref, own_ref, o_ref):
        me = 4 * lax.axis_index("x") + 2 * lax.axis_index("y") + lax.axis_index("c")
        acc = jnp.where(me == 0, own_ref[...], g_ref[0])
        for k in range(1, 8):
            acc = acc + jnp.where(me == k, own_ref[...], g_ref[k])
        o_ref[...] = acc

    return pl.pallas_call(body, name="sum_devices", grid=(R // tr,),
                          in_specs=[pl.BlockSpec((8, tr, C), lambda i: (0, i, 0)), pl.BlockSpec((tr, C), lambda i: (i, 0))],
                          out_specs=pl.BlockSpec((tr, C), lambda i: (i, 0)), out_shape=jax.ShapeDtypeStruct((R, C), F32),
                          compiler_params=_cparams(("parallel",)))(gathered, own)


def gather_job(big, small=()):
    big, small = list(big), list(small)
    nb, ns = len(big), len(small)
    meta = [(kind, *a.shape) for a, kind in big + small]

    def sends(srcs, outs, sems):
        own_s, own_r, ici_s, ici_r, _, _, sm_s, sm_r = sems
        x, y, c, chips = _place()
        me, sib = 2 * x + y, (x, y, 1 - c)
        cps = []
        for a in range(nb):
            kind, r, cc = meta[a]
            for j, chip in enumerate(chips):
                cps.append(_remote(srcs[a].at[pl.ds(c * (r // 2), r // 2)], _slab(outs[a], kind, me, r, cc, c),
                                   (ici_s, ici_r), (a, j), (*chip, c)))
        for a in range(nb):
            kind, r, cc = meta[a]
            cps.append(_remote(srcs[a], _slab(outs[a], kind, me, r, cc), (own_s, own_r), (a,), sib))
        for s in range(ns):
            kind, r, cc = meta[nb + s]
            for t, to in enumerate([sib] + [(*chip, c) for chip in chips]):
                cps.append(_remote(srcs[nb + s], _slab(outs[nb + s], kind, me, r, cc), (sm_s, sm_r), (s, t), to))
        return cps

    def start(srcs, outs, sems):
        for cp in sends(srcs, outs, sems):
            cp.start()

    def finish(srcs, outs, sems):
        own_s, own_r, ici_s, ici_r, fwd_s, fwd_r, sm_s, sm_r = sems
        x, y, c, chips = _place()
        me, sib = 2 * x + y, (x, y, 1 - c)
        cids = [2 * chip[0] + chip[1] for chip in chips]
        hands = []
        for a in range(nb):
            kind, r, cc = meta[a]
            for j in range(3):
                blk = _slab(outs[a], kind, cids[j], r, cc, c)
                _remote(blk, blk, (ici_s, ici_r), (a, j), sib).wait_recv()
                hands.append(_remote(blk, blk, (fwd_s, fwd_r), (a, j), sib))
                hands[-1].start()
        for a in range(nb):
            kind, r, cc = meta[a]
            for j in range(3):
                blk = _slab(outs[a], kind, cids[j], r, cc, 1 - c)
                _remote(blk, blk, (fwd_s, fwd_r), (a, j), sib).wait_recv()
            blk = _slab(outs[a], kind, me, r, cc)
            _remote(blk, blk, (own_s, own_r), (a,), sib).wait_recv()
        for s in range(ns):
            kind, r, cc = meta[nb + s]
            for t, frm in enumerate([me] + cids):
                blk = _slab(outs[nb + s], kind, frm, r, cc)
                _remote(blk, blk, (sm_s, sm_r), (s, t), sib).wait_recv()
        for cp in sends(srcs, outs, sems) + hands:
            cp.wait_send()

    dma = pltpu.SemaphoreType.DMA
    nb1, ns1 = max(nb, 1), max(ns, 1)
    return CommJob([a for a, _ in big + small],
                   [jax.ShapeDtypeStruct(_full_shape(kind, r, cc), BF16) for (kind, r, cc) in meta],
                   [dma((nb1,)), dma((nb1,)), dma((nb1, 3)), dma((nb1, 3)), dma((nb1, 3)), dma((nb1, 3)),
                    dma((ns1, 4)), dma((ns1, 4))], start, finish)


def run_job(name, job):
    n_in, n_out = len(job.operands), len(job.out_shapes)

    def body(*refs):
        ins, outs, sems = refs[:n_in], refs[n_in:n_in + n_out], refs[n_in + n_out:]
        job.start(ins, outs, sems)
        job.finish(ins, outs, sems)

    return pl.pallas_call(body, name=name, in_specs=[HBM_SPEC] * n_in, out_specs=[HBM_SPEC] * n_out,
                          out_shape=job.out_shapes, scratch_shapes=job.scratch)(*job.operands)


def sibling_swap(name, arrays):
    n = len(arrays)

    def body(*refs):
        srcs, outs, sems = refs[:n], refs[n:2 * n], refs[2 * n:]
        x, y, c, _ = _place()
        cps = [_remote(srcs[a], outs[a], sems, (a,), (x, y, 1 - c)) for a in range(n)]
        for cp in cps:
            cp.start()
        for cp in cps:
            cp.wait_recv()
        for cp in cps:
            cp.wait_send()

    return pl.pallas_call(
        body, name=name, in_specs=[HBM_SPEC] * n, out_specs=[HBM_SPEC] * n,
        out_shape=[jax.ShapeDtypeStruct(a.shape, a.dtype) for a in arrays],
        scratch_shapes=[pltpu.SemaphoreType.DMA((n,)), pltpu.SemaphoreType.DMA((n,))],
    )(*arrays)


def exchange_job(sums):
    n = len(sums)

    def sends(srcs, outs, sems):
        ici_s, ici_r, sib_s, sib_r = sems
        x, y, c, chips = _place()
        me, sib = 2 * x + y, (x, y, 1 - c)
        cps = []
        for a, (_, kind, r, cc) in enumerate(sums):
            for j, chip in enumerate(chips):
                cid = 2 * chip[0] + chip[1]
                cps.append(_remote(_slab(srcs[a], kind, cid, r, cc, c), outs[a].at[me], (ici_s, ici_r), (a, j), (*chip, c)))
            cps.append(_remote(_slab(srcs[a], kind, me, r, cc, 1 - c), outs[a].at[me], (sib_s, sib_r), (a,), sib))
        return cps

    def start(srcs, outs, sems):
        for cp in sends(srcs, outs, sems):
            cp.start()

    def finish(srcs, outs, sems):
        ici_s, ici_r, sib_s, sib_r = sems
        x, y, c, chips = _place()
        me, sib = 2 * x + y, (x, y, 1 - c)
        for a in range(n):
            for j, chip in enumerate(chips):
                blk = outs[a].at[2 * chip[0] + chip[1]]
                _remote(blk, blk, (ici_s, ici_r), (a, j), sib).wait_recv()
            _remote(outs[a].at[me], outs[a].at[me], (sib_s, sib_r), (a,), sib).wait_recv()
        for cp in sends(srcs, outs, sems):
            cp.wait_send()

    dma = pltpu.SemaphoreType.DMA
    return CommJob([s[0] for s in sums], [jax.ShapeDtypeStruct((4, r // 2, cc), BF16) for (_, _, r, cc) in sums],
                   [dma((n, 3)), dma((n, 3)), dma((n,)), dma((n,))], start, finish)


def allreduce_small(vec):
    R, C = vec.shape

    def body(src, out, gathered, send_sems, recv_sems):
        x, y, c, _ = _place()
        me = 4 * x + 2 * y + c
        gathered[me] = src[...]
        peers = [(fx, fy, fc) for fx in (0, 1) for fy in (0, 1) for fc in (0, 1) if (fx, fy, fc) != (0, 0, 0)]
        sends = []
        for j, (fx, fy, fc) in enumerate(peers):
            to = (x ^ fx, y ^ fy, c ^ fc)
            cp = pltpu.make_async_remote_copy(
                src_ref=src, dst_ref=gathered.at[me], send_sem=send_sems.at[j], recv_sem=recv_sems.at[j],
                device_id=to, device_id_type=MESH_IDS)
            cp.start()
            sends.append(cp)
        for j, (fx, fy, fc) in enumerate(peers):
            frm = 4 * (x ^ fx) + 2 * (y ^ fy) + (c ^ fc)
            pltpu.make_async_remote_copy(
                src_ref=src, dst_ref=gathered.at[frm], send_sem=send_sems.at[j], recv_sem=recv_sems.at[j],
                device_id=(x, y, c), device_id_type=MESH_IDS).wait_recv()
        for cp in sends:
            cp.wait_send()
        acc = gathered[0]
        for k in range(1, 8):
            acc = acc + gathered[k]
        out[...] = acc

    vm = pl.BlockSpec(memory_space=pltpu.VMEM)
    return pl.pallas_call(
        body, name="allreduce_small", in_specs=[vm], out_specs=vm,
        out_shape=jax.ShapeDtypeStruct((R, C), F32),
        scratch_shapes=[pltpu.VMEM((8, R, C), F32), pltpu.SemaphoreType.DMA((7,)), pltpu.SemaphoreType.DMA((7,))],
        compiler_params=_cparams(),
    )(vec)


def pair_sum(name, mine, theirs):
    rows, cols = mine.shape
    tr = _pick(rows, max(16, (1 << 20) // cols), 16)
    return rowwise(name, lambda a, b: a.astype(F32) + b.astype(F32), [_cols(mine), _cols(theirs)], [], [(cols, BF16)], [], tr)[0]


def chip_sum(name, rb):
    _, rh, C = rb.shape
    tr = _pick(rh, max(16, (1 << 19) // C), 16)

    def body(r_ref, o_ref):
        acc = r_ref[0].astype(F32)
        for k in range(1, 4):
            acc = acc + r_ref[k].astype(F32)
        o_ref[...] = acc

    return pl.pallas_call(body, name=name, grid=(rh // tr,),
                          in_specs=[pl.BlockSpec((4, tr, C), lambda i: (0, i, 0))],
                          out_specs=pl.BlockSpec((tr, C), lambda i: (i, 0)),
                          out_shape=jax.ShapeDtypeStruct((rh, C), F32),
                          compiler_params=_cparams(("parallel",)))(rb)


def adamw_halves(name, w, mine, theirs, m, v):
    rows, cols = w.shape
    tr = _pick(rows // 2, max(8, (1 << 19) // cols), 8)
    nbh = rows // 2 // tr
    full = pl.BlockSpec((tr, cols), lambda i: (i, 0))
    half = pl.BlockSpec((tr, cols), lambda i: (i % nbh, 0))

    def body(w_ref, a_ref, b_ref, m_ref, v_ref, g_out, d_out, m_out, v_out):
        is_mine = (pl.program_id(0) // nbh) == lax.axis_index("c")
        g = jnp.where(is_mine, a_ref[...], b_ref[...])
        d, mn, vn = _adamw_math(w_ref[...], g, m_ref[...], v_ref[...])
        g_out[...] = g
        d_out[...] = d
        m_out[...] = mn
        v_out[...] = vn

    return pl.pallas_call(body, name="adamw_" + name, grid=(rows // tr,), in_specs=[full, half, half, full, full],
                          out_specs=[full] * 4, out_shape=[jax.ShapeDtypeStruct((rows, cols), F32)] * 4,
                          compiler_params=_cparams(("parallel",)))(w, mine, theirs, m, v)


BIG = {"ffn1_wg": "col", "ffn1_wu": "col", "ffn1_wd": "row", "w_in": "slab", "w_branch": "row", "w_out": "row",
       "ffn2_wg": "col", "ffn2_wu": "col", "ffn2_wd": "row"}
LORA = ["gla_w_a2", "rwkv_w_w2", "rwkv_w_a2", "rwkv_w_g2"]
REPLICATED = ["ffn1_norm", "mix_norm", "gla_b_a", "gla_gn_w", "rwkv_mu", "rwkv_w0", "rwkv_a0", "rwkv_k_k", "rwkv_k_a",
              "rwkv_r_k", "rwkv_lnx_w", "rwkv_lnx_b", "gate_b", "ffn2_norm", "final_norm"]
WEIGHTS = ["ffn1_norm", "ffn1_wg", "ffn1_wu", "ffn1_wd", "mix_norm", "w_in", "gla_w_a2", "gla_b_a", "gla_gn_w", "rwkv_mu",
           "rwkv_w0", "rwkv_w_w2", "rwkv_a0", "rwkv_w_a2", "rwkv_w_g2", "rwkv_k_k", "rwkv_k_a", "rwkv_r_k", "rwkv_lnx_w",
           "rwkv_lnx_b", "gate_b", "w_branch", "w_out", "ffn2_norm", "ffn2_wg", "ffn2_wu", "ffn2_wd", "final_norm"]


def kernel(x, ffn1_norm, ffn1_wg, ffn1_wu, ffn1_wd, mix_norm, w_in, gla_w_a2, gla_b_a, gla_gn_w, rwkv_mu, rwkv_w0, rwkv_w_w2, rwkv_a0, rwkv_w_a2, rwkv_w_g2, rwkv_k_k, rwkv_k_a, rwkv_r_k, rwkv_lnx_w, rwkv_lnx_b, gate_b, w_branch, w_out, ffn2_norm, ffn2_wg, ffn2_wu, ffn2_wd, final_norm, loss_target, m_ffn1_norm, m_ffn1_wg, m_ffn1_wu, m_ffn1_wd, m_mix_norm, m_w_in, m_gla_w_a2, m_gla_b_a, m_gla_gn_w, m_rwkv_mu, m_rwkv_w0, m_rwkv_w_w2, m_rwkv_a0, m_rwkv_w_a2, m_rwkv_w_g2, m_rwkv_k_k, m_rwkv_k_a, m_rwkv_r_k, m_rwkv_lnx_w, m_rwkv_lnx_b, m_gate_b, m_w_branch, m_w_out, m_ffn2_norm, m_ffn2_wg, m_ffn2_wu, m_ffn2_wd, m_final_norm, v_ffn1_norm, v_ffn1_wg, v_ffn1_wu, v_ffn1_wd, v_mix_norm, v_w_in, v_gla_w_a2, v_gla_b_a, v_gla_gn_w, v_rwkv_mu, v_rwkv_w0, v_rwkv_w_w2, v_rwkv_a0, v_rwkv_w_a2, v_rwkv_w_g2, v_rwkv_k_k, v_rwkv_k_a, v_rwkv_r_k, v_rwkv_lnx_w, v_rwkv_lnx_b, v_gate_b, v_w_branch, v_w_out, v_ffn2_norm, v_ffn2_wg, v_ffn2_wu, v_ffn2_wd, v_final_norm):
    args = dict(locals())
    wts = {n: args[n] for n in WEIGHTS}
    moms = {n: args["m_" + n] for n in WEIGHTS}
    vars_ = {n: args["v_" + n] for n in WEIGHTS}

    xs = x[0]
    tgt = loss_target[0]
    S, D = xs.shape
    FF = ffn1_wd.shape[1] * 4
    gheads = gla_b_a.shape[-1] // GLA_DK
    GQ, GV = gheads * GLA_DK, gheads * GLA_DV
    rheads = rwkv_r_k.shape[1]
    RW = rheads * RW_HD
    NT = RW // 128
    lo_g = gla_w_a2.shape[1]
    lo_w = rwkv_w_w2.shape[1]
    lo_a = rwkv_w_a2.shape[1]
    assert rwkv_w_g2.shape[1] == GATE_LORA and RW % 128 == 0

    ow = rw_window(RW)
    WIN = -(-ow["used"] // (2 * D)) * (2 * D)
    lay = dict(gate=WIN, v=WIN + 2 * D, r=WIN + 2 * D + GV, q=WIN + 2 * D + 2 * GV, k=WIN + 2 * D + 2 * GV + GQ, a=ow["a"])
    DP = lay["k"] + GQ
    DIN = w_in.shape[-1] * 4
    o_sizes = [GQ, GQ, GV, GV, lo_g, RW, RW, RW, lo_w, lo_a, GATE_LORA, 2 * D]
    o_offs = [sum(o_sizes[:i]) for i in range(len(o_sizes))]
    assert o_offs[-1] + o_sizes[-1] == DIN
    p_offs = [lay["q"], lay["k"], lay["v"], lay["r"], ow["a"], ow["rr"], ow["rk"], ow["rv"], ow["wd"], ow["ad"], ow["gd"], lay["gate"]]

    def to_padded(w):
        order = sorted(range(len(o_sizes)), key=lambda i: p_offs[i])
        parts, pos = [], 0
        for i in order:
            if p_offs[i] > pos:
                parts.append(jnp.zeros((w.shape[0], p_offs[i] - pos), w.dtype))
            parts.append(w[:, o_offs[i]:o_offs[i] + o_sizes[i]])
            pos = p_offs[i] + o_sizes[i]
        if pos < DP:
            parts.append(jnp.zeros((w.shape[0], DP - pos), w.dtype))
        return jnp.concatenate(parts, axis=1)

    def from_padded(w):
        return jnp.concatenate([w[:, p_offs[i]:p_offs[i] + o_sizes[i]] for i in range(len(o_sizes))], axis=1)

    def pad_rows(w, rows):
        return jnp.pad(w, ((0, rows - w.shape[0]), (0, 0)))

    mu = rwkv_mu[0]
    mu_parts = {"rr": mu[0:RW], "rk": mu[RW:2 * RW], "rv": mu[2 * RW:3 * RW], "wd": mu[3 * RW:3 * RW + lo_w],
                "ad": mu[3 * RW + lo_w:3 * RW + lo_w + lo_a], "gd": mu[3 * RW + lo_w + lo_a:]}
    mu_win = jnp.zeros((WIN,), F32)
    for key, val in mu_parts.items():
        mu_win = lax.dynamic_update_slice(mu_win, val, (ow[key],))
    mu_win = mu_win.reshape(1, WIN)

    shard_shapes = {n: wts[n].shape[1:] for n in list(BIG) + LORA}
    W = {}

    def shard(n):
        return (wts[n][0].astype(BF16), BIG[n])

    def mm_gather(a, b, out_dtype, name, gather, lora=(), **epilogue):
        out, got = matmul(a, b, "nn", out_dtype, name, **epilogue,
                          job=gather_job([shard(n) for n in gather], [(wts[n][0].astype(BF16), "col") for n in lora]))
        W.update(zip(list(gather) + list(lora), got))
        return out

    W["ffn1_wg"] = run_job("gather_ffn1_wg", gather_job([shard("ffn1_wg")]))[0]
    r_k = rwkv_r_k.reshape(1, RW)
    fin_g = final_norm.reshape(1, D)

    TR = min(128, S)
    def up_and_act(acc, g):
        return acc, f_swiglu(g.astype(F32), acc)

    h1 = rowwise("rms1", lambda a, g: f_rms(a, g), [_cols(xs)], [ffn1_norm], [(D, BF16)], [], TR)[0]
    g1 = mm_gather(h1, W["ffn1_wg"], BF16, "ffn1_g", ["ffn1_wu"])
    u1, act1 = mm_gather(h1, W["ffn1_wu"], [BF16, BF16], "ffn1_u", ["ffn1_wd"], extras=[g1], epilogue=up_and_act)
    f1 = mm_gather(act1, W["ffn1_wd"], F32, "ffn1_d", ["w_in"], LORA)
    w_in_p = to_padded(W["w_in"].transpose(1, 0, 2).reshape(D, DIN))
    gla_a2_p = pad_rows(W["gla_w_a2"], LORA_PAD)
    w_w2_p = pad_rows(W["rwkv_w_w2"], LORA_PAD)
    w_a2_p = pad_rows(W["rwkv_w_a2"], LORA_PAD)
    w_g2 = W["rwkv_w_g2"]

    def res_rms(coef):
        def fn(a, f, g):
            x1 = a + coef * f
            return x1, f_rms(x1, g)
        return fn

    x1, h2 = rowwise("res_rms_mix", res_rms(0.5), [_cols(xs), _cols(f1)], [mix_norm], [(D, F32), (D, BF16)], [], TR)
    p = mm_gather(h2, w_in_p, F32, "w_in", ["w_branch", "w_out", "ffn2_wg"])
    wb_g, wb_r = W["w_branch"][:GV], W["w_branch"][GV:]

    o_gla, gla_states = gla_forward(p, lay, gla_a2_p, gla_b_a, gla_gn_w, gheads)
    xsh = token_shift_forward(p, mu_win, WIN, TR)
    pre_consts = [rwkv_w0, w_w2_p, rwkv_a0, w_a2_p, w_g2, rwkv_k_k, rwkv_k_a]
    pre_fn = functools.partial(f_rw_pre, rw=RW)

    def pre_f32(xw, w0, ww, a0, wa, wg_, kk_, ka_):
        return pre_fn(xw, w0, ww.astype(F32), a0, wa.astype(F32), wg_.astype(F32), kk_, ka_)

    r_, dec_, k2_, v_, kk_, b_, g_ = rowwise("rw_pre", pre_f32, [_cols(xsh)], pre_consts, [(RW, F32)] * 7, [], 128)

    def tiles(a):
        return a.reshape(S, NT, 128)

    y_t, rw_states = rwkv_scan_forward(*(tiles(a) for a in (r_, dec_, k2_, v_, kk_, b_)))
    y_ = y_t.reshape(S, RW)
    post_consts = [rwkv_lnx_w, rwkv_lnx_b, r_k]
    o_rw = rowwise("rw_post", f_rw_post, [_cols(a) for a in (y_, r_, k2_, v_, g_)], post_consts, [(RW, BF16)], [], TR)[0]

    yg = matmul(o_gla, wb_g, "nn", F32, "branch_gla")
    yr = matmul(o_rw, wb_r, "nn", F32, "branch_rw")
    merge_fn = functools.partial(f_merge, d=D)
    merged = rowwise("merge", merge_fn, [_cols(p, 2 * D, lay["gate"]), _cols(yg), _cols(yr)], [gate_b], [(D, BF16)], [], TR)[0]
    mix = matmul(merged, W["w_out"], "nn", F32, "w_out")
    x2, h3 = rowwise("res_rms_ffn2", res_rms(1.0), [_cols(x1), _cols(mix)], [ffn2_norm], [(D, F32), (D, BF16)], [], TR)
    g3 = mm_gather(h3, W["ffn2_wg"], BF16, "ffn2_g", ["ffn2_wu"])
    u3, act3 = mm_gather(h3, W["ffn2_wu"], [BF16, BF16], "ffn2_u", ["ffn2_wd"], extras=[g3], epilogue=up_and_act)
    f3 = matmul(act3, W["ffn2_wd"], "nn", F32, "ffn2_d")

    def final_fn(a, f, t, g):
        def loss_of(a, f, g):
            yv = f_rms(a + 0.5 * f, g)
            return 0.5 * jnp.sum(jnp.mean(jnp.square(yv - t), axis=-1))
        val, vjp = jax.vjp(loss_of, a, f, g)
        da, df, dg = vjp(jnp.ones((), F32))
        return da, df, jnp.full((1, 128), val, F32), dg

    dx2, df3, loss_acc, d_final = rowwise("final_loss", final_fn, [_cols(x2), _cols(f3), _cols(tgt)], [fin_g],
                                          [(D, F32), (D, BF16)], [(1, 128), (1, D)], TR)
    grads = {"final_norm": d_final.reshape(D)}

    received = {}

    def pair_up(n, dw):
        r, cc = shard_shapes[n]
        flat = (4 * r, cc) if BIG[n] == "slab" else dw.shape
        theirs = sibling_swap("swap_" + n, [dw])[0]
        return n, (pair_sum("pair_" + n, dw.reshape(flat), theirs.reshape(flat)).reshape(dw.shape), BIG[n], r, cc)

    def mm_exchange(a, b, mode, out_dtype, name, pending):
        out, got = matmul(a, b, mode, out_dtype, name, job=exchange_job([entry for _, entry in pending]))
        received.update(zip([n for n, _ in pending], got))
        return out

    def ffn_backward(tag, h, gx, ux, act, df, wg, wu, wd, last_job=None):
        def through_act(dact, g, u):
            _, vjp = jax.vjp(f_swiglu, *f32(g, u))
            return vjp(dact)

        dgx, dux = matmul(df, wd, "nt", [BF16, BF16], tag + "_dact", extras=[gx, ux], epilogue=through_act)
        p_wd = pair_up(tag + "_wd", matmul(act, df, "tn", BF16, tag + "_dwd"))
        p_wg = pair_up(tag + "_wg", mm_exchange(h, dgx, "tn", BF16, tag + "_dwg", [p_wd]))
        p_wu = pair_up(tag + "_wu", mm_exchange(h, dux, "tn", BF16, tag + "_dwu", [p_wg]))
        dha = mm_exchange(dgx, wg, "nt", F32, tag + "_dh_g", [p_wu])
        if last_job is None:
            return dha, matmul(dux, wu, "nt", F32, tag + "_dh_u"), None
        dhb, got = matmul(dux, wu, "nt", F32, tag + "_dh_u", job=last_job)
        return dha, dhb, got

    def res_rms_bwd(name, coef, a, f, g, dx1, dha, dhb):
        def fn(a, f, dx1, dha, dhb, g):
            _, vjp = jax.vjp(res_rms(coef), a, f, g)
            return vjp((dx1, dha + dhb))

        return rowwise(name, fn, [_cols(a), _cols(f), _cols(dx1), _cols(dha), _cols(dhb)], [g],
                       [(D, F32), (D, BF16)], [(1, D)], TR)

    dh3a, dh3b, _ = ffn_backward("ffn2", h3, g3, u3, act3, df3, W["ffn2_wg"], W["ffn2_wu"], W["ffn2_wd"])
    dx1, dmix, grads["ffn2_norm"] = res_rms_bwd("res_rms_ffn2_bwd", 1.0, x1, mix, ffn2_norm, dx2, dh3a, dh3b)

    p_wo = pair_up("w_out", matmul(merged, dmix, "tn", BF16, "d_w_out"))
    dmerged = mm_exchange(dmix, W["w_out"], "nt", F32, "d_merged", [p_wo])

    def merge_bwd(gp, a, b, d, gb):
        _, vjp = jax.vjp(merge_fn, gp, a, b, gb)
        return vjp(d)

    dgate, dyg, dyr, grads["gate_b"] = rowwise(
        "merge_bwd", merge_bwd, [_cols(p, 2 * D, lay["gate"]), _cols(yg), _cols(yr), _cols(dmerged)], [gate_b],
        [(2 * D, BF16), (D, BF16), (D, BF16)], [(1, 2 * D)], TR)
    do_gla = matmul(dyg, wb_g, "nt", BF16, "d_o_gla")
    do_rw = matmul(dyr, wb_r, "nt", F32, "d_o_rw")
    p_wb = pair_up("w_branch", jnp.concatenate([matmul(o_gla, dyg, "tn", BF16, "d_wb_gla"),
                                                matmul(o_rw, dyr, "tn", BF16, "d_wb_rw")], axis=0))

    def post_bwd(yv, rv, kv, vv, gv_, d, lw, lb, rk):
        _, vjp = jax.vjp(f_rw_post, yv, rv, kv, vv, gv_, lw, lb, rk)
        return vjp(d)

    dy_, dr_p, dk2_p, dv_p, dg_p, grads["rwkv_lnx_w"], grads["rwkv_lnx_b"], d_rk = rowwise(
        "rw_post_bwd", post_bwd, [_cols(a) for a in (y_, r_, k2_, v_, g_, do_rw)], post_consts,
        [(RW, F32)] * 5, [(1, RW)] * 3, 128)
    grads["rwkv_r_k"] = d_rk.reshape(rwkv_r_k.shape[1:])

    scan_cots = rwkv_scan_backward(*(tiles(a) for a in (r_, dec_, k2_, v_, kk_, b_)), rw_states, tiles(dy_))
    dr_s, dw_s, dk2_s, dv_s, dkk_s, db_s = (a.reshape(S, RW) for a in scan_cots)

    def pre_bwd(xw, c0, c1, c2, c3, c4, c5, c6, c7, c8, c9, w0, ww, a0, wa, wg_, kk_c, ka_c):
        _, vjp = jax.vjp(pre_fn, xw, w0, ww.astype(F32), a0, wa.astype(F32), wg_.astype(F32), kk_c, ka_c)
        return vjp((c0 + c6, c1, c2 + c7, c3 + c8, c4, c5, c9))

    dxsh, grads["rwkv_w0"], d_ww2, grads["rwkv_a0"], d_wa2, d_wg2, grads["rwkv_k_k"], grads["rwkv_k_a"] = rowwise(
        "rw_pre_bwd", pre_bwd,
        [_cols(xsh)] + [_cols(a) for a in (dr_s, dw_s, dk2_s, dv_s, dkk_s, db_s, dr_p, dk2_p, dv_p, dg_p)], pre_consts,
        [(WIN, F32)], [(1, RW), w_w2_p.shape, (1, RW), w_a2_p.shape, w_g2.shape, (1, RW), (1, RW)], 128)
    grads["rwkv_w_w2"], grads["rwkv_w_a2"], grads["rwkv_w_g2"] = d_ww2[:lo_w], d_wa2[:lo_a], d_wg2

    dq, dk, dv, dr, da, d_ga2, grads["gla_b_a"], grads["gla_gn_w"] = gla_backward(
        p, lay, gla_states, do_gla, gla_a2_p, gla_b_a, gla_gn_w, gheads)
    grads["gla_w_a2"] = d_ga2[:lo_g]

    dpw, dmu_win = token_shift_backward(p, dxsh, da, mu_win, WIN, ow["a"], TR)
    dmu = dmu_win[0]
    grads["rwkv_mu"] = jnp.concatenate([dmu[ow[k_]:ow[k_] + mu_parts[k_].shape[0]] for k_ in ("rr", "rk", "rv", "wd", "ad", "gd")]).reshape(1, -1)

    dp = jnp.concatenate([dpw, dgate, dv, dr, dq, dk], axis=1)
    d_w_in = from_padded(mm_exchange(h2, dp, "tn", BF16, "d_w_in", [p_wb]))
    p_wi = pair_up("w_in", d_w_in.reshape(D, 4, shard_shapes["w_in"][1]).transpose(1, 0, 2))
    dh2 = mm_exchange(dp, w_in_p, "nt", F32, "d_h2", [p_wi])
    zeros_d = jnp.zeros_like(dh2)
    dx0, df1, grads["mix_norm"] = res_rms_bwd("res_rms_mix_bwd", 0.5, xs, f1, mix_norm, dx1, dh2, zeros_d)

    early = [n for n in REPLICATED if n != "ffn1_norm"] + LORA
    early_flat = jnp.concatenate([grads[n].reshape(-1) for n in early])
    early_rows = -(-early_flat.shape[0] // 1024) * 8
    early_vec = jnp.pad(early_flat, (0, early_rows * 128 - early_flat.shape[0])).reshape(early_rows, 128)
    dh1a, dh1b, (early_all,) = ffn_backward(
        "ffn1", h1, g1, u1, act1, df1, W["ffn1_wg"], W["ffn1_wu"], W["ffn1_wd"],
        last_job=device_gather_job(early_vec, jnp.zeros((8, early_rows, 128), F32)))

    def rms1_bwd(a, dha, dhb, dxa, g):
        _, vjp = jax.vjp(f_rms, a, g)
        da_, dg_ = vjp(dha + dhb)
        return da_ + dxa, dg_

    grad_x, grads["ffn1_norm"] = rowwise("rms1_bwd", rms1_bwd, [_cols(xs), _cols(dh1a), _cols(dh1b), _cols(dx0)],
                                         [ffn1_norm], [(D, F32)], [(1, D)], TR)

    names = list(BIG)
    halves = [chip_sum("chip_sum_" + n, received[n]) for n in names]
    others = sibling_swap("sibling_join", halves)
    final_grads, delta, new_m, new_v = {}, {}, {}, {}
    for n, h, o in zip(names, halves, others):
        res = adamw_halves(n, wts[n][0], h, o, moms[n][0], vars_[n][0])
        final_grads[n], delta[n], new_m[n], new_v[n] = (a.reshape(wts[n].shape) for a in res)

    rep_sum = sum_devices(early_all, early_vec).reshape(-1)
    late = grads["ffn1_norm"].reshape(-1, 128)
    final_grads["ffn1_norm"] = allreduce_small(jnp.pad(late, ((0, -late.shape[0] % 8), (0, 0))))[:late.shape[0]].reshape(wts["ffn1_norm"].shape)
    my_chip = 2 * lax.axis_index("x") + lax.axis_index("y")
    off = 0
    for n in early:
        size = grads[n].size
        full = rep_sum[off:off + size].reshape(grads[n].shape)
        off += size
        if n in LORA:
            cc = shard_shapes[n][1]
            full = lax.dynamic_slice_in_dim(full, my_chip * cc, cc, axis=1)
        final_grads[n] = full.reshape(wts[n].shape)

    loss = lax.psum(loss_acc[0, 0], ("x", "y", "c"))

    small = REPLICATED + LORA

    def two(a):
        return a.reshape(-1, a.shape[-1])

    ds, ms_, vs_ = adamw_small([two(wts[n]) for n in small], [two(final_grads[n]) for n in small],
                               [two(moms[n]) for n in small], [two(vars_[n]) for n in small])
    for i, n in enumerate(small):
        shp = wts[n].shape
        delta[n], new_m[n], new_v[n] = ds[i].reshape(shp), ms_[i].reshape(shp), vs_[i].reshape(shp)

    return (loss, grad_x.reshape(x.shape), *[final_grads[n] for n in WEIGHTS], *[delta[n] for n in WEIGHTS],
            *[new_m[n] for n in WEIGHTS], *[new_v[n] for n in WEIGHTS])
```

```python
import functools
import math

import jax
import jax.numpy as jnp
from jax import lax
from jax.experimental import pallas as pl
from jax.experimental.pallas import tpu as pltpu

F32 = jnp.float32
BF16 = jnp.bfloat16
MESH_IDS = pl.DeviceIdType.MESH

NORM_EPS = 1e-6
GN_EPS = 64e-5
GLA_TAU = 16.0
CHUNK = 64
GLA_DK = 128
GLA_DV = 256
RW_HD = 64
LORA_PAD = 128
GATE_LORA = 256
ADAM_LR, ADAM_B1, ADAM_B2, ADAM_EPS, ADAM_WD, ADAM_STEP = 0.001, 0.9, 0.999, 1e-08, 0.01, 10

VMEM_LIMIT_BYTES = 56 * 1024 * 1024
HBM_SPEC = pl.BlockSpec(memory_space=pltpu.HBM)


def _cparams(sem=None):
    return pltpu.CompilerParams(dimension_semantics=sem, vmem_limit_bytes=VMEM_LIMIT_BYTES)


def _pick(n, target, mult=128):
    best = None
    for t in range(mult, min(n, target) + 1, mult):
        if n % t == 0:
            best = t
    return best if best is not None else n


_NN = (((1,), (0,)), ((), ()))
_NT = (((1,), (1,)), ((), ()))
_TN = (((0,), (0,)), ((), ()))


def _dg(a, b, dims):
    return lax.dot_general(a.astype(BF16), b.astype(BF16), dims, preferred_element_type=F32)


@jax.custom_vjp
def mm_nn(a, b):
    return _dg(a, b, _NN)


def _mm_nn_fwd(a, b):
    return _dg(a, b, _NN), (a, b)


def _mm_nn_bwd(res, g):
    a, b = res
    return _dg(g, b, _NT).astype(a.dtype), _dg(a, g, _TN).astype(b.dtype)


mm_nn.defvjp(_mm_nn_fwd, _mm_nn_bwd)


@jax.custom_vjp
def mm_nt(a, b):
    return _dg(a, b, _NT)


def _mm_nt_fwd(a, b):
    return _dg(a, b, _NT), (a, b)


def _mm_nt_bwd(res, g):
    a, b = res
    return _dg(g, b, _NN).astype(a.dtype), _dg(g, a, _TN).astype(b.dtype)


mm_nt.defvjp(_mm_nt_fwd, _mm_nt_bwd)


@jax.custom_vjp
def mm_tn(a, b):
    return _dg(a, b, _TN)


def _mm_tn_fwd(a, b):
    return _dg(a, b, _TN), (a, b)


def _mm_tn_bwd(res, g):
    a, b = res
    return _dg(b, g, _NT).astype(a.dtype), _dg(a, g, _NN).astype(b.dtype)


mm_tn.defvjp(_mm_tn_fwd, _mm_tn_bwd)


def _split3(x):
    h = x.astype(BF16)
    r = x - h.astype(F32)
    m = r.astype(BF16)
    l = (r - m.astype(F32)).astype(BF16)
    return h, m, l


def _block_ones(n, seg):
    i = lax.broadcasted_iota(jnp.int32, (n, n), 0) // seg
    j = lax.broadcasted_iota(jnp.int32, (n, n), 1) // seg
    return (i == j).astype(BF16)


def _segsum_raw(x, seg, terms):
    ones = _block_ones(128, seg)
    outs = []
    for j in range(x.shape[1] // 128):
        t = x[:, j * 128:(j + 1) * 128]
        parts = _split3(t)[:terms]
        acc = jnp.dot(parts[0], ones, preferred_element_type=F32)
        for p_ in parts[1:]:
            acc = acc + jnp.dot(p_, ones, preferred_element_type=F32)
        outs.append(acc)
    return outs[0] if len(outs) == 1 else jnp.concatenate(outs, axis=1)


@jax.custom_vjp
def segsum64(x):
    return _segsum_raw(x, RW_HD, 3)


segsum64.defvjp(lambda x: (_segsum_raw(x, RW_HD, 3), None), lambda _, g: (_segsum_raw(g, RW_HD, 3),))


def _tri(n, upper):
    i = lax.broadcasted_iota(jnp.int32, (n, n), 0)
    j = lax.broadcasted_iota(jnp.int32, (n, n), 1)
    return ((i <= j) if upper else (i >= j)).astype(BF16)


def _tri_mm(x, upper):
    t = _tri(x.shape[0], upper)
    h, m, l = _split3(x)
    return (jnp.dot(t, h, preferred_element_type=F32) + jnp.dot(t, m, preferred_element_type=F32)
            + jnp.dot(t, l, preferred_element_type=F32))


@jax.custom_vjp
def cumsum_rows(x):
    return _tri_mm(x, False)


cumsum_rows.defvjp(lambda x: (_tri_mm(x, False), None), lambda _, g: (_tri_mm(g, True),))


def _make_split(sizes, axis):
    offs = [sum(sizes[:i]) for i in range(len(sizes))]

    def cut(x):
        if axis == 1:
            return tuple(x[:, o:o + s] for o, s in zip(offs, sizes))
        return tuple(x[o:o + s, :] for o, s in zip(offs, sizes))

    @jax.custom_vjp
    def split(x):
        return cut(x)

    split.defvjp(lambda x: (cut(x), None), lambda _, gs: (jnp.concatenate(gs, axis=axis),))
    return split


@jax.custom_vjp
def log_sigmoid(z):
    return jnp.minimum(z, 0.0) - jnp.log(1.0 + jnp.exp(-jnp.abs(z)))


log_sigmoid.defvjp(lambda z: (log_sigmoid(z), z), lambda z, g: (g * (1.0 - jax.nn.sigmoid(z)),))


def silu(x):
    return x * jax.nn.sigmoid(x)


def matmul(a, b, mode, out_dtype, name, tm=1024, tn=512, tk=2048, job=None, extras=(), epilogue=None):
    if mode == "nn":
        (M, K), (K2, N) = a.shape, b.shape
    elif mode == "nt":
        (M, K), (N, K2) = a.shape, b.shape
    else:
        (K, M), (K2, N) = a.shape, b.shape
    assert K == K2, (name, a.shape, b.shape)
    tm, tn, tk = _pick(M, tm), _pick(N, tn), _pick(K, tk)
    grid = (M // tm, N // tn, K // tk)
    dims = {"nn": _NN, "nt": _NT, "tn": _TN}[mode]
    a_spec = pl.BlockSpec((tk, tm), lambda i, j, k: (k, i)) if mode == "tn" else pl.BlockSpec((tm, tk), lambda i, j, k: (i, k))
    b_spec = pl.BlockSpec((tn, tk), lambda i, j, k: (j, k)) if mode == "nt" else pl.BlockSpec((tk, tn), lambda i, j, k: (k, j))
    n_in = 0 if job is None else len(job.operands)
    n_out = 0 if job is None else len(job.out_shapes)
    n_ex = len(extras)
    main_dtypes = [out_dtype] if epilogue is None else list(out_dtype)
    n_main = len(main_dtypes)

    def body(a_ref, b_ref, *rest):
        ex_refs, rest = rest[:n_ex], rest[n_ex:]
        job_ins, o_refs, job_outs = rest[:n_in], rest[n_in:n_in + n_main], rest[n_in + n_main:n_in + n_main + n_out]
        acc_ref, sems = rest[n_in + n_main + n_out], rest[n_in + n_main + n_out + 1:]
        i, j, k = pl.program_id(0), pl.program_id(1), pl.program_id(2)
        if job is not None:
            @pl.when((i == 0) & (j == 0) & (k == 0))
            def _():
                job.start(job_ins, job_outs, sems)

        part = _dg(a_ref[...], b_ref[...], dims)

        @pl.when(k == 0)
        def _():
            acc_ref[...] = part

        @pl.when(k > 0)
        def _():
            acc_ref[...] += part

        @pl.when(k == grid[2] - 1)
        def _():
            acc = acc_ref[...]
            vals = (acc,) if epilogue is None else epilogue(acc, *[e[...] for e in ex_refs])
            for o_ref, val in zip(o_refs, vals):
                o_ref[...] = val.astype(o_ref.dtype)

        if job is not None:
            @pl.when((i == grid[0] - 1) & (j == grid[1] - 1) & (k == grid[2] - 1))
            def _():
                job.finish(job_ins, job_outs, sems)

    main_spec = pl.BlockSpec((tm, tn), lambda i, j, k: (i, j))
    job_operands = [] if job is None else list(job.operands)
    aliases = {} if job is None else {2 + n_ex + op: n_main + out for op, out in job.aliases.items()}
    res = pl.pallas_call(
        body, name=name, grid=grid, in_specs=[a_spec, b_spec] + [main_spec] * n_ex + [HBM_SPEC] * n_in,
        out_specs=[main_spec] * n_main + [HBM_SPEC] * n_out,
        out_shape=[jax.ShapeDtypeStruct((M, N), dt) for dt in main_dtypes] + ([] if job is None else list(job.out_shapes)),
        scratch_shapes=[pltpu.VMEM((tm, tn), F32)] + ([] if job is None else list(job.scratch)),
        input_output_aliases=aliases,
        compiler_params=_cparams(("parallel", "parallel", "arbitrary") if job is None else ("arbitrary",) * 3),
    )(a, b, *extras, *job_operands)
    mains = res[0] if epilogue is None else tuple(res[:n_main])
    return mains if job is None else (mains, res[n_main:])


def _cols(arr, width=None, off=0):
    width = arr.shape[1] if width is None else width
    assert off % width == 0, (off, width)
    return (arr, width, off // width)


def rowwise(name, fn, rows, consts, row_outs, acc_outs, tr, extra_specs=()):
    S = rows[0][0].shape[0]
    tr = min(tr, S)
    assert S % tr == 0
    n_in = len(rows) + len(extra_specs) + len(consts)
    n_ro = len(row_outs)
    in_specs = [pl.BlockSpec((tr, w), functools.partial(lambda i, cb: (i, cb), cb=cb)) for (_, w, cb) in rows]
    in_specs += [spec for (_, spec) in extra_specs]
    in_specs += [pl.BlockSpec(c.shape, lambda i: (0, 0)) for c in consts]
    out_shape = [jax.ShapeDtypeStruct((S, w), dt) for (w, dt) in row_outs]
    out_shape += [jax.ShapeDtypeStruct(shp, F32) for shp in acc_outs]
    out_specs = [pl.BlockSpec((tr, w), lambda i: (i, 0)) for (w, _) in row_outs]
    out_specs += [pl.BlockSpec(shp, lambda i: (0, 0)) for shp in acc_outs]

    def body(*refs):
        ins = [r[...] for r in refs[:n_in]]
        outs = fn(*ins)
        outs = outs if isinstance(outs, (tuple, list)) else (outs,)
        assert len(outs) == n_ro + len(acc_outs), (name, len(outs))
        for o_ref, val in zip(refs[n_in:n_in + n_ro], outs[:n_ro]):
            o_ref[...] = val.astype(o_ref.dtype)
        i = pl.program_id(0)
        for a_ref, val in zip(refs[n_in + n_ro:], outs[n_ro:]):
            @pl.when(i == 0)
            def _(a_ref=a_ref, val=val):
                a_ref[...] = val.astype(F32)

            @pl.when(i > 0)
            def _(a_ref=a_ref, val=val):
                a_ref[...] += val.astype(F32)

    res = pl.pallas_call(
        body, name=name, grid=(S // tr,), in_specs=in_specs, out_specs=out_specs, out_shape=out_shape,
        compiler_params=_cparams(("arbitrary",) if acc_outs else ("parallel",)),
    )(*[r[0] for r in rows], *[e[0] for e in extra_specs], *consts)
    return res


def f32(*xs):
    return [x.astype(F32) for x in xs]


def f_rms(x, g):
    return x * lax.rsqrt(jnp.mean(x * x, axis=-1, keepdims=True) + NORM_EPS) * g


def f_swiglu(gx, ux):
    return silu(gx) * ux


def f_merge(gp, yg, yr, gate_b, d):
    gates = jax.nn.sigmoid(gp + gate_b)
    g1, g2 = _make_split((d, d), 1)(gates)
    return g1 * yg + g2 * yr


def rw_window(rw):
    o = dict(rr=0, rk=rw, rv=2 * rw, gd=3 * rw, wd=3 * rw + GATE_LORA)
    o["ad"] = o["wd"] + LORA_PAD
    o["a"] = o["ad"] + LORA_PAD
    o["used"] = o["a"] + LORA_PAD
    return o


def f_rw_pre(xs, w0, w_w2, a0, w_a2, w_g2, k_k, k_a, rw):
    win = xs.shape[1]
    o = rw_window(rw)
    sizes = (rw, rw, rw, GATE_LORA, LORA_PAD, LORA_PAD, win - o["a"])
    rr, rk, rv, gd, wd, ad, _ = _make_split(sizes, 1)(xs)
    w_raw = w0 + mm_nn(jnp.tanh(wd), w_w2)
    dec = jnp.exp(-jnp.exp(log_sigmoid(w_raw) - 0.5))
    a = jax.nn.sigmoid(a0 + mm_nn(ad, w_a2))
    g = mm_nn(jax.nn.sigmoid(gd), w_g2)
    kx = rk * k_k
    kk = kx / jnp.maximum(jnp.sqrt(segsum64(kx * kx)), 1e-12)
    k2 = rk * (1.0 + (a - 1.0) * k_a)
    return rr, dec, k2, rv, kk, kk * a, g


def f_rw_post(y, r, k2, v, g, lnx_w, lnx_b, r_k):
    mu = segsum64(y) * (1.0 / RW_HD)
    yc = y - mu
    var = segsum64(yc * yc) * (1.0 / RW_HD)
    yn = yc * lax.rsqrt(var + GN_EPS) * lnx_w + lnx_b
    bonus = segsum64(r * k2 * r_k) * v
    return (yn + bonus) * g


def f_gla_chunk(q, k, v, r, a, st_prev, w_a2, b_a, gn_w, heads):
    z = mm_nn(a, w_a2) + b_a
    la = log_sigmoid(z) * (1.0 / GLA_TAU)
    cum = cumsum_rows(la)
    total = jnp.sum(la, axis=0, keepdims=True)
    kdec = k * jnp.exp(total - cum)
    et = jnp.exp(total)
    qs = q * (GLA_DK ** -0.5)
    sk = _make_split((GLA_DK,) * heads, 1)
    sv = _make_split((GLA_DV,) * heads, 1)
    ss = _make_split((GLA_DV,) * heads, 0)
    kd_h, q_h, et_h, v_h, st_h = sk(kdec), sk(qs), sk(et), sv(v), ss(st_prev)
    outs, news = [], []
    for h in range(heads):
        st_new = st_h[h] * et_h[h] + mm_tn(v_h[h], kd_h[h])
        o = mm_nt(q_h[h], st_new)
        o = o * lax.rsqrt(jnp.mean(o * o, axis=-1, keepdims=True) + NORM_EPS) * gn_w
        outs.append(o)
        news.append(st_new)
    o_all = outs[0] if heads == 1 else jnp.concatenate(outs, axis=1)
    st_all = news[0] if heads == 1 else jnp.concatenate(news, axis=0)
    return o_all * silu(r), st_all


def gla_forward(p, lay, w_a2, b_a, gn_w, heads):
    S = p.shape[0]
    nc = S // CHUNK
    gq, gv = heads * GLA_DK, heads * GLA_DV

    def spec(width, off, rev=False):
        assert off % width == 0
        return pl.BlockSpec((CHUNK, width), functools.partial(lambda n, cb: (n, cb), cb=off // width))

    def body(q_ref, k_ref, v_ref, r_ref, a_ref, w_ref, b_ref, g_ref, o_ref, st_out_ref, st_sc):
        @pl.when(pl.program_id(0) == 0)
        def _():
            st_sc[...] = jnp.zeros_like(st_sc)

        st_prev = st_sc[...]
        st_out_ref[0] = st_prev
        o, st_new = f_gla_chunk(*f32(q_ref[...], k_ref[...], v_ref[...], r_ref[...], a_ref[...]), st_prev,
                                w_ref[...], b_ref[...], g_ref[...], heads)
        o_ref[...] = o.astype(o_ref.dtype)
        st_sc[...] = st_new

    return pl.pallas_call(
        body, name="gla_fwd", grid=(nc,),
        in_specs=[spec(gq, lay["q"]), spec(gq, lay["k"]), spec(gv, lay["v"]), spec(gv, lay["r"]), spec(LORA_PAD, lay["a"]),
                  pl.BlockSpec(w_a2.shape, lambda n: (0, 0)), pl.BlockSpec(b_a.shape, lambda n: (0, 0)),
                  pl.BlockSpec(gn_w.shape, lambda n: (0, 0))],
        out_specs=[pl.BlockSpec((CHUNK, gv), lambda n: (n, 0)), pl.BlockSpec((1, gv, GLA_DK), lambda n: (n, 0, 0))],
        out_shape=[jax.ShapeDtypeStruct((S, gv), BF16), jax.ShapeDtypeStruct((nc, gv, GLA_DK), F32)],
        scratch_shapes=[pltpu.VMEM((gv, GLA_DK), F32)],
        compiler_params=_cparams(("arbitrary",)),
    )(p, p, p, p, p, w_a2, b_a, gn_w)


def gla_backward(p, lay, states, d_out, w_a2, b_a, gn_w, heads):
    S = p.shape[0]
    nc = S // CHUNK
    gq, gv = heads * GLA_DK, heads * GLA_DV

    def spec(width, off):
        assert off % width == 0
        return pl.BlockSpec((CHUNK, width), functools.partial(lambda n, cb: (nc - 1 - n, cb), cb=off // width))

    def rev(width):
        return pl.BlockSpec((CHUNK, width), lambda n: (nc - 1 - n, 0))

    def whole(arr):
        return pl.BlockSpec(arr.shape, lambda n: (0, 0))

    def body(q_ref, k_ref, v_ref, r_ref, a_ref, st_ref, do_ref, w_ref, b_ref, g_ref,
             dq_ref, dk_ref, dv_ref, dr_ref, da_ref, dw_ref, db_ref, dg_ref, dst_sc):
        n = pl.program_id(0)

        @pl.when(n == 0)
        def _():
            dst_sc[...] = jnp.zeros_like(dst_sc)

        fn = functools.partial(f_gla_chunk, heads=heads)
        prim = (*f32(q_ref[...], k_ref[...], v_ref[...], r_ref[...], a_ref[...]), st_ref[0],
                w_ref[...].astype(F32), b_ref[...], g_ref[...])
        _, vjp = jax.vjp(fn, *prim)
        dq, dk, dv, dr, da, dst, dw, db, dg = vjp((do_ref[...].astype(F32), dst_sc[...]))
        for ref, val in ((dq_ref, dq), (dk_ref, dk), (dv_ref, dv), (dr_ref, dr), (da_ref, da)):
            ref[...] = val.astype(ref.dtype)
        dst_sc[...] = dst

        @pl.when(n == 0)
        def _():
            dw_ref[...] = dw
            db_ref[...] = db
            dg_ref[...] = dg

        @pl.when(n > 0)
        def _():
            dw_ref[...] += dw
            db_ref[...] += db
            dg_ref[...] += dg

    return pl.pallas_call(
        body, name="gla_bwd", grid=(nc,),
        in_specs=[spec(gq, lay["q"]), spec(gq, lay["k"]), spec(gv, lay["v"]), spec(gv, lay["r"]), spec(LORA_PAD, lay["a"]),
                  pl.BlockSpec((1, gv, GLA_DK), lambda n: (nc - 1 - n, 0, 0)), rev(gv),
                  whole(w_a2), whole(b_a), whole(gn_w)],
        out_specs=[rev(gq), rev(gq), rev(gv), rev(gv), rev(LORA_PAD), whole(w_a2), whole(b_a), whole(gn_w)],
        out_shape=[jax.ShapeDtypeStruct((S, gq), BF16), jax.ShapeDtypeStruct((S, gq), BF16),
                   jax.ShapeDtypeStruct((S, gv), BF16), jax.ShapeDtypeStruct((S, gv), BF16),
                   jax.ShapeDtypeStruct((S, LORA_PAD), F32),
                   jax.ShapeDtypeStruct(w_a2.shape, F32), jax.ShapeDtypeStruct(b_a.shape, F32),
                   jax.ShapeDtypeStruct(gn_w.shape, F32)],
        scratch_shapes=[pltpu.VMEM((gv, GLA_DK), F32)],
        compiler_params=_cparams(("arbitrary",)),
    )(p, p, p, p, p, states, d_out, w_a2, b_a, gn_w)


SCAN_BLOCK = 32


def _scan_helpers(nt):
    ones = _block_ones(128, RW_HD)
    rows = lax.broadcasted_iota(jnp.int32, (nt * RW_HD, 128), 0) % RW_HD
    lanes = lax.broadcasted_iota(jnp.int32, (nt * RW_HD, 128), 1) % RW_HD
    eye = rows == lanes

    def bc(ref, t):
        parts = [jnp.broadcast_to(ref[t, j:j + 1, :], (RW_HD, 128)) for j in range(nt)]
        return parts[0] if nt == 1 else jnp.concatenate(parts, axis=0)

    def seg1(x):
        return jnp.dot(x.astype(BF16), ones, preferred_element_type=F32)

    def column(ref, t):
        return seg1(jnp.where(eye, bc(ref, t), 0.0))

    def put_diag(ref, t, x):
        put_colsum(ref, t, jnp.where(eye, x, 0.0))

    def put_colsum(ref, t, x, sign=1.0):
        for j in range(nt):
            ref[t, j:j + 1, :] = sign * jnp.sum(x[j * RW_HD:(j + 1) * RW_HD, :], axis=0, keepdims=True)

    return bc, seg1, column, put_diag, put_colsum


def rwkv_scan_forward(r, w, k2, v, kk, b):
    S, nt, _ = r.shape
    tb = min(SCAN_BLOCK, S)

    def body(r_ref, w_ref, k2_ref, v_ref, kk_ref, b_ref, y_ref, st_ref, s_sc):
        @pl.when(pl.program_id(0) == 0)
        def _():
            s_sc[...] = jnp.zeros_like(s_sc)

        bc, seg1, column, put_diag, put_colsum = _scan_helpers(nt)

        def step(t, carry):
            s = s_sc[...]
            sa_e = seg1(s * bc(kk_ref, t))
            s = s * bc(w_ref, t) - sa_e * bc(b_ref, t) + column(v_ref, t) * bc(k2_ref, t)
            s_sc[...] = s
            st_ref[t] = s
            put_diag(y_ref, t, seg1(s * bc(r_ref, t)))
            return carry

        lax.fori_loop(0, tb, step, 0, unroll=8)

    row = pl.BlockSpec((tb, nt, 128), lambda i: (i, 0, 0))
    return pl.pallas_call(
        body, name="rwkv_scan_fwd", grid=(S // tb,),
        in_specs=[row] * 6,
        out_specs=[row, pl.BlockSpec((tb, nt * RW_HD, 128), lambda i: (i, 0, 0))],
        out_shape=[jax.ShapeDtypeStruct((S, nt, 128), F32), jax.ShapeDtypeStruct((S, nt * RW_HD, 128), F32)],
        scratch_shapes=[pltpu.VMEM((nt * RW_HD, 128), F32)],
        compiler_params=_cparams(("arbitrary",)),
    )(r, w, k2, v, kk, b)


def rwkv_scan_backward(r, w, k2, v, kk, b, states, dy):
    S, nt, _ = r.shape
    tb = min(SCAN_BLOCK, S)
    nb = S // tb

    def body(r_ref, w_ref, k2_ref, v_ref, kk_ref, b_ref, st_ref, edge_ref, dy_ref,
             dr_ref, dw_ref, dk2_ref, dv_ref, dkk_ref, db_ref, ds_sc, before_sc):
        @pl.when(pl.program_id(0) == 0)
        def _():
            ds_sc[...] = jnp.zeros_like(ds_sc)

        before_sc[...] = jnp.where(pl.program_id(0) == nb - 1, 0.0, edge_ref[0])

        bc, seg1, column, put_diag, put_colsum = _scan_helpers(nt)

        def step(i, carry):
            t = tb - 1 - i
            s_prev = jnp.where(t == 0, before_sc[...], st_ref[jnp.maximum(t - 1, 0)])
            r_e, w_e, k2_e, kk_e, b_e = (bc(ref, t) for ref in (r_ref, w_ref, k2_ref, kk_ref, b_ref))
            v_e = column(v_ref, t)
            sa_e = seg1(s_prev * kk_e)
            dy_e = column(dy_ref, t)
            put_colsum(dr_ref, t, st_ref[t] * dy_e)
            ds = ds_sc[...] + dy_e * r_e
            put_colsum(dw_ref, t, ds * s_prev)
            nsa_e = seg1(ds * b_e)
            put_colsum(db_ref, t, ds * sa_e, -1.0)
            put_diag(dv_ref, t, seg1(ds * k2_e))
            put_colsum(dk2_ref, t, ds * v_e)
            put_colsum(dkk_ref, t, s_prev * nsa_e, -1.0)
            ds_sc[...] = ds * w_e - nsa_e * kk_e
            return carry

        lax.fori_loop(0, tb, step, 0, unroll=8)

    row = pl.BlockSpec((tb, nt, 128), lambda i: (nb - 1 - i, 0, 0))
    return pl.pallas_call(
        body, name="rwkv_scan_bwd", grid=(nb,),
        in_specs=[row] * 6 + [pl.BlockSpec((tb, nt * RW_HD, 128), lambda i: (nb - 1 - i, 0, 0)),
                              pl.BlockSpec((1, nt * RW_HD, 128), lambda i: (jnp.maximum((nb - 1 - i) * tb - 1, 0), 0, 0)), row],
        out_specs=[row] * 6,
        out_shape=[jax.ShapeDtypeStruct((S, nt, 128), F32)] * 6,
        scratch_shapes=[pltpu.VMEM((nt * RW_HD, 128), F32), pltpu.VMEM((nt * RW_HD, 128), F32)],
        compiler_params=_cparams(("arbitrary",)),
    )(r, w, k2, v, kk, b, states, states, dy)


def _edge_spec(width, col_block, tr, n_rows, after):
    last = n_rows // 8 - 1
    if after:
        return pl.BlockSpec((8, width), lambda i: (jnp.minimum((i + 1) * (tr // 8), last), col_block))
    return pl.BlockSpec((8, width), lambda i: (jnp.maximum(i * (tr // 8) - 1, 0), col_block))


def _shifted_prev(p, prev8):
    first = jnp.where(pl.program_id(0) == 0, 0.0, prev8[7:8, :])
    rows = lax.broadcasted_iota(jnp.int32, p.shape, 0)
    return jnp.where(rows == 0, first, pltpu.roll(p, 1, axis=0))


def token_shift_forward(p, mu_win, win, tr):
    def fn(pw, prev8, mu):
        return pw + mu * (_shifted_prev(pw, prev8) - pw)

    return rowwise("token_shift_fwd", fn, [_cols(p, win, 0)], [mu_win], [(win, F32)], [], tr,
                   extra_specs=[(p, _edge_spec(win, 0, tr, p.shape[0], False))])[0]


def token_shift_backward(p, dxs, da_gla, mu_win, win, a_off, tr):
    S = p.shape[0]
    n = S // min(tr, S)

    def fn(pw, dx, da, prev8, next8, mu):
        trr = pw.shape[0]
        last = jnp.where(pl.program_id(0) == n - 1, 0.0, next8[0:1, :])
        rows = lax.broadcasted_iota(jnp.int32, dx.shape, 0)
        dnext = jnp.where(rows == trr - 1, last, pltpu.roll(dx, trr - 1, axis=0))
        dp = (1.0 - mu) * dx + mu * dnext
        dp = jnp.concatenate([dp[:, :a_off], dp[:, a_off:a_off + LORA_PAD] + da, dp[:, a_off + LORA_PAD:]], axis=1)
        dmu = jnp.sum(dx * (_shifted_prev(pw, prev8) - pw), axis=0, keepdims=True)
        return dp, dmu

    return rowwise("token_shift_bwd", fn, [_cols(p, win, 0), _cols(dxs), _cols(da_gla)], [mu_win],
                   [(win, BF16)], [(1, win)], tr,
                   extra_specs=[(p, _edge_spec(win, 0, tr, S, False)), (dxs, _edge_spec(win, 0, tr, S, True))])


def _adamw_math(w, g, m, v):
    m = ADAM_B1 * m + (1.0 - ADAM_B1) * g
    v = ADAM_B2 * v + (1.0 - ADAM_B2) * (g * g)
    m_hat = m / (1.0 - ADAM_B1 ** ADAM_STEP)
    v_hat = v / (1.0 - ADAM_B2 ** ADAM_STEP)
    delta = -ADAM_LR * (m_hat / (jnp.sqrt(v_hat) + ADAM_EPS) + ADAM_WD * w)
    return delta, m, v


def adamw_small(ws, gs, ms, vs):
    n = len(ws)

    def body(*refs):
        for i in range(n):
            d, m, v = _adamw_math(refs[i][...], refs[n + i][...], refs[2 * n + i][...], refs[3 * n + i][...])
            refs[4 * n + i][...] = d
            refs[5 * n + i][...] = m
            refs[6 * n + i][...] = v

    shapes = [jax.ShapeDtypeStruct(w.shape, F32) for w in ws]
    outs = pl.pallas_call(body, name="adamw_small", out_shape=shapes * 3, compiler_params=_cparams())(*ws, *gs, *ms, *vs)
    return outs[:n], outs[n:2 * n], outs[2 * n:]


def _place():
    x, y, c = lax.axis_index("x"), lax.axis_index("y"), lax.axis_index("c")
    chips = [(1 - x, y), (x, 1 - y), (1 - x, 1 - y)]
    return x, y, c, chips


def _full_shape(kind, r, c):
    return {"col": (r, 4 * c), "row": (4 * r, c), "slab": (4, r, c)}[kind]


def _slab(ref, kind, k, r, c, half=None):
    n, off = (r, 0) if half is None else (r // 2, half * (r // 2))
    if kind == "col":
        return ref.at[pl.ds(off, n), pl.ds(k * c, c)]
    if kind == "row":
        return ref.at[pl.ds(k * r + off, n), :]
    return ref.at[k, pl.ds(off, n), :]


def _remote(src, dst, sems, idx, to):
    return pltpu.make_async_remote_copy(src_ref=src, dst_ref=dst, send_sem=sems[0].at[idx], recv_sem=sems[1].at[idx],
                                        device_id=to, device_id_type=MESH_IDS)


class CommJob:
    def __init__(self, operands, out_shapes, scratch, start, finish, aliases=None):
        self.operands, self.out_shapes, self.scratch, self.start, self.finish = operands, out_shapes, scratch, start, finish
        self.aliases = aliases or {}


def device_gather_job(vec, zeros8):
    flips = [(fx, fy, fc) for fx in (0, 1) for fy in (0, 1) for fc in (0, 1) if (fx, fy, fc) != (0, 0, 0)]

    def sends(srcs, outs, sems):
        x, y, c, _ = _place()
        return [_remote(srcs[0], outs[0].at[4 * x + 2 * y + c], sems, (j,), (x ^ fx, y ^ fy, c ^ fc))
                for j, (fx, fy, fc) in enumerate(flips)]

    def start(srcs, outs, sems):
        for cp in sends(srcs, outs, sems):
            cp.start()

    def finish(srcs, outs, sems):
        x, y, c, _ = _place()
        for j, (fx, fy, fc) in enumerate(flips):
            blk = outs[0].at[4 * (x ^ fx) + 2 * (y ^ fy) + (c ^ fc)]
            _remote(blk, blk, sems, (j,), (x, y, c)).wait_recv()
        for cp in sends(srcs, outs, sems):
            cp.wait_send()

    dma = pltpu.SemaphoreType.DMA
    return CommJob([vec, zeros8], [jax.ShapeDtypeStruct(zeros8.shape, F32)], [dma((7,)), dma((7,))], start, finish, {1: 0})


def sum_devices(gathered, own):
    _, R, C = gathered.shape
    tr = _pick(R, 2048, 8)

    def body(g_ref, own_ref, o_ref):
        me = 4 * lax.axis_index("x") + 2 * lax.axis_index("y") + lax.axis_index("c")
        acc = jnp.where(me == 0, own_ref[...], g_ref[0])
        for k in range(1, 8):
            acc = acc + jnp.where(me == k, own_ref[...], g_ref[k])
        o_ref[...] = acc

    return pl.pallas_call(body, name="sum_devices", grid=(R // tr,),
                          in_specs=[pl.BlockSpec((8, tr, C), lambda i: (0, i, 0)), pl.BlockSpec((tr, C), lambda i: (i, 0))],
                          out_specs=pl.BlockSpec((tr, C), lambda i: (i, 0)), out_shape=jax.ShapeDtypeStruct((R, C), F32),
                          compiler_params=_cparams(("parallel",)))(gathered, own)


def gather_job(big, small=()):
    big, small = list(big), list(small)
    nb, ns = len(big), len(small)
    meta = [(kind, *a.shape) for a, kind in big + small]

    def sends(srcs, outs, sems):
        own_s, own_r, ici_s, ici_r, _, _, sm_s, sm_r = sems
        x, y, c, chips = _place()
        me, sib = 2 * x + y, (x, y, 1 - c)
        cps = []
        for a in range(nb):
            kind, r, cc = meta[a]
            for j, chip in enumerate(chips):
                cps.append(_remote(srcs[a].at[pl.ds(c * (r // 2), r // 2)], _slab(outs[a], kind, me, r, cc, c),
                                   (ici_s, ici_r), (a, j), (*chip, c)))
        for a in range(nb):
            kind, r, cc = meta[a]
            cps.append(_remote(srcs[a], _slab(outs[a], kind, me, r, cc), (own_s, own_r), (a,), sib))
        for s in range(ns):
            kind, r, cc = meta[nb + s]
            for t, to in enumerate([sib] + [(*chip, c) for chip in chips]):
                cps.append(_remote(srcs[nb + s], _slab(outs[nb + s], kind, me, r, cc), (sm_s, sm_r), (s, t), to))
        return cps

    def start(srcs, outs, sems):
        for cp in sends(srcs, outs, sems):
            cp.start()

    def finish(srcs, outs, sems):
        own_s, own_r, ici_s, ici_r, fwd_s, fwd_r, sm_s, sm_r = sems
        x, y, c, chips = _place()
        me, sib = 2 * x + y, (x, y, 1 - c)
        cids = [2 * chip[0] + chip[1] for chip in chips]
        hands = []
        for a in range(nb):
            kind, r, cc = meta[a]
            for j in range(3):
                blk = _slab(outs[a], kind, cids[j], r, cc, c)
                _remote(blk, blk, (ici_s, ici_r), (a, j), sib).wait_recv()
                hands.append(_remote(blk, blk, (fwd_s, fwd_r), (a, j), sib))
                hands[-1].start()
        for a in range(nb):
            kind, r, cc = meta[a]
            for j in range(3):
                blk = _slab(outs[a], kind, cids[j], r, cc, 1 - c)
                _remote(blk, blk, (fwd_s, fwd_r), (a, j), sib).wait_recv()
            blk = _slab(outs[a], kind, me, r, cc)
            _remote(blk, blk, (own_s, own_r), (a,), sib).wait_recv()
        for s in range(ns):
            kind, r, cc = meta[nb + s]
            for t, frm in enumerate([me] + cids):
                blk = _slab(outs[nb + s], kind, frm, r, cc)
                _remote(blk, blk, (sm_s, sm_r), (s, t), sib).wait_recv()
        for cp in sends(srcs, outs, sems) + hands:
            cp.wait_send()

    dma = pltpu.SemaphoreType.DMA
    nb1, ns1 = max(nb, 1), max(ns, 1)
    return CommJob([a for a, _ in big + small],
                   [jax.ShapeDtypeStruct(_full_shape(kind, r, cc), BF16) for (kind, r, cc) in meta],
                   [dma((nb1,)), dma((nb1,)), dma((nb1, 3)), dma((nb1, 3)), dma((nb1, 3)), dma((nb1, 3)),
                    dma((ns1, 4)), dma((ns1, 4))], start, finish)


def run_job(name, job):
    n_in, n_out = len(job.operands), len(job.out_shapes)

    def body(*refs):
        ins, outs, sems = refs[:n_in], refs[n_in:n_in + n_out], refs[n_in + n_out:]
        job.start(ins, outs, sems)
        job.finish(ins, outs, sems)

    return pl.pallas_call(body, name=name, in_specs=[HBM_SPEC] * n_in, out_specs=[HBM_SPEC] * n_out,
                          out_shape=job.out_shapes, scratch_shapes=job.scratch)(*job.operands)


def sibling_swap(name, arrays):
    n = len(arrays)

    def body(*refs):
        srcs, outs, sems = refs[:n], refs[n:2 * n], refs[2 * n:]
        x, y, c, _ = _place()
        cps = [_remote(srcs[a], outs[a], sems, (a,), (x, y, 1 - c)) for a in range(n)]
        for cp in cps:
            cp.start()
        for cp in cps:
            cp.wait_recv()
        for cp in cps:
            cp.wait_send()

    return pl.pallas_call(
        body, name=name, in_specs=[HBM_SPEC] * n, out_specs=[HBM_SPEC] * n,
        out_shape=[jax.ShapeDtypeStruct(a.shape, a.dtype) for a in arrays],
        scratch_shapes=[pltpu.SemaphoreType.DMA((n,)), pltpu.SemaphoreType.DMA((n,))],
    )(*arrays)


def exchange_job(sums):
    n = len(sums)

    def sends(srcs, outs, sems):
        ici_s, ici_r, sib_s, sib_r = sems
        x, y, c, chips = _place()
        me, sib = 2 * x + y, (x, y, 1 - c)
        cps = []
        for a, (_, kind, r, cc) in enumerate(sums):
            for j, chip in enumerate(chips):
                cid = 2 * chip[0] + chip[1]
                cps.append(_remote(_slab(srcs[a], kind, cid, r, cc, c), outs[a].at[me], (ici_s, ici_r), (a, j), (*chip, c)))
            cps.append(_remote(_slab(srcs[a], kind, me, r, cc, 1 - c), outs[a].at[me], (sib_s, sib_r), (a,), sib))
        return cps

    def start(srcs, outs, sems):
        for cp in sends(srcs, outs, sems):
            cp.start()

    def finish(srcs, outs, sems):
        ici_s, ici_r, sib_s, sib_r = sems
        x, y, c, chips = _place()
        me, sib = 2 * x + y, (x, y, 1 - c)
        for a in range(n):
            for j, chip in enumerate(chips):
                blk = outs[a].at[2 * chip[0] + chip[1]]
                _remote(blk, blk, (ici_s, ici_r), (a, j), sib).wait_recv()
            _remote(outs[a].at[me], outs[a].at[me], (sib_s, sib_r), (a,), sib).wait_recv()
        for cp in sends(srcs, outs, sems):
            cp.wait_send()

    dma = pltpu.SemaphoreType.DMA
    return CommJob([s[0] for s in sums], [jax.ShapeDtypeStruct((4, r // 2, cc), BF16) for (_, _, r, cc) in sums],
                   [dma((n, 3)), dma((n, 3)), dma((n,)), dma((n,))], start, finish)


def allreduce_small(vec):
    R, C = vec.shape

    def body(src, out, gathered, send_sems, recv_sems):
        x, y, c, _ = _place()
        me = 4 * x + 2 * y + c
        gathered[me] = src[...]
        peers = [(fx, fy, fc) for fx in (0, 1) for fy in (0, 1) for fc in (0, 1) if (fx, fy, fc) != (0, 0, 0)]
        sends = []
        for j, (fx, fy, fc) in enumerate(peers):
            to = (x ^ fx, y ^ fy, c ^ fc)
            cp = pltpu.make_async_remote_copy(
                src_ref=src, dst_ref=gathered.at[me], send_sem=send_sems.at[j], recv_sem=recv_sems.at[j],
                device_id=to, device_id_type=MESH_IDS)
            cp.start()
            sends.append(cp)
        for j, (fx, fy, fc) in enumerate(peers):
            frm = 4 * (x ^ fx) + 2 * (y ^ fy) + (c ^ fc)
            pltpu.make_async_remote_copy(
                src_ref=src, dst_ref=gathered.at[frm], send_sem=send_sems.at[j], recv_sem=recv_sems.at[j],
                device_id=(x, y, c), device_id_type=MESH_IDS).wait_recv()
        for cp in sends:
            cp.wait_send()
        acc = gathered[0]
        for k in range(1, 8):
            acc = acc + gathered[k]
        out[...] = acc

    vm = pl.BlockSpec(memory_space=pltpu.VMEM)
    return pl.pallas_call(
        body, name="allreduce_small", in_specs=[vm], out_specs=vm,
        out_shape=jax.ShapeDtypeStruct((R, C), F32),
        scratch_shapes=[pltpu.VMEM((8, R, C), F32), pltpu.SemaphoreType.DMA((7,)), pltpu.SemaphoreType.DMA((7,))],
        compiler_params=_cparams(),
    )(vec)


def pair_sum(name, mine, theirs):
    rows, cols = mine.shape
    tr = _pick(rows, max(16, (1 << 20) // cols), 16)
    return rowwise(name, lambda a, b: a.astype(F32) + b.astype(F32), [_cols(mine), _cols(theirs)], [], [(cols, BF16)], [], tr)[0]


def chip_sum(name, rb):
    _, rh, C = rb.shape
    tr = _pick(rh, max(16, (1 << 19) // C), 16)

    def body(r_ref, o_ref):
        acc = r_ref[0].astype(F32)
        for k in range(1, 4):
            acc = acc + r_ref[k].astype(F32)
        o_ref[...] = acc

    return pl.pallas_call(body, name=name, grid=(rh // tr,),
                          in_specs=[pl.BlockSpec((4, tr, C), lambda i: (0, i, 0))],
                          out_specs=pl.BlockSpec((tr, C), lambda i: (i, 0)),
                          out_shape=jax.ShapeDtypeStruct((rh, C), F32),
                          compiler_params=_cparams(("parallel",)))(rb)


def adamw_halves(name, w, mine, theirs, m, v):
    rows, cols = w.shape
    tr = _pick(rows // 2, max(8, (1 << 19) // cols), 8)
    nbh = rows // 2 // tr
    full = pl.BlockSpec((tr, cols), lambda i: (i, 0))
    half = pl.BlockSpec((tr, cols), lambda i: (i % nbh, 0))

    def body(w_ref, a_ref, b_ref, m_ref, v_ref, g_out, d_out, m_out, v_out):
        is_mine = (pl.program_id(0) // nbh) == lax.axis_index("c")
        g = jnp.where(is_mine, a_ref[...], b_ref[...])
        d, mn, vn = _adamw_math(w_ref[...], g, m_ref[...], v_ref[...])
        g_out[...] = g
        d_out[...] = d
        m_out[...] = mn
        v_out[...] = vn

    return pl.pallas_call(body, name="adamw_" + name, grid=(rows // tr,), in_specs=[full, half, half, full, full],
                          out_specs=[full] * 4, out_shape=[jax.ShapeDtypeStruct((rows, cols), F32)] * 4,
                          compiler_params=_cparams(("parallel",)))(w, mine, theirs, m, v)


BIG = {"ffn1_wg": "col", "ffn1_wu": "col", "ffn1_wd": "row", "w_in": "slab", "w_branch": "row", "w_out": "row",
       "ffn2_wg": "col", "ffn2_wu": "col", "ffn2_wd": "row"}
LORA = ["gla_w_a2", "rwkv_w_w2", "rwkv_w_a2", "rwkv_w_g2"]
REPLICATED = ["ffn1_norm", "mix_norm", "gla_b_a", "gla_gn_w", "rwkv_mu", "rwkv_w0", "rwkv_a0", "rwkv_k_k", "rwkv_k_a",
              "rwkv_r_k", "rwkv_lnx_w", "rwkv_lnx_b", "gate_b", "ffn2_norm", "final_norm"]
WEIGHTS = ["ffn1_norm", "ffn1_wg", "ffn1_wu", "ffn1_wd", "mix_norm", "w_in", "gla_w_a2", "gla_b_a", "gla_gn_w", "rwkv_mu",
           "rwkv_w0", "rwkv_w_w2", "rwkv_a0", "rwkv_w_a2", "rwkv_w_g2", "rwkv_k_k", "rwkv_k_a", "rwkv_r_k", "rwkv_lnx_w",
           "rwkv_lnx_b", "gate_b", "w_branch", "w_out", "ffn2_norm", "ffn2_wg", "ffn2_wu", "ffn2_wd", "final_norm"]


def kernel(x, ffn1_norm, ffn1_wg, ffn1_wu, ffn1_wd, mix_norm, w_in, gla_w_a2, gla_b_a, gla_gn_w, rwkv_mu, rwkv_w0, rwkv_w_w2, rwkv_a0, rwkv_w_a2, rwkv_w_g2, rwkv_k_k, rwkv_k_a, rwkv_r_k, rwkv_lnx_w, rwkv_lnx_b, gate_b, w_branch, w_out, ffn2_norm, ffn2_wg, ffn2_wu, ffn2_wd, final_norm, loss_target, m_ffn1_norm, m_ffn1_wg, m_ffn1_wu, m_ffn1_wd, m_mix_norm, m_w_in, m_gla_w_a2, m_gla_b_a, m_gla_gn_w, m_rwkv_mu, m_rwkv_w0, m_rwkv_w_w2, m_rwkv_a0, m_rwkv_w_a2, m_rwkv_w_g2, m_rwkv_k_k, m_rwkv_k_a, m_rwkv_r_k, m_rwkv_lnx_w, m_rwkv_lnx_b, m_gate_b, m_w_branch, m_w_out, m_ffn2_norm, m_ffn2_wg, m_ffn2_wu, m_ffn2_wd, m_final_norm, v_ffn1_norm, v_ffn1_wg, v_ffn1_wu, v_ffn1_wd, v_mix_norm, v_w_in, v_gla_w_a2, v_gla_b_a, v_gla_gn_w, v_rwkv_mu, v_rwkv_w0, v_rwkv_w_w2, v_rwkv_a0, v_rwkv_w_a2, v_rwkv_w_g2, v_rwkv_k_k, v_rwkv_k_a, v_rwkv_r_k, v_rwkv_lnx_w, v_rwkv_lnx_b, v_gate_b, v_w_branch, v_w_out, v_ffn2_norm, v_ffn2_wg, v_ffn2_wu, v_ffn2_wd, v_final_norm):
    args = dict(locals())
    wts = {n: args[n] for n in WEIGHTS}
    moms = {n: args["m_" + n] for n in WEIGHTS}
    vars_ = {n: args["v_" + n] for n in WEIGHTS}

    xs = x[0]
    tgt = loss_target[0]
    S, D = xs.shape
    FF = ffn1_wd.shape[1] * 4
    gheads = gla_b_a.shape[-1] // GLA_DK
    GQ, GV = gheads * GLA_DK, gheads * GLA_DV
    rheads = rwkv_r_k.shape[1]
    RW = rheads * RW_HD
    NT = RW // 128
    lo_g = gla_w_a2.shape[1]
    lo_w = rwkv_w_w2.shape[1]
    lo_a = rwkv_w_a2.shape[1]
    assert rwkv_w_g2.shape[1] == GATE_LORA and RW % 128 == 0

    ow = rw_window(RW)
    WIN = -(-ow["used"] // (2 * D)) * (2 * D)
    lay = dict(gate=WIN, v=WIN + 2 * D, r=WIN + 2 * D + GV, q=WIN + 2 * D + 2 * GV, k=WIN + 2 * D + 2 * GV + GQ, a=ow["a"])
    DP = lay["k"] + GQ
    DIN = w_in.shape[-1] * 4
    o_sizes = [GQ, GQ, GV, GV, lo_g, RW, RW, RW, lo_w, lo_a, GATE_LORA, 2 * D]
    o_offs = [sum(o_sizes[:i]) for i in range(len(o_sizes))]
    assert o_offs[-1] + o_sizes[-1] == DIN
    p_offs = [lay["q"], lay["k"], lay["v"], lay["r"], ow["a"], ow["rr"], ow["rk"], ow["rv"], ow["wd"], ow["ad"], ow["gd"], lay["gate"]]

    def to_padded(w):
        order = sorted(range(len(o_sizes)), key=lambda i: p_offs[i])
        parts, pos = [], 0
        for i in order:
            if p_offs[i] > pos:
                parts.append(jnp.zeros((w.shape[0], p_offs[i] - pos), w.dtype))
            parts.append(w[:, o_offs[i]:o_offs[i] + o_sizes[i]])
            pos = p_offs[i] + o_sizes[i]
        if pos < DP:
            parts.append(jnp.zeros((w.shape[0], DP - pos), w.dtype))
        return jnp.concatenate(parts, axis=1)

    def from_padded(w):
        return jnp.concatenate([w[:, p_offs[i]:p_offs[i] + o_sizes[i]] for i in range(len(o_sizes))], axis=1)

    def pad_rows(w, rows):
        return jnp.pad(w, ((0, rows - w.shape[0]), (0, 0)))

    mu = rwkv_mu[0]
    mu_parts = {"rr": mu[0:RW], "rk": mu[RW:2 * RW], "rv": mu[2 * RW:3 * RW], "wd": mu[3 * RW:3 * RW + lo_w],
                "ad": mu[3 * RW + lo_w:3 * RW + lo_w + lo_a], "gd": mu[3 * RW + lo_w + lo_a:]}
    mu_win = jnp.zeros((WIN,), F32)
    for key, val in mu_parts.items():
        mu_win = lax.dynamic_update_slice(mu_win, val, (ow[key],))
    mu_win = mu_win.reshape(1, WIN)

    shard_shapes = {n: wts[n].shape[1:] for n in list(BIG) + LORA}
    W = {}

    def shard(n):
        return (wts[n][0].astype(BF16), BIG[n])

    def mm_gather(a, b, out_dtype, name, gather, lora=(), **epilogue):
        out, got = matmul(a, b, "nn", out_dtype, name, **epilogue,
                          job=gather_job([shard(n) for n in gather], [(wts[n][0].astype(BF16), "col") for n in lora]))
        W.update(zip(list(gather) + list(lora), got))
        return out

    W["ffn1_wg"] = run_job("gather_ffn1_wg", gather_job([shard("ffn1_wg")]))[0]
    r_k = rwkv_r_k.reshape(1, RW)
    fin_g = final_norm.reshape(1, D)

    TR = min(128, S)
    def up_and_act(acc, g):
        return acc, f_swiglu(g.astype(F32), acc)

    h1 = rowwise("rms1", lambda a, g: f_rms(a, g), [_cols(xs)], [ffn1_norm], [(D, BF16)], [], TR)[0]
    g1 = mm_gather(h1, W["ffn1_wg"], BF16, "ffn1_g", ["ffn1_wu"])
    u1, act1 = mm_gather(h1, W["ffn1_wu"], [BF16, BF16], "ffn1_u", ["ffn1_wd"], extras=[g1], epilogue=up_and_act)
    f1 = mm_gather(act1, W["ffn1_wd"], F32, "ffn1_d", ["w_in"], LORA)
    w_in_p = to_padded(W["w_in"].transpose(1, 0, 2).reshape(D, DIN))
    gla_a2_p = pad_rows(W["gla_w_a2"], LORA_PAD)
    w_w2_p = pad_rows(W["rwkv_w_w2"], LORA_PAD)
    w_a2_p = pad_rows(W["rwkv_w_a2"], LORA_PAD)
    w_g2 = W["rwkv_w_g2"]

    def res_rms(coef):
        def fn(a, f, g):
            x1 = a + coef * f
            return x1, f_rms(x1, g)
        return fn

    x1, h2 = rowwise("res_rms_mix", res_rms(0.5), [_cols(xs), _cols(f1)], [mix_norm], [(D, F32), (D, BF16)], [], TR)
    p = mm_gather(h2, w_in_p, F32, "w_in", ["w_branch", "w_out", "ffn2_wg"])
    wb_g, wb_r = W["w_branch"][:GV], W["w_branch"][GV:]

    o_gla, gla_states = gla_forward(p, lay, gla_a2_p, gla_b_a, gla_gn_w, gheads)
    xsh = token_shift_forward(p, mu_win, WIN, TR)
    pre_consts = [rwkv_w0, w_w2_p, rwkv_a0, w_a2_p, w_g2, rwkv_k_k, rwkv_k_a]
    pre_fn = functools.partial(f_rw_pre, rw=RW)

    def pre_f32(xw, w0, ww, a0, wa, wg_, kk_, ka_):
        return pre_fn(xw, w0, ww.astype(F32), a0, wa.astype(F32), wg_.astype(F32), kk_, ka_)

    r_, dec_, k2_, v_, kk_, b_, g_ = rowwise("rw_pre", pre_f32, [_cols(xsh)], pre_consts, [(RW, F32)] * 7, [], 128)

    def tiles(a):
        return a.reshape(S, NT, 128)

    y_t, rw_states = rwkv_scan_forward(*(tiles(a) for a in (r_, dec_, k2_, v_, kk_, b_)))
    y_ = y_t.reshape(S, RW)
    post_consts = [rwkv_lnx_w, rwkv_lnx_b, r_k]
    o_rw = rowwise("rw_post", f_rw_post, [_cols(a) for a in (y_, r_, k2_, v_, g_)], post_consts, [(RW, BF16)], [], TR)[0]

    yg = matmul(o_gla, wb_g, "nn", F32, "branch_gla")
    yr = matmul(o_rw, wb_r, "nn", F32, "branch_rw")
    merge_fn = functools.partial(f_merge, d=D)
    merged = rowwise("merge", merge_fn, [_cols(p, 2 * D, lay["gate"]), _cols(yg), _cols(yr)], [gate_b], [(D, BF16)], [], TR)[0]
    mix = matmul(merged, W["w_out"], "nn", F32, "w_out")
    x2, h3 = rowwise("res_rms_ffn2", res_rms(1.0), [_cols(x1), _cols(mix)], [ffn2_norm], [(D, F32), (D, BF16)], [], TR)
    g3 = mm_gather(h3, W["ffn2_wg"], BF16, "ffn2_g", ["ffn2_wu"])
    u3, act3 = mm_gather(h3, W["ffn2_wu"], [BF16, BF16], "ffn2_u", ["ffn2_wd"], extras=[g3], epilogue=up_and_act)
    f3 = matmul(act3, W["ffn2_wd"], "nn", F32, "ffn2_d")

    def final_fn(a, f, t, g):
        def loss_of(a, f, g):
            yv = f_rms(a + 0.5 * f, g)
            return 0.5 * jnp.sum(jnp.mean(jnp.square(yv - t), axis=-1))
        val, vjp = jax.vjp(loss_of, a, f, g)
        da, df, dg = vjp(jnp.ones((), F32))
        return da, df, jnp.full((1, 128), val, F32), dg

    dx2, df3, loss_acc, d_final = rowwise("final_loss", final_fn, [_cols(x2), _cols(f3), _cols(tgt)], [fin_g],
                                          [(D, F32), (D, BF16)], [(1, 128), (1, D)], TR)
    grads = {"final_norm": d_final.reshape(D)}

    received = {}

    def pair_up(n, dw):
        r, cc = shard_shapes[n]
        flat = (4 * r, cc) if BIG[n] == "slab" else dw.shape
        theirs = sibling_swap("swap_" + n, [dw])[0]
        return n, (pair_sum("pair_" + n, dw.reshape(flat), theirs.reshape(flat)).reshape(dw.shape), BIG[n], r, cc)

    def mm_exchange(a, b, mode, out_dtype, name, pending):
        out, got = matmul(a, b, mode, out_dtype, name, job=exchange_job([entry for _, entry in pending]))
        received.update(zip([n for n, _ in pending], got))
        return out

    def ffn_backward(tag, h, gx, ux, act, df, wg, wu, wd, last_job=None):
        def through_act(dact, g, u):
            _, vjp = jax.vjp(f_swiglu, *f32(g, u))
            return vjp(dact)

        dgx, dux = matmul(df, wd, "nt", [BF16, BF16], tag + "_dact", extras=[gx, ux], epilogue=through_act)
        p_wd = pair_up(tag + "_wd", matmul(act, df, "tn", BF16, tag + "_dwd"))
        p_wg = pair_up(tag + "_wg", mm_exchange(h, dgx, "tn", BF16, tag + "_dwg", [p_wd]))
        p_wu = pair_up(tag + "_wu", mm_exchange(h, dux, "tn", BF16, tag + "_dwu", [p_wg]))
        dha = mm_exchange(dgx, wg, "nt", F32, tag + "_dh_g", [p_wu])
        if last_job is None:
            return dha, matmul(dux, wu, "nt", F32, tag + "_dh_u"), None
        dhb, got = matmul(dux, wu, "nt", F32, tag + "_dh_u", job=last_job)
        return dha, dhb, got

    def res_rms_bwd(name, coef, a, f, g, dx1, dha, dhb):
        def fn(a, f, dx1, dha, dhb, g):
            _, vjp = jax.vjp(res_rms(coef), a, f, g)
            return vjp((dx1, dha + dhb))

        return rowwise(name, fn, [_cols(a), _cols(f), _cols(dx1), _cols(dha), _cols(dhb)], [g],
                       [(D, F32), (D, BF16)], [(1, D)], TR)

    dh3a, dh3b, _ = ffn_backward("ffn2", h3, g3, u3, act3, df3, W["ffn2_wg"], W["ffn2_wu"], W["ffn2_wd"])
    dx1, dmix, grads["ffn2_norm"] = res_rms_bwd("res_rms_ffn2_bwd", 1.0, x1, mix, ffn2_norm, dx2, dh3a, dh3b)

    p_wo = pair_up("w_out", matmul(merged, dmix, "tn", BF16, "d_w_out"))
    dmerged = mm_exchange(dmix, W["w_out"], "nt", F32, "d_merged", [p_wo])

    def merge_bwd(gp, a, b, d, gb):
        _, vjp = jax.vjp(merge_fn, gp, a, b, gb)
        return vjp(d)

    dgate, dyg, dyr, grads["gate_b"] = rowwise(
        "merge_bwd", merge_bwd, [_cols(p, 2 * D, lay["gate"]), _cols(yg), _cols(yr), _cols(dmerged)], [gate_b],
        [(2 * D, BF16), (D, BF16), (D, BF16)], [(1, 2 * D)], TR)
    do_gla = matmul(dyg, wb_g, "nt", BF16, "d_o_gla")
    do_rw = matmul(dyr, wb_r, "nt", F32, "d_o_rw")
    p_wb = pair_up("w_branch", jnp.concatenate([matmul(o_gla, dyg, "tn", BF16, "d_wb_gla"),
                                                matmul(o_rw, dyr, "tn", BF16, "d_wb_rw")], axis=0))

    def post_bwd(yv, rv, kv, vv, gv_, d, lw, lb, rk):
        _, vjp = jax.vjp(f_rw_post, yv, rv, kv, vv, gv_, lw, lb, rk)
        return vjp(d)

    dy_, dr_p, dk2_p, dv_p, dg_p, grads["rwkv_lnx_w"], grads["rwkv_lnx_b"], d_rk = rowwise(
        "rw_post_bwd", post_bwd, [_cols(a) for a in (y_, r_, k2_, v_, g_, do_rw)], post_consts,
        [(RW, F32)] * 5, [(1, RW)] * 3, 128)
    grads["rwkv_r_k"] = d_rk.reshape(rwkv_r_k.shape[1:])

    scan_cots = rwkv_scan_backward(*(tiles(a) for a in (r_, dec_, k2_, v_, kk_, b_)), rw_states, tiles(dy_))
    dr_s, dw_s, dk2_s, dv_s, dkk_s, db_s = (a.reshape(S, RW) for a in scan_cots)

    def pre_bwd(xw, c0, c1, c2, c3, c4, c5, c6, c7, c8, c9, w0, ww, a0, wa, wg_, kk_c, ka_c):
        _, vjp = jax.vjp(pre_fn, xw, w0, ww.astype(F32), a0, wa.astype(F32), wg_.astype(F32), kk_c, ka_c)
        return vjp((c0 + c6, c1, c2 + c7, c3 + c8, c4, c5, c9))

    dxsh, grads["rwkv_w0"], d_ww2, grads["rwkv_a0"], d_wa2, d_wg2, grads["rwkv_k_k"], grads["rwkv_k_a"] = rowwise(
        "rw_pre_bwd", pre_bwd,
        [_cols(xsh)] + [_cols(a) for a in (dr_s, dw_s, dk2_s, dv_s, dkk_s, db_s, dr_p, dk2_p, dv_p, dg_p)], pre_consts,
        [(WIN, F32)], [(1, RW), w_w2_p.shape, (1, RW), w_a2_p.shape, w_g2.shape, (1, RW), (1, RW)], 128)
    grads["rwkv_w_w2"], grads["rwkv_w_a2"], grads["rwkv_w_g2"] = d_ww2[:lo_w], d_wa2[:lo_a], d_wg2

    dq, dk, dv, dr, da, d_ga2, grads["gla_b_a"], grads["gla_gn_w"] = gla_backward(
        p, lay, gla_states, do_gla, gla_a2_p, gla_b_a, gla_gn_w, gheads)
    grads["gla_w_a2"] = d_ga2[:lo_g]

    dpw, dmu_win = token_shift_backward(p, dxsh, da, mu_win, WIN, ow["a"], TR)
    dmu = dmu_win[0]
    grads["rwkv_mu"] = jnp.concatenate([dmu[ow[k_]:ow[k_] + mu_parts[k_].shape[0]] for k_ in ("rr", "rk", "rv", "wd", "ad", "gd")]).reshape(1, -1)

    dp = jnp.concatenate([dpw, dgate, dv, dr, dq, dk], axis=1)
    d_w_in = from_padded(mm_exchange(h2, dp, "tn", BF16, "d_w_in", [p_wb]))
    p_wi = pair_up("w_in", d_w_in.reshape(D, 4, shard_shapes["w_in"][1]).transpose(1, 0, 2))
    dh2 = mm_exchange(dp, w_in_p, "nt", F32, "d_h2", [p_wi])
    zeros_d = jnp.zeros_like(dh2)
    dx0, df1, grads["mix_norm"] = res_rms_bwd("res_rms_mix_bwd", 0.5, xs, f1, mix_norm, dx1, dh2, zeros_d)

    early = [n for n in REPLICATED if n != "ffn1_norm"] + LORA
    early_flat = jnp.concatenate([grads[n].reshape(-1) for n in early])
    early_rows = -(-early_flat.shape[0] // 1024) * 8
    early_vec = jnp.pad(early_flat, (0, early_rows * 128 - early_flat.shape[0])).reshape(early_rows, 128)
    dh1a, dh1b, (early_all,) = ffn_backward(
        "ffn1", h1, g1, u1, act1, df1, W["ffn1_wg"], W["ffn1_wu"], W["ffn1_wd"],
        last_job=device_gather_job(early_vec, jnp.zeros((8, early_rows, 128), F32)))

    def rms1_bwd(a, dha, dhb, dxa, g):
        _, vjp = jax.vjp(f_rms, a, g)
        da_, dg_ = vjp(dha + dhb)
        return da_ + dxa, dg_

    grad_x, grads["ffn1_norm"] = rowwise("rms1_bwd", rms1_bwd, [_cols(xs), _cols(dh1a), _cols(dh1b), _cols(dx0)],
                                         [ffn1_norm], [(D, F32)], [(1, D)], TR)

    names = list(BIG)
    halves = [chip_sum("chip_sum_" + n, received[n]) for n in names]
    others = sibling_swap("sibling_join", halves)
    final_grads, delta, new_m, new_v = {}, {}, {}, {}
    for n, h, o in zip(names, halves, others):
        res = adamw_halves(n, wts[n][0], h, o, moms[n][0], vars_[n][0])
        final_grads[n], delta[n], new_m[n], new_v[n] = (a.reshape(wts[n].shape) for a in res)

    rep_sum = sum_devices(early_all, early_vec).reshape(-1)
    late = grads["ffn1_norm"].reshape(-1, 128)
    final_grads["ffn1_norm"] = allreduce_small(jnp.pad(late, ((0, -late.shape[0] % 8), (0, 0))))[:late.shape[0]].reshape(wts["ffn1_norm"].shape)
    my_chip = 2 * lax.axis_index("x") + lax.axis_index("y")
    off = 0
    for n in early:
        size = grads[n].size
        full = rep_sum[off:off + size].reshape(grads[n].shape)
        off += size
        if n in LORA:
            cc = shard_shapes[n][1]
            full = lax.dynamic_slice_in_dim(full, my_chip * cc, cc, axis=1)
        final_grads[n] = full.reshape(wts[n].shape)

    loss = lax.psum(loss_acc[0, 0], ("x", "y", "c"))

    small = REPLICATED + LORA

    def two(a):
        return a.reshape(-1, a.shape[-1])

    ds, ms_, vs_ = adamw_small([two(wts[n]) for n in small], [two(final_grads[n]) for n in small],
                               [two(moms[n]) for n in small], [two(vars_[n]) for n in small])
    for i, n in enumerate(small):
        shp = wts[n].shape
        delta[n], new_m[n], new_v[n] = ds[i].reshape(shp), ms_[i].reshape(shp), vs_[i].reshape(shp)

    return (loss, grad_x.reshape(x.shape), *[final_grads[n] for n in WEIGHTS], *[delta[n] for n in WEIGHTS],
            *[new_m[n] for n in WEIGHTS], *[new_v[n] for n in WEIGHTS])
```

```python
import functools
import math

import jax
import jax.numpy as jnp
from jax import lax
from jax.experimental import pallas as pl
from jax.experimental.pallas import tpu as pltpu

F32 = jnp.float32
BF16 = jnp.bfloat16
MESH_IDS = pl.DeviceIdType.MESH

NORM_EPS = 1e-6
GN_EPS = 64e-5
GLA_TAU = 16.0
CHUNK = 64
GLA_DK = 128
GLA_DV = 256
RW_HD = 64
LORA_PAD = 128
GATE_LORA = 256
ADAM_LR, ADAM_B1, ADAM_B2, ADAM_EPS, ADAM_WD, ADAM_STEP = 0.001, 0.9, 0.999, 1e-08, 0.01, 10

VMEM_LIMIT_BYTES = 56 * 1024 * 1024
HBM_SPEC = pl.BlockSpec(memory_space=pltpu.HBM)


def _cparams(sem=None):
    return pltpu.CompilerParams(dimension_semantics=sem, vmem_limit_bytes=VMEM_LIMIT_BYTES)


def _pick(n, target, mult=128):
    best = None
    for t in range(mult, min(n, target) + 1, mult):
        if n % t == 0:
            best = t
    return best if best is not None else n


_NN = (((1,), (0,)), ((), ()))
_NT = (((1,), (1,)), ((), ()))
_TN = (((0,), (0,)), ((), ()))


def _dg(a, b, dims):
    return lax.dot_general(a.astype(BF16), b.astype(BF16), dims, preferred_element_type=F32)


@jax.custom_vjp
def mm_nn(a, b):
    return _dg(a, b, _NN)


def _mm_nn_fwd(a, b):
    return _dg(a, b, _NN), (a, b)


def _mm_nn_bwd(res, g):
    a, b = res
    return _dg(g, b, _NT).astype(a.dtype), _dg(a, g, _TN).astype(b.dtype)


mm_nn.defvjp(_mm_nn_fwd, _mm_nn_bwd)


@jax.custom_vjp
def mm_nt(a, b):
    return _dg(a, b, _NT)


def _mm_nt_fwd(a, b):
    return _dg(a, b, _NT), (a, b)


def _mm_nt_bwd(res, g):
    a, b = res
    return _dg(g, b, _NN).astype(a.dtype), _dg(g, a, _TN).astype(b.dtype)


mm_nt.defvjp(_mm_nt_fwd, _mm_nt_bwd)


@jax.custom_vjp
def mm_tn(a, b):
    return _dg(a, b, _TN)


def _mm_tn_fwd(a, b):
    return _dg(a, b, _TN), (a, b)


def _mm_tn_bwd(res, g):
    a, b = res
    return _dg(b, g, _NT).astype(a.dtype), _dg(a, g, _NN).astype(b.dtype)


mm_tn.defvjp(_mm_tn_fwd, _mm_tn_bwd)


def _split3(x):
    h = x.astype(BF16)
    r = x - h.astype(F32)
    m = r.astype(BF16)
    l = (r - m.astype(F32)).astype(BF16)
    return h, m, l


def _block_ones(n, seg):
    i = lax.broadcasted_iota(jnp.int32, (n, n), 0) // seg
    j = lax.broadcasted_iota(jnp.int32, (n, n), 1) // seg
    return (i == j).astype(BF16)


def _segsum_raw(x, seg, terms):
    ones = _block_ones(128, seg)
    outs = []
    for j in range(x.shape[1] // 128):
        t = x[:, j * 128:(j + 1) * 128]
        parts = _split3(t)[:terms]
        acc = jnp.dot(parts[0], ones, preferred_element_type=F32)
        for p_ in parts[1:]:
            acc = acc + jnp.dot(p_, ones, preferred_element_type=F32)
        outs.append(acc)
    return outs[0] if len(outs) == 1 else jnp.concatenate(outs, axis=1)


@jax.custom_vjp
def segsum64(x):
    return _segsum_raw(x, RW_HD, 3)


segsum64.defvjp(lambda x: (_segsum_raw(x, RW_HD, 3), None), lambda _, g: (_segsum_raw(g, RW_HD, 3),))


def _tri(n, upper):
    i = lax.broadcasted_iota(jnp.int32, (n, n), 0)
    j = lax.broadcasted_iota(jnp.int32, (n, n), 1)
    return ((i <= j) if upper else (i >= j)).astype(BF16)


def _tri_mm(x, upper):
    t = _tri(x.shape[0], upper)
    h, m, l = _split3(x)
    return (jnp.dot(t, h, preferred_element_type=F32) + jnp.dot(t, m, preferred_element_type=F32)
            + jnp.dot(t, l, preferred_element_type=F32))


@jax.custom_vjp
def cumsum_rows(x):
    return _tri_mm(x, False)


cumsum_rows.defvjp(lambda x: (_tri_mm(x, False), None), lambda _, g: (_tri_mm(g, True),))


def _make_split(sizes, axis):
    offs = [sum(sizes[:i]) for i in range(len(sizes))]

    def cut(x):
        if axis == 1:
            return tuple(x[:, o:o + s] for o, s in zip(offs, sizes))
        return tuple(x[o:o + s, :] for o, s in zip(offs, sizes))

    @jax.custom_vjp
    def split(x):
        return cut(x)

    split.defvjp(lambda x: (cut(x), None), lambda _, gs: (jnp.concatenate(gs, axis=axis),))
    return split


@jax.custom_vjp
def log_sigmoid(z):
    return jnp.minimum(z, 0.0) - jnp.log(1.0 + jnp.exp(-jnp.abs(z)))


log_sigmoid.defvjp(lambda z: (log_sigmoid(z), z), lambda z, g: (g * (1.0 - jax.nn.sigmoid(z)),))


def silu(x):
    return x * jax.nn.sigmoid(x)


def matmul(a, b, mode, out_dtype, name, tm=1024, tn=512, tk=2048, job=None, extras=(), epilogue=None):
    if mode == "nn":
        (M, K), (K2, N) = a.shape, b.shape
    elif mode == "nt":
        (M, K), (N, K2) = a.shape, b.shape
    else:
        (K, M), (K2, N) = a.shape, b.shape
    assert K == K2, (name, a.shape, b.shape)
    tm, tn, tk = _pick(M, tm), _pick(N, tn), _pick(K, tk)
    grid = (M // tm, N // tn, K // tk)
    dims = {"nn": _NN, "nt": _NT, "tn": _TN}[mode]
    a_spec = pl.BlockSpec((tk, tm), lambda i, j, k: (k, i)) if mode == "tn" else pl.BlockSpec((tm, tk), lambda i, j, k: (i, k))
    b_spec = pl.BlockSpec((tn, tk), lambda i, j, k: (j, k)) if mode == "nt" else pl.BlockSpec((tk, tn), lambda i, j, k: (k, j))
    n_in = 0 if job is None else len(job.operands)
    n_out = 0 if job is None else len(job.out_shapes)
    n_ex = len(extras)
    main_dtypes = [out_dtype] if epilogue is None else list(out_dtype)
    n_main = len(main_dtypes)

    def body(a_ref, b_ref, *rest):
        ex_refs, rest = rest[:n_ex], rest[n_ex:]
        job_ins, o_refs, job_outs = rest[:n_in], rest[n_in:n_in + n_main], rest[n_in + n_main:n_in + n_main + n_out]
        acc_ref, sems = rest[n_in + n_main + n_out], rest[n_in + n_main + n_out + 1:]
        i, j, k = pl.program_id(0), pl.program_id(1), pl.program_id(2)
        if job is not None:
            @pl.when((i == 0) & (j == 0) & (k == 0))
            def _():
                job.start(job_ins, job_outs, sems)

        part = _dg(a_ref[...], b_ref[...], dims)

        @pl.when(k == 0)
        def _():
            acc_ref[...] = part

        @pl.when(k > 0)
        def _():
            acc_ref[...] += part

        @pl.when(k == grid[2] - 1)
        def _():
            acc = acc_ref[...]
            vals = (acc,) if epilogue is None else epilogue(acc, *[e[...] for e in ex_refs])
            for o_ref, val in zip(o_refs, vals):
                o_ref[...] = val.astype(o_ref.dtype)

        if job is not None:
            @pl.when((i == grid[0] - 1) & (j == grid[1] - 1) & (k == grid[2] - 1))
            def _():
                job.finish(job_ins, job_outs, sems)

    main_spec = pl.BlockSpec((tm, tn), lambda i, j, k: (i, j))
    job_operands = [] if job is None else list(job.operands)
    aliases = {} if job is None else {2 + n_ex + op: n_main + out for op, out in job.aliases.items()}
    res = pl.pallas_call(
        body, name=name, grid=grid, in_specs=[a_spec, b_spec] + [main_spec] * n_ex + [HBM_SPEC] * n_in,
        out_specs=[main_spec] * n_main + [HBM_SPEC] * n_out,
        out_shape=[jax.ShapeDtypeStruct((M, N), dt) for dt in main_dtypes] + ([] if job is None else list(job.out_shapes)),
        scratch_shapes=[pltpu.VMEM((tm, tn), F32)] + ([] if job is None else list(job.scratch)),
        input_output_aliases=aliases,
        compiler_params=_cparams(("parallel", "parallel", "arbitrary") if job is None else ("arbitrary",) * 3),
    )(a, b, *extras, *job_operands)
    mains = res[0] if epilogue is None else tuple(res[:n_main])
    return mains if job is None else (mains, res[n_main:])


def _cols(arr, width=None, off=0):
    width = arr.shape[1] if width is None else width
    assert off % width == 0, (off, width)
    return (arr, width, off // width)


def rowwise(name, fn, rows, consts, row_outs, acc_outs, tr, extra_specs=()):
    S = rows[0][0].shape[0]
    tr = min(tr, S)
    assert S % tr == 0
    n_in = len(rows) + len(extra_specs) + len(consts)
    n_ro = len(row_outs)
    in_specs = [pl.BlockSpec((tr, w), functools.partial(lambda i, cb: (i, cb), cb=cb)) for (_, w, cb) in rows]
    in_specs += [spec for (_, spec) in extra_specs]
    in_specs += [pl.BlockSpec(c.shape, lambda i: (0, 0)) for c in consts]
    out_shape = [jax.ShapeDtypeStruct((S, w), dt) for (w, dt) in row_outs]
    out_shape += [jax.ShapeDtypeStruct(shp, F32) for shp in acc_outs]
    out_specs = [pl.BlockSpec((tr, w), lambda i: (i, 0)) for (w, _) in row_outs]
    out_specs += [pl.BlockSpec(shp, lambda i: (0, 0)) for shp in acc_outs]

    def body(*refs):
        ins = [r[...] for r in refs[:n_in]]
        outs = fn(*ins)
        outs = outs if isinstance(outs, (tuple, list)) else (outs,)
        assert len(outs) == n_ro + len(acc_outs), (name, len(outs))
        for o_ref, val in zip(refs[n_in:n_in + n_ro], outs[:n_ro]):
            o_ref[...] = val.astype(o_ref.dtype)
        i = pl.program_id(0)
        for a_ref, val in zip(refs[n_in + n_ro:], outs[n_ro:]):
            @pl.when(i == 0)
            def _(a_ref=a_ref, val=val):
                a_ref[...] = val.astype(F32)

            @pl.when(i > 0)
            def _(a_ref=a_ref, val=val):
                a_ref[...] += val.astype(F32)

    res = pl.pallas_call(
        body, name=name, grid=(S // tr,), in_specs=in_specs, out_specs=out_specs, out_shape=out_shape,
        compiler_params=_cparams(("arbitrary",) if acc_outs else ("parallel",)),
    )(*[r[0] for r in rows], *[e[0] for e in extra_specs], *consts)
    return res


def f32(*xs):
    return [x.astype(F32) for x in xs]


def f_rms(x, g):
    return x * lax.rsqrt(jnp.mean(x * x, axis=-1, keepdims=True) + NORM_EPS) * g


def f_swiglu(gx, ux):
    return silu(gx) * ux


def f_merge(gp, yg, yr, gate_b, d):
    gates = jax.nn.sigmoid(gp + gate_b)
    g1, g2 = _make_split((d, d), 1)(gates)
    return g1 * yg + g2 * yr


def rw_window(rw):
    o = dict(rr=0, rk=rw, rv=2 * rw, gd=3 * rw, wd=3 * rw + GATE_LORA)
    o["ad"] = o["wd"] + LORA_PAD
    o["a"] = o["ad"] + LORA_PAD
    o["used"] = o["a"] + LORA_PAD
    return o


def f_rw_pre(xs, w0, w_w2, a0, w_a2, w_g2, k_k, k_a, rw):
    win = xs.shape[1]
    o = rw_window(rw)
    sizes = (rw, rw, rw, GATE_LORA, LORA_PAD, LORA_PAD, win - o["a"])
    rr, rk, rv, gd, wd, ad, _ = _make_split(sizes, 1)(xs)
    w_raw = w0 + mm_nn(jnp.tanh(wd), w_w2)
    dec = jnp.exp(-jnp.exp(log_sigmoid(w_raw) - 0.5))
    a = jax.nn.sigmoid(a0 + mm_nn(ad, w_a2))
    g = mm_nn(jax.nn.sigmoid(gd), w_g2)
    kx = rk * k_k
    kk = kx / jnp.maximum(jnp.sqrt(segsum64(kx * kx)), 1e-12)
    k2 = rk * (1.0 + (a - 1.0) * k_a)
    return rr, dec, k2, rv, kk, kk * a, g


def f_rw_post(y, r, k2, v, g, lnx_w, lnx_b, r_k):
    mu = segsum64(y) * (1.0 / RW_HD)
    yc = y - mu
    var = segsum64(yc * yc) * (1.0 / RW_HD)
    yn = yc * lax.rsqrt(var + GN_EPS) * lnx_w + lnx_b
    bonus = segsum64(r * k2 * r_k) * v
    return (yn + bonus) * g


def f_gla_chunk(q, k, v, r, a, st_prev, w_a2, b_a, gn_w, heads):
    z = mm_nn(a, w_a2) + b_a
    la = log_sigmoid(z) * (1.0 / GLA_TAU)
    cum = cumsum_rows(la)
    total = jnp.sum(la, axis=0, keepdims=True)
    kdec = k * jnp.exp(total - cum)
    et = jnp.exp(total)
    qs = q * (GLA_DK ** -0.5)
    sk = _make_split((GLA_DK,) * heads, 1)
    sv = _make_split((GLA_DV,) * heads, 1)
    ss = _make_split((GLA_DV,) * heads, 0)
    kd_h, q_h, et_h, v_h, st_h = sk(kdec), sk(qs), sk(et), sv(v), ss(st_prev)
    outs, news = [], []
    for h in range(heads):
        st_new = st_h[h] * et_h[h] + mm_tn(v_h[h], kd_h[h])
        o = mm_nt(q_h[h], st_new)
        o = o * lax.rsqrt(jnp.mean(o * o, axis=-1, keepdims=True) + NORM_EPS) * gn_w
        outs.append(o)
        news.append(st_new)
    o_all = outs[0] if heads == 1 else jnp.concatenate(outs, axis=1)
    st_all = news[0] if heads == 1 else jnp.concatenate(news, axis=0)
    return o_all * silu(r), st_all


def gla_forward(p, lay, w_a2, b_a, gn_w, heads):
    S = p.shape[0]
    nc = S // CHUNK
    gq, gv = heads * GLA_DK, heads * GLA_DV

    def spec(width, off, rev=False):
        assert off % width == 0
        return pl.BlockSpec((CHUNK, width), functools.partial(lambda n, cb: (n, cb), cb=off // width))

    def body(q_ref, k_ref, v_ref, r_ref, a_ref, w_ref, b_ref, g_ref, o_ref, st_out_ref, st_sc):
        @pl.when(pl.program_id(0) == 0)
        def _():
            st_sc[...] = jnp.zeros_like(st_sc)

        st_prev = st_sc[...]
        st_out_ref[0] = st_prev
        o, st_new = f_gla_chunk(*f32(q_ref[...], k_ref[...], v_ref[...], r_ref[...], a_ref[...]), st_prev,
                                w_ref[...], b_ref[...], g_ref[...], heads)
        o_ref[...] = o.astype(o_ref.dtype)
        st_sc[...] = st_new

    return pl.pallas_call(
        body, name="gla_fwd", grid=(nc,),
        in_specs=[spec(gq, lay["q"]), spec(gq, lay["k"]), spec(gv, lay["v"]), spec(gv, lay["r"]), spec(LORA_PAD, lay["a"]),
                  pl.BlockSpec(w_a2.shape, lambda n: (0, 0)), pl.BlockSpec(b_a.shape, lambda n: (0, 0)),
                  pl.BlockSpec(gn_w.shape, lambda n: (0, 0))],
        out_specs=[pl.BlockSpec((CHUNK, gv), lambda n: (n, 0)), pl.BlockSpec((1, gv, GLA_DK), lambda n: (n, 0, 0))],
        out_shape=[jax.ShapeDtypeStruct((S, gv), BF16), jax.ShapeDtypeStruct((nc, gv, GLA_DK), F32)],
        scratch_shapes=[pltpu.VMEM((gv, GLA_DK), F32)],
        compiler_params=_cparams(("arbitrary",)),
    )(p, p, p, p, p, w_a2, b_a, gn_w)


def gla_backward(p, lay, states, d_out, w_a2, b_a, gn_w, heads):
    S = p.shape[0]
    nc = S // CHUNK
    gq, gv = heads * GLA_DK, heads * GLA_DV

    def spec(width, off):
        assert off % width == 0
        return pl.BlockSpec((CHUNK, width), functools.partial(lambda n, cb: (nc - 1 - n, cb), cb=off // width))

    def rev(width):
        return pl.BlockSpec((CHUNK, width), lambda n: (nc - 1 - n, 0))

    def whole(arr):
        return pl.BlockSpec(arr.shape, lambda n: (0, 0))

    def body(q_ref, k_ref, v_ref, r_ref, a_ref, st_ref, do_ref, w_ref, b_ref, g_ref,
             dq_ref, dk_ref, dv_ref, dr_ref, da_ref, dw_ref, db_ref, dg_ref, dst_sc):
        n = pl.program_id(0)

        @pl.when(n == 0)
        def _():
            dst_sc[...] = jnp.zeros_like(dst_sc)

        fn = functools.partial(f_gla_chunk, heads=heads)
        prim = (*f32(q_ref[...], k_ref[...], v_ref[...], r_ref[...], a_ref[...]), st_ref[0],
                w_ref[...].astype(F32), b_ref[...], g_ref[...])
        _, vjp = jax.vjp(fn, *prim)
        dq, dk, dv, dr, da, dst, dw, db, dg = vjp((do_ref[...].astype(F32), dst_sc[...]))
        for ref, val in ((dq_ref, dq), (dk_ref, dk), (dv_ref, dv), (dr_ref, dr), (da_ref, da)):
            ref[...] = val.astype(ref.dtype)
        dst_sc[...] = dst

        @pl.when(n == 0)
        def _():
            dw_ref[...] = dw
            db_ref[...] = db
            dg_ref[...] = dg

        @pl.when(n > 0)
        def _():
            dw_ref[...] += dw
            db_ref[...] += db
            dg_ref[...] += dg

    return pl.pallas_call(
        body, name="gla_bwd", grid=(nc,),
        in_specs=[spec(gq, lay["q"]), spec(gq, lay["k"]), spec(gv, lay["v"]), spec(gv, lay["r"]), spec(LORA_PAD, lay["a"]),
                  pl.BlockSpec((1, gv, GLA_DK), lambda n: (nc - 1 - n, 0, 0)), rev(gv),
                  whole(w_a2), whole(b_a), whole(gn_w)],
        out_specs=[rev(gq), rev(gq), rev(gv), rev(gv), rev(LORA_PAD), whole(w_a2), whole(b_a), whole(gn_w)],
        out_shape=[jax.ShapeDtypeStruct((S, gq), BF16), jax.ShapeDtypeStruct((S, gq), BF16),
                   jax.ShapeDtypeStruct((S, gv), BF16), jax.ShapeDtypeStruct((S, gv), BF16),
                   jax.ShapeDtypeStruct((S, LORA_PAD), F32),
                   jax.ShapeDtypeStruct(w_a2.shape, F32), jax.ShapeDtypeStruct(b_a.shape, F32),
                   jax.ShapeDtypeStruct(gn_w.shape, F32)],
        scratch_shapes=[pltpu.VMEM((gv, GLA_DK), F32)],
        compiler_params=_cparams(("arbitrary",)),
    )(p, p, p, p, p, states, d_out, w_a2, b_a, gn_w)


SCAN_BLOCK = 32


def _scan_helpers(nt):
    ones = _block_ones(128, RW_HD)
    rows = lax.broadcasted_iota(jnp.int32, (nt * RW_HD, 128), 0) % RW_HD
    lanes = lax.broadcasted_iota(jnp.int32, (nt * RW_HD, 128), 1) % RW_HD
    eye = rows == lanes

    def bc(ref, t):
        parts = [jnp.broadcast_to(ref[t, j:j + 1, :], (RW_HD, 128)) for j in range(nt)]
        return parts[0] if nt == 1 else jnp.concatenate(parts, axis=0)

    def seg1(x):
        return jnp.dot(x.astype(BF16), ones, preferred_element_type=F32)

    def column(ref, t):
        return seg1(jnp.where(eye, bc(ref, t), 0.0))

    def put_diag(ref, t, x):
        put_colsum(ref, t, jnp.where(eye, x, 0.0))

    def put_colsum(ref, t, x, sign=1.0):
        for j in range(nt):
            ref[t, j:j + 1, :] = sign * jnp.sum(x[j * RW_HD:(j + 1) * RW_HD, :], axis=0, keepdims=True)

    return bc, seg1, column, put_diag, put_colsum


def rwkv_scan_forward(r, w, k2, v, kk, b):
    S, nt, _ = r.shape
    tb = min(SCAN_BLOCK, S)

    def body(r_ref, w_ref, k2_ref, v_ref, kk_ref, b_ref, y_ref, st_ref, s_sc):
        @pl.when(pl.program_id(0) == 0)
        def _():
            s_sc[...] = jnp.zeros_like(s_sc)

        bc, seg1, column, put_diag, put_colsum = _scan_helpers(nt)

        def step(t, carry):
            s = s_sc[...]
            sa_e = seg1(s * bc(kk_ref, t))
            s = s * bc(w_ref, t) - sa_e * bc(b_ref, t) + column(v_ref, t) * bc(k2_ref, t)
            s_sc[...] = s
            st_ref[t] = s
            return carry

        def readout(t, carry):
            put_diag(y_ref, t, seg1(st_ref[t] * bc(r_ref, t)))
            return carry

        lax.fori_loop(0, tb, step, 0, unroll=8)
        lax.fori_loop(0, tb, readout, 0, unroll=8)

    row = pl.BlockSpec((tb, nt, 128), lambda i: (i, 0, 0))
    return pl.pallas_call(
        body, name="rwkv_scan_fwd", grid=(S // tb,),
        in_specs=[row] * 6,
        out_specs=[row, pl.BlockSpec((tb, nt * RW_HD, 128), lambda i: (i, 0, 0))],
        out_shape=[jax.ShapeDtypeStruct((S, nt, 128), F32), jax.ShapeDtypeStruct((S, nt * RW_HD, 128), F32)],
        scratch_shapes=[pltpu.VMEM((nt * RW_HD, 128), F32)],
        compiler_params=_cparams(("arbitrary",)),
    )(r, w, k2, v, kk, b)


def rwkv_scan_backward(r, w, k2, v, kk, b, states, dy):
    S, nt, _ = r.shape
    tb = min(SCAN_BLOCK, S)
    nb = S // tb

    def body(r_ref, w_ref, k2_ref, v_ref, kk_ref, b_ref, st_ref, edge_ref, dy_ref,
             dr_ref, dw_ref, dk2_ref, dv_ref, dkk_ref, db_ref, ds_sc, before_sc):
        @pl.when(pl.program_id(0) == 0)
        def _():
            ds_sc[...] = jnp.zeros_like(ds_sc)

        before_sc[...] = jnp.where(pl.program_id(0) == nb - 1, 0.0, edge_ref[0])

        bc, seg1, column, put_diag, put_colsum = _scan_helpers(nt)

        def step(i, carry):
            t = tb - 1 - i
            s_prev = jnp.where(t == 0, before_sc[...], st_ref[jnp.maximum(t - 1, 0)])
            r_e, w_e, k2_e, kk_e, b_e = (bc(ref, t) for ref in (r_ref, w_ref, k2_ref, kk_ref, b_ref))
            v_e = column(v_ref, t)
            sa_e = seg1(s_prev * kk_e)
            dy_e = column(dy_ref, t)
            put_colsum(dr_ref, t, st_ref[t] * dy_e)
            ds = ds_sc[...] + dy_e * r_e
            put_colsum(dw_ref, t, ds * s_prev)
            nsa_e = seg1(ds * b_e)
            put_colsum(db_ref, t, ds * sa_e, -1.0)
            put_diag(dv_ref, t, seg1(ds * k2_e))
            put_colsum(dk2_ref, t, ds * v_e)
            put_colsum(dkk_ref, t, s_prev * nsa_e, -1.0)
            ds_sc[...] = ds * w_e - nsa_e * kk_e
            return carry

        lax.fori_loop(0, tb, step, 0, unroll=8)

    row = pl.BlockSpec((tb, nt, 128), lambda i: (nb - 1 - i, 0, 0))
    return pl.pallas_call(
        body, name="rwkv_scan_bwd", grid=(nb,),
        in_specs=[row] * 6 + [pl.BlockSpec((tb, nt * RW_HD, 128), lambda i: (nb - 1 - i, 0, 0)),
                              pl.BlockSpec((1, nt * RW_HD, 128), lambda i: (jnp.maximum((nb - 1 - i) * tb - 1, 0), 0, 0)), row],
        out_specs=[row] * 6,
        out_shape=[jax.ShapeDtypeStruct((S, nt, 128), F32)] * 6,
        scratch_shapes=[pltpu.VMEM((nt * RW_HD, 128), F32), pltpu.VMEM((nt * RW_HD, 128), F32)],
        compiler_params=_cparams(("arbitrary",)),
    )(r, w, k2, v, kk, b, states, states, dy)


def _edge_spec(width, col_block, tr, n_rows, after):
    last = n_rows // 8 - 1
    if after:
        return pl.BlockSpec((8, width), lambda i: (jnp.minimum((i + 1) * (tr // 8), last), col_block))
    return pl.BlockSpec((8, width), lambda i: (jnp.maximum(i * (tr // 8) - 1, 0), col_block))


def _shifted_prev(p, prev8):
    first = jnp.where(pl.program_id(0) == 0, 0.0, prev8[7:8, :])
    rows = lax.broadcasted_iota(jnp.int32, p.shape, 0)
    return jnp.where(rows == 0, first, pltpu.roll(p, 1, axis=0))


def token_shift_forward(p, mu_win, win, tr):
    def fn(pw, prev8, mu):
        return pw + mu * (_shifted_prev(pw, prev8) - pw)

    return rowwise("token_shift_fwd", fn, [_cols(p, win, 0)], [mu_win], [(win, F32)], [], tr,
                   extra_specs=[(p, _edge_spec(win, 0, tr, p.shape[0], False))])[0]


def token_shift_backward(p, dxs, da_gla, mu_win, win, a_off, tr):
    S = p.shape[0]
    n = S // min(tr, S)

    def fn(pw, dx, da, prev8, next8, mu):
        trr = pw.shape[0]
        last = jnp.where(pl.program_id(0) == n - 1, 0.0, next8[0:1, :])
        rows = lax.broadcasted_iota(jnp.int32, dx.shape, 0)
        dnext = jnp.where(rows == trr - 1, last, pltpu.roll(dx, trr - 1, axis=0))
        dp = (1.0 - mu) * dx + mu * dnext
        dp = jnp.concatenate([dp[:, :a_off], dp[:, a_off:a_off + LORA_PAD] + da, dp[:, a_off + LORA_PAD:]], axis=1)
        dmu = jnp.sum(dx * (_shifted_prev(pw, prev8) - pw), axis=0, keepdims=True)
        return dp, dmu

    return rowwise("token_shift_bwd", fn, [_cols(p, win, 0), _cols(dxs), _cols(da_gla)], [mu_win],
                   [(win, BF16)], [(1, win)], tr,
                   extra_specs=[(p, _edge_spec(win, 0, tr, S, False)), (dxs, _edge_spec(win, 0, tr, S, True))])


def _adamw_math(w, g, m, v):
    m = ADAM_B1 * m + (1.0 - ADAM_B1) * g
    v = ADAM_B2 * v + (1.0 - ADAM_B2) * (g * g)
    m_hat = m / (1.0 - ADAM_B1 ** ADAM_STEP)
    v_hat = v / (1.0 - ADAM_B2 ** ADAM_STEP)
    delta = -ADAM_LR * (m_hat / (jnp.sqrt(v_hat) + ADAM_EPS) + ADAM_WD * w)
    return delta, m, v


def adamw_small(ws, gs, ms, vs):
    n = len(ws)

    def body(*refs):
        for i in range(n):
            d, m, v = _adamw_math(refs[i][...], refs[n + i][...], refs[2 * n + i][...], refs[3 * n + i][...])
            refs[4 * n + i][...] = d
            refs[5 * n + i][...] = m
            refs[6 * n + i][...] = v

    shapes = [jax.ShapeDtypeStruct(w.shape, F32) for w in ws]
    outs = pl.pallas_call(body, name="adamw_small", out_shape=shapes * 3, compiler_params=_cparams())(*ws, *gs, *ms, *vs)
    return outs[:n], outs[n:2 * n], outs[2 * n:]


def _place():
    x, y, c = lax.axis_index("x"), lax.axis_index("y"), lax.axis_index("c")
    chips = [(1 - x, y), (x, 1 - y), (1 - x, 1 - y)]
    return x, y, c, chips


def _full_shape(kind, r, c):
    return {"col": (r, 4 * c), "row": (4 * r, c), "slab": (4, r, c)}[kind]


def _slab(ref, kind, k, r, c, half=None):
    n, off = (r, 0) if half is None else (r // 2, half * (r // 2))
    if kind == "col":
        return ref.at[pl.ds(off, n), pl.ds(k * c, c)]
    if kind == "row":
        return ref.at[pl.ds(k * r + off, n), :]
    return ref.at[k, pl.ds(off, n), :]


def _remote(src, dst, sems, idx, to):
    return pltpu.make_async_remote_copy(src_ref=src, dst_ref=dst, send_sem=sems[0].at[idx], recv_sem=sems[1].at[idx],
                                        device_id=to, device_id_type=MESH_IDS)


class CommJob:
    def __init__(self, operands, out_shapes, scratch, start, finish, aliases=None):
        self.operands, self.out_shapes, self.scratch, self.start, self.finish = operands, out_shapes, scratch, start, finish
        self.aliases = aliases or {}


def device_gather_job(vec, zeros8):
    flips = [(fx, fy, fc) for fx in (0, 1) for fy in (0, 1) for fc in (0, 1) if (fx, fy, fc) != (0, 0, 0)]

    def sends(srcs, outs, sems):
        x, y, c, _ = _place()
        return [_remote(srcs[0], outs[0].at[4 * x + 2 * y + c], sems, (j,), (x ^ fx, y ^ fy, c ^ fc))
                for j, (fx, fy, fc) in enumerate(flips)]

    def start(srcs, outs, sems):
        for cp in sends(srcs, outs, sems):
            cp.start()

    def finish(srcs, outs, sems):
        x, y, c, _ = _place()
        for j, (fx, fy, fc) in enumerate(flips):
            blk = outs[0].at[4 * (x ^ fx) + 2 * (y ^ fy) + (c ^ fc)]
            _remote(blk, blk, sems, (j,), (x, y, c)).wait_recv()
        for cp in sends(srcs, outs, sems):
            cp.wait_send()

    dma = pltpu.SemaphoreType.DMA
    return CommJob([vec, zeros8], [jax.ShapeDtypeStruct(zeros8.shape, F32)], [dma((7,)), dma((7,))], start, finish, {1: 0})


def sum_devices(gathered, own):
    _, R, C = gathered.shape
    tr = _pick(R, 2048, 8)

    def body(g_ref, own_ref, o_ref):
        me = 4 * lax.axis_index("x") + 2 * lax.axis_index("y") + lax.axis_index("c")
        acc = jnp.where(me == 0, own_ref[...], g_ref[0])
        for k in range(1, 8):
            acc = acc + jnp.where(me == k, own_ref[...], g_ref[k])
        o_ref[...] = acc

    return pl.pallas_call(body, name="sum_devices", grid=(R // tr,),
                          in_specs=[pl.BlockSpec((8, tr, C), lambda i: (0, i, 0)), pl.BlockSpec((tr, C), lambda i: (i, 0))],
                          out_specs=pl.BlockSpec((tr, C), lambda i: (i, 0)), out_shape=jax.ShapeDtypeStruct((R, C), F32),
                          compiler_params=_cparams(("parallel",)))(gathered, own)


def gather_job(big, small=()):
    big, small = list(big), list(small)
    nb, ns = len(big), len(small)
    meta = [(kind, *a.shape) for a, kind in big + small]

    def sends(srcs, outs, sems):
        own_s, own_r, ici_s, ici_r, _, _, sm_s, sm_r = sems
        x, y, c, chips = _place()
        me, sib = 2 * x + y, (x, y, 1 - c)
        cps = []
        for a in range(nb):
            kind, r, cc = meta[a]
            for j, chip in enumerate(chips):
                cps.append(_remote(srcs[a].at[pl.ds(c * (r // 2), r // 2)], _slab(outs[a], kind, me, r, cc, c),
                                   (ici_s, ici_r), (a, j), (*chip, c)))
        for a in range(nb):
            kind, r, cc = meta[a]
            cps.append(_remote(srcs[a], _slab(outs[a], kind, me, r, cc), (own_s, own_r), (a,), sib))
        for s in range(ns):
            kind, r, cc = meta[nb + s]
            for t, to in enumerate([sib] + [(*chip, c) for chip in chips]):
                cps.append(_remote(srcs[nb + s], _slab(outs[nb + s], kind, me, r, cc), (sm_s, sm_r), (s, t), to))
        return cps

    def start(srcs, outs, sems):
        for cp in sends(srcs, outs, sems):
            cp.start()

    def finish(srcs, outs, sems):
        own_s, own_r, ici_s, ici_r, fwd_s, fwd_r, sm_s, sm_r = sems
        x, y, c, chips = _place()
        me, sib = 2 * x + y, (x, y, 1 - c)
        cids = [2 * chip[0] + chip[1] for chip in chips]
        hands = []
        for a in range(nb):
            kind, r, cc = meta[a]
            for j in range(3):
                blk = _slab(outs[a], kind, cids[j], r, cc, c)
                _remote(blk, blk, (ici_s, ici_r), (a, j), sib).wait_recv()
                hands.append(_remote(blk, blk, (fwd_s, fwd_r), (a, j), sib))
                hands[-1].start()
        for a in range(nb):
            kind, r, cc = meta[a]
            for j in range(3):
                blk = _slab(outs[a], kind, cids[j], r, cc, 1 - c)
                _remote(blk, blk, (fwd_s, fwd_r), (a, j), sib).wait_recv()
            blk = _slab(outs[a], kind, me, r, cc)
            _remote(blk, blk, (own_s, own_r), (a,), sib).wait_recv()
        for s in range(ns):
            kind, r, cc = meta[nb + s]
            for t, frm in enumerate([me] + cids):
                blk = _slab(outs[nb + s], kind, frm, r, cc)
                _remote(blk, blk, (sm_s, sm_r), (s, t), sib).wait_recv()
        for cp in sends(srcs, outs, sems) + hands:
            cp.wait_send()

    dma = pltpu.SemaphoreType.DMA
    nb1, ns1 = max(nb, 1), max(ns, 1)
    return CommJob([a for a, _ in big + small],
                   [jax.ShapeDtypeStruct(_full_shape(kind, r, cc), BF16) for (kind, r, cc) in meta],
                   [dma((nb1,)), dma((nb1,)), dma((nb1, 3)), dma((nb1, 3)), dma((nb1, 3)), dma((nb1, 3)),
                    dma((ns1, 4)), dma((ns1, 4))], start, finish)


def run_job(name, job):
    n_in, n_out = len(job.operands), len(job.out_shapes)

    def body(*refs):
        ins, outs, sems = refs[:n_in], refs[n_in:n_in + n_out], refs[n_in + n_out:]
        job.start(ins, outs, sems)
        job.finish(ins, outs, sems)

    return pl.pallas_call(body, name=name, in_specs=[HBM_SPEC] * n_in, out_specs=[HBM_SPEC] * n_out,
                          out_shape=job.out_shapes, scratch_shapes=job.scratch)(*job.operands)


def sibling_swap(name, arrays):
    n = len(arrays)

    def body(*refs):
        srcs, outs, sems = refs[:n], refs[n:2 * n], refs[2 * n:]
        x, y, c, _ = _place()
        cps = [_remote(srcs[a], outs[a], sems, (a,), (x, y, 1 - c)) for a in range(n)]
        for cp in cps:
            cp.start()
        for cp in cps:
            cp.wait_recv()
        for cp in cps:
            cp.wait_send()

    return pl.pallas_call(
        body, name=name, in_specs=[HBM_SPEC] * n, out_specs=[HBM_SPEC] * n,
        out_shape=[jax.ShapeDtypeStruct(a.shape, a.dtype) for a in arrays],
        scratch_shapes=[pltpu.SemaphoreType.DMA((n,)), pltpu.SemaphoreType.DMA((n,))],
    )(*arrays)


def exchange_job(sums):
    n = len(sums)

    def sends(srcs, outs, sems):
        ici_s, ici_r, sib_s, sib_r = sems
        x, y, c, chips = _place()
        me, sib = 2 * x + y, (x, y, 1 - c)
        cps = []
        for a, (_, kind, r, cc) in enumerate(sums):
            for j, chip in enumerate(chips):
                cid = 2 * chip[0] + chip[1]
                cps.append(_remote(_slab(srcs[a], kind, cid, r, cc, c), outs[a].at[me], (ici_s, ici_r), (a, j), (*chip, c)))
            cps.append(_remote(_slab(srcs[a], kind, me, r, cc, 1 - c), outs[a].at[me], (sib_s, sib_r), (a,), sib))
        return cps

    def start(srcs, outs, sems):
        for cp in sends(srcs, outs, sems):
            cp.start()

    def finish(srcs, outs, sems):
        ici_s, ici_r, sib_s, sib_r = sems
        x, y, c, chips = _place()
        me, sib = 2 * x + y, (x, y, 1 - c)
        for a in range(n):
            for j, chip in enumerate(chips):
                blk = outs[a].at[2 * chip[0] + chip[1]]
                _remote(blk, blk, (ici_s, ici_r), (a, j), sib).wait_recv()
            _remote(outs[a].at[me], outs[a].at[me], (sib_s, sib_r), (a,), sib).wait_recv()
        for cp in sends(srcs, outs, sems):
            cp.wait_send()

    dma = pltpu.SemaphoreType.DMA
    return CommJob([s[0] for s in sums], [jax.ShapeDtypeStruct((4, r // 2, cc), BF16) for (_, _, r, cc) in sums],
                   [dma((n, 3)), dma((n, 3)), dma((n,)), dma((n,))], start, finish)


def allreduce_small(vec):
    R, C = vec.shape

    def body(src, out, gathered, send_sems, recv_sems):
        x, y, c, _ = _place()
        me = 4 * x + 2 * y + c
        gathered[me] = src[...]
        peers = [(fx, fy, fc) for fx in (0, 1) for fy in (0, 1) for fc in (0, 1) if (fx, fy, fc) != (0, 0, 0)]
        sends = []
        for j, (fx, fy, fc) in enumerate(peers):
            to = (x ^ fx, y ^ fy, c ^ fc)
            cp = pltpu.make_async_remote_copy(
                src_ref=src, dst_ref=gathered.at[me], send_sem=send_sems.at[j], recv_sem=recv_sems.at[j],
                device_id=to, device_id_type=MESH_IDS)
            cp.start()
            sends.append(cp)
        for j, (fx, fy, fc) in enumerate(peers):
            frm = 4 * (x ^ fx) + 2 * (y ^ fy) + (c ^ fc)
            pltpu.make_async_remote_copy(
                src_ref=src, dst_ref=gathered.at[frm], send_sem=send_sems.at[j], recv_sem=recv_sems.at[j],
                device_id=(x, y, c), device_id_type=MESH_IDS).wait_recv()
        for cp in sends:
            cp.wait_send()
        acc = gathered[0]
        for k in range(1, 8):
            acc = acc + gathered[k]
        out[...] = acc

    vm = pl.BlockSpec(memory_space=pltpu.VMEM)
    return pl.pallas_call(
        body, name="allreduce_small", in_specs=[vm], out_specs=vm,
        out_shape=jax.ShapeDtypeStruct((R, C), F32),
        scratch_shapes=[pltpu.VMEM((8, R, C), F32), pltpu.SemaphoreType.DMA((7,)), pltpu.SemaphoreType.DMA((7,))],
        compiler_params=_cparams(),
    )(vec)


def pair_sum(name, mine, theirs):
    rows, cols = mine.shape
    tr = _pick(rows, max(16, (1 << 20) // cols), 16)
    return rowwise(name, lambda a, b: a.astype(F32) + b.astype(F32), [_cols(mine), _cols(theirs)], [], [(cols, BF16)], [], tr)[0]


def chip_sum(name, rb):
    _, rh, C = rb.shape
    tr = _pick(rh, max(16, (1 << 19) // C), 16)

    def body(r_ref, o_ref):
        acc = r_ref[0].astype(F32)
        for k in range(1, 4):
            acc = acc + r_ref[k].astype(F32)
        o_ref[...] = acc

    return pl.pallas_call(body, name=name, grid=(rh // tr,),
                          in_specs=[pl.BlockSpec((4, tr, C), lambda i: (0, i, 0))],
                          out_specs=pl.BlockSpec((tr, C), lambda i: (i, 0)),
                          out_shape=jax.ShapeDtypeStruct((rh, C), F32),
                          compiler_params=_cparams(("parallel",)))(rb)


def adamw_halves(name, w, mine, theirs, m, v):
    rows, cols = w.shape
    tr = _pick(rows // 2, max(8, (1 << 19) // cols), 8)
    nbh = rows // 2 // tr
    full = pl.BlockSpec((tr, cols), lambda i: (i, 0))
    half = pl.BlockSpec((tr, cols), lambda i: (i % nbh, 0))

    def body(w_ref, a_ref, b_ref, m_ref, v_ref, g_out, d_out, m_out, v_out):
        is_mine = (pl.program_id(0) // nbh) == lax.axis_index("c")
        g = jnp.where(is_mine, a_ref[...], b_ref[...])
        d, mn, vn = _adamw_math(w_ref[...], g, m_ref[...], v_ref[...])
        g_out[...] = g
        d_out[...] = d
        m_out[...] = mn
        v_out[...] = vn

    return pl.pallas_call(body, name="adamw_" + name, grid=(rows // tr,), in_specs=[full, half, half, full, full],
                          out_specs=[full] * 4, out_shape=[jax.ShapeDtypeStruct((rows, cols), F32)] * 4,
                          compiler_params=_cparams(("parallel",)))(w, mine, theirs, m, v)


BIG = {"ffn1_wg": "col", "ffn1_wu": "col", "ffn1_wd": "row", "w_in": "slab", "w_branch": "row", "w_out": "row",
       "ffn2_wg": "col", "ffn2_wu": "col", "ffn2_wd": "row"}
LORA = ["gla_w_a2", "rwkv_w_w2", "rwkv_w_a2", "rwkv_w_g2"]
REPLICATED = ["ffn1_norm", "mix_norm", "gla_b_a", "gla_gn_w", "rwkv_mu", "rwkv_w0", "rwkv_a0", "rwkv_k_k", "rwkv_k_a",
              "rwkv_r_k", "rwkv_lnx_w", "rwkv_lnx_b", "gate_b", "ffn2_norm", "final_norm"]
WEIGHTS = ["ffn1_norm", "ffn1_wg", "ffn1_wu", "ffn1_wd", "mix_norm", "w_in", "gla_w_a2", "gla_b_a", "gla_gn_w", "rwkv_mu",
           "rwkv_w0", "rwkv_w_w2", "rwkv_a0", "rwkv_w_a2", "rwkv_w_g2", "rwkv_k_k", "rwkv_k_a", "rwkv_r_k", "rwkv_lnx_w",
           "rwkv_lnx_b", "gate_b", "w_branch", "w_out", "ffn2_norm", "ffn2_wg", "ffn2_wu", "ffn2_wd", "final_norm"]


def kernel(x, ffn1_norm, ffn1_wg, ffn1_wu, ffn1_wd, mix_norm, w_in, gla_w_a2, gla_b_a, gla_gn_w, rwkv_mu, rwkv_w0, rwkv_w_w2, rwkv_a0, rwkv_w_a2, rwkv_w_g2, rwkv_k_k, rwkv_k_a, rwkv_r_k, rwkv_lnx_w, rwkv_lnx_b, gate_b, w_branch, w_out, ffn2_norm, ffn2_wg, ffn2_wu, ffn2_wd, final_norm, loss_target, m_ffn1_norm, m_ffn1_wg, m_ffn1_wu, m_ffn1_wd, m_mix_norm, m_w_in, m_gla_w_a2, m_gla_b_a, m_gla_gn_w, m_rwkv_mu, m_rwkv_w0, m_rwkv_w_w2, m_rwkv_a0, m_rwkv_w_a2, m_rwkv_w_g2, m_rwkv_k_k, m_rwkv_k_a, m_rwkv_r_k, m_rwkv_lnx_w, m_rwkv_lnx_b, m_gate_b, m_w_branch, m_w_out, m_ffn2_norm, m_ffn2_wg, m_ffn2_wu, m_ffn2_wd, m_final_norm, v_ffn1_norm, v_ffn1_wg, v_ffn1_wu, v_ffn1_wd, v_mix_norm, v_w_in, v_gla_w_a2, v_gla_b_a, v_gla_gn_w, v_rwkv_mu, v_rwkv_w0, v_rwkv_w_w2, v_rwkv_a0, v_rwkv_w_a2, v_rwkv_w_g2, v_rwkv_k_k, v_rwkv_k_a, v_rwkv_r_k, v_rwkv_lnx_w, v_rwkv_lnx_b, v_gate_b, v_w_branch, v_w_out, v_ffn2_norm, v_ffn2_wg, v_ffn2_wu, v_ffn2_wd, v_final_norm):
    args = dict(locals())
    wts = {n: args[n] for n in WEIGHTS}
    moms = {n: args["m_" + n] for n in WEIGHTS}
    vars_ = {n: args["v_" + n] for n in WEIGHTS}

    xs = x[0]
    tgt = loss_target[0]
    S, D = xs.shape
    FF = ffn1_wd.shape[1] * 4
    gheads = gla_b_a.shape[-1] // GLA_DK
    GQ, GV = gheads * GLA_DK, gheads * GLA_DV
    rheads = rwkv_r_k.shape[1]
    RW = rheads * RW_HD
    NT = RW // 128
    lo_g = gla_w_a2.shape[1]
    lo_w = rwkv_w_w2.shape[1]
    lo_a = rwkv_w_a2.shape[1]
    assert rwkv_w_g2.shape[1] == GATE_LORA and RW % 128 == 0

    ow = rw_window(RW)
    WIN = -(-ow["used"] // (2 * D)) * (2 * D)
    lay = dict(gate=WIN, v=WIN + 2 * D, r=WIN + 2 * D + GV, q=WIN + 2 * D + 2 * GV, k=WIN + 2 * D + 2 * GV + GQ, a=ow["a"])
    DP = lay["k"] + GQ
    DIN = w_in.shape[-1] * 4
    o_sizes = [GQ, GQ, GV, GV, lo_g, RW, RW, RW, lo_w, lo_a, GATE_LORA, 2 * D]
    o_offs = [sum(o_sizes[:i]) for i in range(len(o_sizes))]
    assert o_offs[-1] + o_sizes[-1] == DIN
    p_offs = [lay["q"], lay["k"], lay["v"], lay["r"], ow["a"], ow["rr"], ow["rk"], ow["rv"], ow["wd"], ow["ad"], ow["gd"], lay["gate"]]

    def to_padded(w):
        order = sorted(range(len(o_sizes)), key=lambda i: p_offs[i])
        parts, pos = [], 0
        for i in order:
            if p_offs[i] > pos:
                parts.append(jnp.zeros((w.shape[0], p_offs[i] - pos), w.dtype))
            parts.append(w[:, o_offs[i]:o_offs[i] + o_sizes[i]])
            pos = p_offs[i] + o_sizes[i]
        if pos < DP:
            parts.append(jnp.zeros((w.shape[0], DP - pos), w.dtype))
        return jnp.concatenate(parts, axis=1)

    def from_padded(w):
        return jnp.concatenate([w[:, p_offs[i]:p_offs[i] + o_sizes[i]] for i in range(len(o_sizes))], axis=1)

    def pad_rows(w, rows):
        return jnp.pad(w, ((0, rows - w.shape[0]), (0, 0)))

    mu = rwkv_mu[0]
    mu_parts = {"rr": mu[0:RW], "rk": mu[RW:2 * RW], "rv": mu[2 * RW:3 * RW], "wd": mu[3 * RW:3 * RW + lo_w],
                "ad": mu[3 * RW + lo_w:3 * RW + lo_w + lo_a], "gd": mu[3 * RW + lo_w + lo_a:]}
    mu_win = jnp.zeros((WIN,), F32)
    for key, val in mu_parts.items():
        mu_win = lax.dynamic_update_slice(mu_win, val, (ow[key],))
    mu_win = mu_win.reshape(1, WIN)

    shard_shapes = {n: wts[n].shape[1:] for n in list(BIG) + LORA}
    W = {}

    def shard(n):
        return (wts[n][0].astype(BF16), BIG[n])

    def mm_gather(a, b, out_dtype, name, gather, lora=(), **epilogue):
        out, got = matmul(a, b, "nn", out_dtype, name, **epilogue,
                          job=gather_job([shard(n) for n in gather], [(wts[n][0].astype(BF16), "col") for n in lora]))
        W.update(zip(list(gather) + list(lora), got))
        return out

    W["ffn1_wg"] = run_job("gather_ffn1_wg", gather_job([shard("ffn1_wg")]))[0]
    r_k = rwkv_r_k.reshape(1, RW)
    fin_g = final_norm.reshape(1, D)

    TR = min(128, S)
    def up_and_act(acc, g):
        return acc, f_swiglu(g.astype(F32), acc)

    h1 = rowwise("rms1", lambda a, g: f_rms(a, g), [_cols(xs)], [ffn1_norm], [(D, BF16)], [], TR)[0]
    g1 = mm_gather(h1, W["ffn1_wg"], BF16, "ffn1_g", ["ffn1_wu"])
    u1, act1 = mm_gather(h1, W["ffn1_wu"], [BF16, BF16], "ffn1_u", ["ffn1_wd"], extras=[g1], epilogue=up_and_act)
    f1 = mm_gather(act1, W["ffn1_wd"], F32, "ffn1_d", ["w_in"], LORA)
    w_in_p = to_padded(W["w_in"].transpose(1, 0, 2).reshape(D, DIN))
    gla_a2_p = pad_rows(W["gla_w_a2"], LORA_PAD)
    w_w2_p = pad_rows(W["rwkv_w_w2"], LORA_PAD)
    w_a2_p = pad_rows(W["rwkv_w_a2"], LORA_PAD)
    w_g2 = W["rwkv_w_g2"]

    def res_rms(coef):
        def fn(a, f, g):
            x1 = a + coef * f
            return x1, f_rms(x1, g)
        return fn

    x1, h2 = rowwise("res_rms_mix", res_rms(0.5), [_cols(xs), _cols(f1)], [mix_norm], [(D, F32), (D, BF16)], [], TR)
    p = mm_gather(h2, w_in_p, F32, "w_in", ["w_branch", "w_out", "ffn2_wg"])
    wb_g, wb_r = W["w_branch"][:GV], W["w_branch"][GV:]

    o_gla, gla_states = gla_forward(p, lay, gla_a2_p, gla_b_a, gla_gn_w, gheads)
    xsh = token_shift_forward(p, mu_win, WIN, TR)
    pre_consts = [rwkv_w0, w_w2_p, rwkv_a0, w_a2_p, w_g2, rwkv_k_k, rwkv_k_a]
    pre_fn = functools.partial(f_rw_pre, rw=RW)

    def pre_f32(xw, w0, ww, a0, wa, wg_, kk_, ka_):
        return pre_fn(xw, w0, ww.astype(F32), a0, wa.astype(F32), wg_.astype(F32), kk_, ka_)

    r_, dec_, k2_, v_, kk_, b_, g_ = rowwise("rw_pre", pre_f32, [_cols(xsh)], pre_consts, [(RW, F32)] * 7, [], 128)

    def tiles(a):
        return a.reshape(S, NT, 128)

    y_t, rw_states = rwkv_scan_forward(*(tiles(a) for a in (r_, dec_, k2_, v_, kk_, b_)))
    y_ = y_t.reshape(S, RW)
    post_consts = [rwkv_lnx_w, rwkv_lnx_b, r_k]
    o_rw = rowwise("rw_post", f_rw_post, [_cols(a) for a in (y_, r_, k2_, v_, g_)], post_consts, [(RW, BF16)], [], TR)[0]

    yg = matmul(o_gla, wb_g, "nn", F32, "branch_gla")
    yr = matmul(o_rw, wb_r, "nn", F32, "branch_rw")
    merge_fn = functools.partial(f_merge, d=D)
    merged = rowwise("merge", merge_fn, [_cols(p, 2 * D, lay["gate"]), _cols(yg), _cols(yr)], [gate_b], [(D, BF16)], [], TR)[0]
    mix = matmul(merged, W["w_out"], "nn", F32, "w_out")
    x2, h3 = rowwise("res_rms_ffn2", res_rms(1.0), [_cols(x1), _cols(mix)], [ffn2_norm], [(D, F32), (D, BF16)], [], TR)
    g3 = mm_gather(h3, W["ffn2_wg"], BF16, "ffn2_g", ["ffn2_wu"])
    u3, act3 = mm_gather(h3, W["ffn2_wu"], [BF16, BF16], "ffn2_u", ["ffn2_wd"], extras=[g3], epilogue=up_and_act)
    f3 = matmul(act3, W["ffn2_wd"], "nn", F32, "ffn2_d")

    def final_fn(a, f, t, g):
        def loss_of(a, f, g):
            yv = f_rms(a + 0.5 * f, g)
            return 0.5 * jnp.sum(jnp.mean(jnp.square(yv - t), axis=-1))
        val, vjp = jax.vjp(loss_of, a, f, g)
        da, df, dg = vjp(jnp.ones((), F32))
        return da, df, jnp.full((1, 128), val, F32), dg

    dx2, df3, loss_acc, d_final = rowwise("final_loss", final_fn, [_cols(x2), _cols(f3), _cols(tgt)], [fin_g],
                                          [(D, F32), (D, BF16)], [(1, 128), (1, D)], TR)
    grads = {"final_norm": d_final.reshape(D)}

    received = {}

    def pair_up(n, dw):
        r, cc = shard_shapes[n]
        flat = (4 * r, cc) if BIG[n] == "slab" else dw.shape
        theirs = sibling_swap("swap_" + n, [dw])[0]
        return n, (pair_sum("pair_" + n, dw.reshape(flat), theirs.reshape(flat)).reshape(dw.shape), BIG[n], r, cc)

    def mm_exchange(a, b, mode, out_dtype, name, pending):
        out, got = matmul(a, b, mode, out_dtype, name, job=exchange_job([entry for _, entry in pending]))
        received.update(zip([n for n, _ in pending], got))
        return out

    def ffn_backward(tag, h, gx, ux, act, df, wg, wu, wd, last_job=None):
        def through_act(dact, g, u):
            _, vjp = jax.vjp(f_swiglu, *f32(g, u))
            return vjp(dact)

        dgx, dux = matmul(df, wd, "nt", [BF16, BF16], tag + "_dact", extras=[gx, ux], epilogue=through_act)
        p_wd = pair_up(tag + "_wd", matmul(act, df, "tn", BF16, tag + "_dwd"))
        p_wg = pair_up(tag + "_wg", mm_exchange(h, dgx, "tn", BF16, tag + "_dwg", [p_wd]))
        p_wu = pair_up(tag + "_wu", mm_exchange(h, dux, "tn", BF16, tag + "_dwu", [p_wg]))
        dha = mm_exchange(dgx, wg, "nt", F32, tag + "_dh_g", [p_wu])
        if last_job is None:
            return dha, matmul(dux, wu, "nt", F32, tag + "_dh_u"), None
        dhb, got = matmul(dux, wu, "nt", F32, tag + "_dh_u", job=last_job)
        return dha, dhb, got

    def res_rms_bwd(name, coef, a, f, g, dx1, dha, dhb):
        def fn(a, f, dx1, dha, dhb, g):
            _, vjp = jax.vjp(res_rms(coef), a, f, g)
            return vjp((dx1, dha + dhb))

        return rowwise(name, fn, [_cols(a), _cols(f), _cols(dx1), _cols(dha), _cols(dhb)], [g],
                       [(D, F32), (D, BF16)], [(1, D)], TR)

    dh3a, dh3b, _ = ffn_backward("ffn2", h3, g3, u3, act3, df3, W["ffn2_wg"], W["ffn2_wu"], W["ffn2_wd"])
    dx1, dmix, grads["ffn2_norm"] = res_rms_bwd("res_rms_ffn2_bwd", 1.0, x1, mix, ffn2_norm, dx2, dh3a, dh3b)

    p_wo = pair_up("w_out", matmul(merged, dmix, "tn", BF16, "d_w_out"))
    dmerged = mm_exchange(dmix, W["w_out"], "nt", F32, "d_merged", [p_wo])

    def merge_bwd(gp, a, b, d, gb):
        _, vjp = jax.vjp(merge_fn, gp, a, b, gb)
        return vjp(d)

    dgate, dyg, dyr, grads["gate_b"] = rowwise(
        "merge_bwd", merge_bwd, [_cols(p, 2 * D, lay["gate"]), _cols(yg), _cols(yr), _cols(dmerged)], [gate_b],
        [(2 * D, BF16), (D, BF16), (D, BF16)], [(1, 2 * D)], TR)
    do_gla = matmul(dyg, wb_g, "nt", BF16, "d_o_gla")
    do_rw = matmul(dyr, wb_r, "nt", F32, "d_o_rw")
    p_wb = pair_up("w_branch", jnp.concatenate([matmul(o_gla, dyg, "tn", BF16, "d_wb_gla"),
                                                matmul(o_rw, dyr, "tn", BF16, "d_wb_rw")], axis=0))

    def post_bwd(yv, rv, kv, vv, gv_, d, lw, lb, rk):
        _, vjp = jax.vjp(f_rw_post, yv, rv, kv, vv, gv_, lw, lb, rk)
        return vjp(d)

    dy_, dr_p, dk2_p, dv_p, dg_p, grads["rwkv_lnx_w"], grads["rwkv_lnx_b"], d_rk = rowwise(
        "rw_post_bwd", post_bwd, [_cols(a) for a in (y_, r_, k2_, v_, g_, do_rw)], post_consts,
        [(RW, F32)] * 5, [(1, RW)] * 3, 128)
    grads["rwkv_r_k"] = d_rk.reshape(rwkv_r_k.shape[1:])

    scan_cots = rwkv_scan_backward(*(tiles(a) for a in (r_, dec_, k2_, v_, kk_, b_)), rw_states, tiles(dy_))
    dr_s, dw_s, dk2_s, dv_s, dkk_s, db_s = (a.reshape(S, RW) for a in scan_cots)

    def pre_bwd(xw, c0, c1, c2, c3, c4, c5, c6, c7, c8, c9, w0, ww, a0, wa, wg_, kk_c, ka_c):
        _, vjp = jax.vjp(pre_fn, xw, w0, ww.astype(F32), a0, wa.astype(F32), wg_.astype(F32), kk_c, ka_c)
        return vjp((c0 + c6, c1, c2 + c7, c3 + c8, c4, c5, c9))

    dxsh, grads["rwkv_w0"], d_ww2, grads["rwkv_a0"], d_wa2, d_wg2, grads["rwkv_k_k"], grads["rwkv_k_a"] = rowwise(
        "rw_pre_bwd", pre_bwd,
        [_cols(xsh)] + [_cols(a) for a in (dr_s, dw_s, dk2_s, dv_s, dkk_s, db_s, dr_p, dk2_p, dv_p, dg_p)], pre_consts,
        [(WIN, F32)], [(1, RW), w_w2_p.shape, (1, RW), w_a2_p.shape, w_g2.shape, (1, RW), (1, RW)], 128)
    grads["rwkv_w_w2"], grads["rwkv_w_a2"], grads["rwkv_w_g2"] = d_ww2[:lo_w], d_wa2[:lo_a], d_wg2

    dq, dk, dv, dr, da, d_ga2, grads["gla_b_a"], grads["gla_gn_w"] = gla_backward(
        p, lay, gla_states, do_gla, gla_a2_p, gla_b_a, gla_gn_w, gheads)
    grads["gla_w_a2"] = d_ga2[:lo_g]

    dpw, dmu_win = token_shift_backward(p, dxsh, da, mu_win, WIN, ow["a"], TR)
    dmu = dmu_win[0]
    grads["rwkv_mu"] = jnp.concatenate([dmu[ow[k_]:ow[k_] + mu_parts[k_].shape[0]] for k_ in ("rr", "rk", "rv", "wd", "ad", "gd")]).reshape(1, -1)

    dp = jnp.concatenate([dpw, dgate, dv, dr, dq, dk], axis=1)
    d_w_in = from_padded(mm_exchange(h2, dp, "tn", BF16, "d_w_in", [p_wb]))
    p_wi = pair_up("w_in", d_w_in.reshape(D, 4, shard_shapes["w_in"][1]).transpose(1, 0, 2))
    dh2 = mm_exchange(dp, w_in_p, "nt", F32, "d_h2", [p_wi])
    zeros_d = jnp.zeros_like(dh2)
    dx0, df1, grads["mix_norm"] = res_rms_bwd("res_rms_mix_bwd", 0.5, xs, f1, mix_norm, dx1, dh2, zeros_d)

    early = [n for n in REPLICATED if n != "ffn1_norm"] + LORA
    early_flat = jnp.concatenate([grads[n].reshape(-1) for n in early])
    early_rows = -(-early_flat.shape[0] // 1024) * 8
    early_vec = jnp.pad(early_flat, (0, early_rows * 128 - early_flat.shape[0])).reshape(early_rows, 128)
    dh1a, dh1b, (early_all,) = ffn_backward(
        "ffn1", h1, g1, u1, act1, df1, W["ffn1_wg"], W["ffn1_wu"], W["ffn1_wd"],
        last_job=device_gather_job(early_vec, jnp.zeros((8, early_rows, 128), F32)))

    def rms1_bwd(a, dha, dhb, dxa, g):
        _, vjp = jax.vjp(f_rms, a, g)
        da_, dg_ = vjp(dha + dhb)
        return da_ + dxa, dg_

    grad_x, grads["ffn1_norm"] = rowwise("rms1_bwd", rms1_bwd, [_cols(xs), _cols(dh1a), _cols(dh1b), _cols(dx0)],
                                         [ffn1_norm], [(D, F32)], [(1, D)], TR)

    names = list(BIG)
    halves = [chip_sum("chip_sum_" + n, received[n]) for n in names]
    others = sibling_swap("sibling_join", halves)
    final_grads, delta, new_m, new_v = {}, {}, {}, {}
    for n, h, o in zip(names, halves, others):
        res = adamw_halves(n, wts[n][0], h, o, moms[n][0], vars_[n][0])
        final_grads[n], delta[n], new_m[n], new_v[n] = (a.reshape(wts[n].shape) for a in res)

    rep_sum = sum_devices(early_all, early_vec).reshape(-1)
    late = grads["ffn1_norm"].reshape(-1, 128)
    final_grads["ffn1_norm"] = allreduce_small(jnp.pad(late, ((0, -late.shape[0] % 8), (0, 0))))[:late.shape[0]].reshape(wts["ffn1_norm"].shape)
    my_chip = 2 * lax.axis_index("x") + lax.axis_index("y")
    off = 0
    for n in early:
        size = grads[n].size
        full = rep_sum[off:off + size].reshape(grads[n].shape)
        off += size
        if n in LORA:
            cc = shard_shapes[n][1]
            full = lax.dynamic_slice_in_dim(full, my_chip * cc, cc, axis=1)
        final_grads[n] = full.reshape(wts[n].shape)

    loss = lax.psum(loss_acc[0, 0], ("x", "y", "c"))

    small = REPLICATED + LORA

    def two(a):
        return a.reshape(-1, a.shape[-1])

    ds, ms_, vs_ = adamw_small([two(wts[n]) for n in small], [two(final_grads[n]) for n in small],
                               [two(moms[n]) for n in small], [two(vars_[n]) for n in small])
    for i, n in enumerate(small):
        shp = wts[n].shape
        delta[n], new_m[n], new_v[n] = ds[i].reshape(shp), ms_[i].reshape(shp), vs_[i].reshape(shp)

    return (loss, grad_x.reshape(x.shape), *[final_grads[n] for n in WEIGHTS], *[delta[n] for n in WEIGHTS],
            *[new_m[n] for n in WEIGHTS], *[new_v[n] for n in WEIGHTS])
```
